```python
import math
import jax, jax.numpy as jnp
from jax import lax
import numpy as np

D_MODEL = 2048
BATCH = 4
SEQ = 2048
DEPTH = 1
DEC_BATCH = 128
DEC_SEQ = 4
PAST_LEN = 16384
PAGE_SIZE = 128

MIX_WIDTH = D_MODEL
MLSTM_WIDTH = MIX_WIDTH // 2
MLSTM_HEADS = 4
MLSTM_HEAD_DIM = MLSTM_WIDTH // MLSTM_HEADS
MLSTM_CHUNK = 64
SSM_WIDTH = MIX_WIDTH - MLSTM_WIDTH
SSM_GROUP = 16
SSM_GROUPS = SSM_WIDTH // SSM_GROUP
SSM_STATE = 64
PLE_DIM = 256
PEER_HEADS = 8
PEER_KEYS = 128
PEER_EXPERTS = PEER_KEYS * PEER_KEYS
PEER_TOPK = 16
PEER_KEY_DIM = 128
PEER_BLOCK = 128
EPS = 1e-6

OFF_Q = 0
OFF_K = OFF_Q + MLSTM_WIDTH
OFF_V = OFF_K + MLSTM_WIDTH
OFF_O = OFF_V + MLSTM_WIDTH
OFF_I = OFF_O + MLSTM_WIDTH
OFF_F = OFF_I + MLSTM_HEADS
OFF_U = OFF_F + MLSTM_HEADS
IN_COLS = OFF_U + SSM_WIDTH

kernel_name = 'hymba_mlstm_s5_peer_decode_step'


def rmsnorm(x, g):
    x32 = x.astype(jnp.float32)
    y = x32 * lax.rsqrt(jnp.mean(x32 * x32, axis=-1, keepdims=True) + EPS)
    return (y * g.astype(jnp.float32)).astype(x.dtype)


def mlstm_chunkwise(q, k, v, ig, lf, C0, n0, m0):
    f32 = jnp.float32
    Bt, S, H, dh = q.shape
    L = math.gcd(S, MLSTM_CHUNK)
    nc = S // L

    def to_chunks(t):
        t = t.astype(f32).reshape((Bt, nc, L, H) + t.shape[3:])
        return jnp.swapaxes(jnp.moveaxis(t, 1, 0), 2, 3)

    qc, kc, vc, igc, lfc = (to_chunks(t) for t in (q, k, v, ig, lf))
    causal = jnp.tril(jnp.ones((L, L), dtype=bool))

    def step(carry, inp):
        C, n, m = carry
        qb, kb, vb, igb, lfb = inp
        b = jnp.cumsum(lfb, axis=-1)
        dmat = jnp.where(causal, b[..., :, None] - b[..., None, :] + igb[..., None, :], -jnp.inf)
        inter = b + m[..., None]
        m_t = jnp.maximum(inter, jnp.max(dmat, axis=-1))
        w_intra = jnp.exp(dmat - m_t[..., None])
        w_inter = jnp.exp(inter - m_t)
        s = jnp.einsum('bhtd,bhsd->bhts', qb, kb) * w_intra
        num = w_inter[..., None] * jnp.einsum('bhtd,bhde->bhte', qb, C) + jnp.einsum('bhts,bhse->bhte', s, vb)
        den = w_inter * jnp.einsum('bhtd,bhd->bht', qb, n) + jnp.sum(s, axis=-1)
        hb = num / jnp.maximum(jnp.abs(den), jnp.exp(-m_t))[..., None]
        b_last = b[..., -1]
        dec = b_last[..., None] - b + igb
        m_new = jnp.maximum(b_last + m, jnp.max(dec, axis=-1))
        kw = kb * jnp.exp(dec - m_new[..., None])[..., None]
        sc = jnp.exp(b_last + m - m_new)
        C_new = sc[..., None, None] * C + jnp.einsum('bhsd,bhse->bhde', kw, vb)
        n_new = sc[..., None] * n + jnp.sum(kw, axis=-2)
        return (C_new, n_new, m_new), hb

    (C1, n1, m1), hs = lax.scan(step, (C0.astype(f32), n0.astype(f32), m0.astype(f32)),
                                (qc, kc, vc, igc, lfc))
    hs = jnp.swapaxes(jnp.moveaxis(hs, 0, 1), 2, 3).reshape(Bt, S, H, dh)
    return hs, C1, n1, m1


def _complex_affine_combine(e1, e2):
    a1r, a1i, b1r, b1i = e1
    a2r, a2i, b2r, b2i = e2
    return (a2r * a1r - a2i * a1i,
            a2r * a1i + a2i * a1r,
            a2r * b1r - a2i * b1i + b2r,
            a2r * b1i + a2i * b1r + b2i)


def s5_scan(u, A_re, A_im, B_re, B_im, C_re, C_im, Dd, log_dt, s0_re, s0_im):
    f32 = jnp.float32
    u = u.astype(f32)
    A_re, A_im, B_re, B_im, C_re, C_im, Dd, log_dt, s0_re, s0_im = (
        t.astype(f32) for t in (A_re, A_im, B_re, B_im, C_re, C_im, Dd, log_dt, s0_re, s0_im))
    dt = jnp.exp(log_dt)[:, None]
    mag = jnp.exp(dt * A_re)
    ab_re = mag * jnp.cos(dt * A_im)
    ab_im = mag * jnp.sin(dt * A_im)
    den = A_re * A_re + A_im * A_im
    xr = ab_re - 1.0
    f_re = (xr * A_re + ab_im * A_im) / den
    f_im = (ab_im * A_re - xr * A_im) / den
    Bb_re = f_re[..., None] * B_re - f_im[..., None] * B_im
    Bb_im = f_re[..., None] * B_im + f_im[..., None] * B_re
    bu_re = jnp.einsum('gpc,bsgc->bsgp', Bb_re, u)
    bu_im = jnp.einsum('gpc,bsgc->bsgp', Bb_im, u)
    bu_re = bu_re.at[:, 0].add(ab_re * s0_re - ab_im * s0_im)
    bu_im = bu_im.at[:, 0].add(ab_re * s0_im + ab_im * s0_re)
    a_re = jnp.broadcast_to(ab_re, bu_re.shape)
    a_im = jnp.broadcast_to(ab_im, bu_im.shape)
    _, _, st_re, st_im = lax.associative_scan(_complex_affine_combine, (a_re, a_im, bu_re, bu_im), axis=1)
    y = (jnp.einsum('gcp,bsgp->bsgc', C_re, st_re) - jnp.einsum('gcp,bsgp->bsgc', C_im, st_im)
         + Dd * u)
    return y, st_re[:, -1], st_im[:, -1]


def peer_ffn(c, w_q, keys, U, V):
    f32 = jnp.float32
    T, D = c.shape
    half = PEER_KEY_DIM // 2
    q = (c @ w_q).astype(f32).reshape(T, PEER_HEADS, PEER_KEY_DIM)
    kf = keys.astype(f32)
    s1 = jnp.einsum('thd,hkd->thk', q[..., :half], kf[:, 0])
    s2 = jnp.einsum('thd,hkd->thk', q[..., half:], kf[:, 1])
    v1, i1 = lax.top_k(s1, PEER_TOPK)
    v2, i2 = lax.top_k(s2, PEER_TOPK)
    cand = (v1[..., :, None] + v2[..., None, :]).reshape(T, PEER_HEADS, PEER_TOPK * PEER_TOPK)
    vs, ic = lax.top_k(cand, PEER_TOPK)
    e1 = jnp.take_along_axis(i1, ic // PEER_TOPK, axis=-1)
    e2 = jnp.take_along_axis(i2, ic % PEER_TOPK, axis=-1)
    idx = e1 * PEER_KEYS + e2
    g = jax.nn.softmax(vs, axis=-1)

    nb = -(-T // PEER_BLOCK)
    pad = nb * PEER_BLOCK - T
    cp = jnp.pad(c, ((0, pad), (0, 0))).reshape(nb, PEER_BLOCK, D)
    ip = jnp.pad(idx, ((0, pad), (0, 0), (0, 0))).reshape(nb, PEER_BLOCK, PEER_HEADS, PEER_TOPK)
    gp = jnp.pad(g, ((0, pad), (0, 0), (0, 0))).reshape(nb, PEER_BLOCK, PEER_HEADS, PEER_TOPK)

    def block(args):
        xb, ib, gb = args
        ub = jnp.take(U, ib, axis=0)
        act = jax.nn.gelu(jnp.einsum('thkd,td->thk', ub, xb).astype(f32)) * gb
        vb = jnp.take(V, ib, axis=0)
        return jnp.einsum('thk,thkd->td', act.astype(V.dtype), vb).astype(c.dtype)

    out = lax.map(block, (cp, ip, gp))
    return out.reshape(nb * PEER_BLOCK, D)[:T]


def hybrid_layer(h, p, C0, n0, m0, s0r, s0i, prm):
    f32 = jnp.float32
    Bt, S, _ = h.shape
    H, dh = MLSTM_HEADS, MLSTM_HEAD_DIM
    a = rmsnorm(h, prm['norm_mix'])
    z = a @ prm['w_in']
    q = z[..., OFF_Q:OFF_K].reshape(Bt, S, H, dh)
    k = z[..., OFF_K:OFF_V].reshape(Bt, S, H, dh) * (dh ** -0.5)
    v = z[..., OFF_V:OFF_O].reshape(Bt, S, H, dh)
    o = jax.nn.sigmoid(z[..., OFF_O:OFF_I].astype(f32))
    ig = z[..., OFF_I:OFF_F].astype(f32) + prm['b_igate'].astype(f32)
    lf = jax.nn.log_sigmoid(z[..., OFF_F:OFF_U].astype(f32) + prm['b_fgate'].astype(f32))
    u = z[..., OFF_U:].reshape(Bt, S, SSM_GROUPS, SSM_GROUP)

    hm, C1, n1, m1 = mlstm_chunkwise(q, k, v, ig, lf, C0, n0, m0)
    hm = hm * lax.rsqrt(jnp.mean(hm * hm, axis=-1, keepdims=True) + EPS)
    hm = hm.reshape(Bt, S, MLSTM_WIDTH) * prm['mlstm_norm'].astype(f32) * o

    ys, s1r, s1i = s5_scan(u, prm['ssm_A_re'], prm['ssm_A_im'], prm['ssm_B_re'], prm['ssm_B_im'],
                           prm['ssm_C_re'], prm['ssm_C_im'], prm['ssm_D'], prm['ssm_log_dt'], s0r, s0i)
    ys = jax.nn.gelu(ys.reshape(Bt, S, SSM_WIDTH))
    ys = ys * jax.nn.sigmoid(ys @ prm['w_glu'].astype(f32) + prm['b_glu'].astype(f32))

    mix = jnp.concatenate([hm, ys], axis=-1).astype(h.dtype) @ prm['w_out']
    h = h + mix
    c = rmsnorm(h, prm['norm_ffn'])
    h = h + peer_ffn(c.reshape(Bt * S, D_MODEL), prm['peer_w_q'], prm['peer_keys'],
                     prm['peer_u'], prm['peer_v']).reshape(Bt, S, D_MODEL)
    gate = jax.nn.sigmoid((rmsnorm(h, prm['norm_ple']) @ prm['w_ple_gate']).astype(f32))
    e = (p @ prm['w_ple_proj']).astype(f32)
    h = h + (e * gate).astype(h.dtype)
    return h, (C1, n1, m1, s1r, s1i)


def setup_inputs(seed: int = 0) -> dict:
    key = jax.random.key(seed)
    ks = jax.random.split(key, 40)
    f32 = jnp.float32
    L = DEPTH
    G, P = SSM_GROUPS, SSM_STATE

    def nrm(i, shape, scale):
        return jax.random.normal(ks[i], shape, f32) * scale

    return {
        'x_prompt': nrm(0, (BATCH, SEQ, D_MODEL), 1.0),
        'x_sample': nrm(1, (DEC_BATCH, DEC_SEQ, D_MODEL), 1.0),
        'state_mlstm_C': nrm(2, (L, DEC_BATCH, MLSTM_HEADS, MLSTM_HEAD_DIM, MLSTM_HEAD_DIM), 0.02),
        'state_mlstm_n': nrm(3, (L, DEC_BATCH, MLSTM_HEADS, MLSTM_HEAD_DIM), 0.1),
        'state_mlstm_m': nrm(4, (L, DEC_BATCH, MLSTM_HEADS), 1.0),
        'state_ssm_re': nrm(5, (L, DEC_BATCH, G, P), 0.1),
        'state_ssm_im': nrm(6, (L, DEC_BATCH, G, P), 0.1),
        'p_prompt': nrm(7, (L, BATCH, SEQ, PLE_DIM), 1.0),
        'p_sample': nrm(8, (L, DEC_BATCH, DEC_SEQ, PLE_DIM), 1.0),
        'norm_mix': 1.0 + nrm(9, (L, D_MODEL), 0.02),
        'w_in': nrm(10, (L, D_MODEL, IN_COLS), D_MODEL ** -0.5),
        'b_igate': nrm(11, (L, MLSTM_HEADS), 0.1),
        'b_fgate': jnp.linspace(3.0, 6.0, MLSTM_HEADS, dtype=f32)[None] + nrm(12, (L, MLSTM_HEADS), 0.1),
        'mlstm_norm': 1.0 + nrm(13, (L, MLSTM_WIDTH), 0.02),
        'ssm_A_re': -0.5 + nrm(14, (L, G, P), 0.01),
        'ssm_A_im': math.pi * jnp.arange(P, dtype=f32) + nrm(15, (L, G, P), 0.01),
        'ssm_B_re': nrm(16, (L, G, P, SSM_GROUP), (2 * SSM_GROUP) ** -0.5),
        'ssm_B_im': nrm(17, (L, G, P, SSM_GROUP), (2 * SSM_GROUP) ** -0.5),
        'ssm_C_re': nrm(18, (L, G, SSM_GROUP, P), P ** -0.5),
        'ssm_C_im': nrm(19, (L, G, SSM_GROUP, P), P ** -0.5),
        'ssm_D': nrm(20, (L, G, SSM_GROUP), 1.0),
        'ssm_log_dt': jax.random.uniform(ks[21], (L, G), f32, math.log(1e-3), math.log(1e-1)),
        'w_glu': nrm(22, (L, SSM_WIDTH, SSM_WIDTH), SSM_WIDTH ** -0.5),
        'b_glu': nrm(23, (L, SSM_WIDTH), 0.02),
        'w_out': nrm(24, (L, MIX_WIDTH, D_MODEL), MIX_WIDTH ** -0.5),
        'norm_ffn': 1.0 + nrm(25, (L, D_MODEL), 0.02),
        'peer_w_q': nrm(26, (L, D_MODEL, PEER_HEADS * PEER_KEY_DIM), D_MODEL ** -0.5),
        'peer_keys': nrm(27, (L, PEER_HEADS, 2, PEER_KEYS, PEER_KEY_DIM // 2), (PEER_KEY_DIM // 2) ** -0.5),
        'peer_u': nrm(28, (L, PEER_EXPERTS, D_MODEL), D_MODEL ** -0.5),
        'peer_v': nrm(29, (L, PEER_EXPERTS, D_MODEL), PEER_HEADS ** -0.5),
        'norm_ple': 1.0 + nrm(30, (L, D_MODEL), 0.02),
        'w_ple_gate': nrm(31, (L, D_MODEL, D_MODEL), D_MODEL ** -0.5),
        'w_ple_proj': nrm(32, (L, PLE_DIM, D_MODEL), PLE_DIM ** -0.5),
        'norm_final': 1.0 + nrm(33, (D_MODEL,), 0.02),
    }


def reference(x_prompt, x_sample, state_mlstm_C, state_mlstm_n, state_mlstm_m, state_ssm_re, state_ssm_im,
              p_prompt, p_sample, norm_mix, w_in, b_igate, b_fgate, mlstm_norm, ssm_A_re, ssm_A_im,
              ssm_B_re, ssm_B_im, ssm_C_re, ssm_C_im, ssm_D, ssm_log_dt, w_glu, b_glu, w_out, norm_ffn,
              peer_w_q, peer_keys, peer_u, peer_v, norm_ple, w_ple_gate, w_ple_proj, norm_final):
    f32 = jnp.float32
    H, dh, G, P = MLSTM_HEADS, MLSTM_HEAD_DIM, SSM_GROUPS, SSM_STATE
    bp = x_prompt.shape[0]
    hp, hs = x_prompt, x_sample
    outs_p = [[] for _ in range(5)]
    outs_s = [[] for _ in range(5)]
    for l in range(DEPTH):
        prm = {
            'norm_mix': norm_mix[l], 'w_in': w_in[l], 'b_igate': b_igate[l], 'b_fgate': b_fgate[l],
            'mlstm_norm': mlstm_norm[l], 'ssm_A_re': ssm_A_re[l], 'ssm_A_im': ssm_A_im[l],
            'ssm_B_re': ssm_B_re[l], 'ssm_B_im': ssm_B_im[l], 'ssm_C_re': ssm_C_re[l], 'ssm_C_im': ssm_C_im[l],
            'ssm_D': ssm_D[l], 'ssm_log_dt': ssm_log_dt[l], 'w_glu': w_glu[l], 'b_glu': b_glu[l],
            'w_out': w_out[l], 'norm_ffn': norm_ffn[l], 'peer_w_q': peer_w_q[l], 'peer_keys': peer_keys[l],
            'peer_u': peer_u[l], 'peer_v': peer_v[l], 'norm_ple': norm_ple[l], 'w_ple_gate': w_ple_gate[l],
            'w_ple_proj': w_ple_proj[l],
        }
        hp, st_p = hybrid_layer(hp, p_prompt[l],
                                jnp.zeros((bp, H, dh, dh), f32), jnp.zeros((bp, H, dh), f32),
                                jnp.zeros((bp, H), f32), jnp.zeros((bp, G, P), f32),
                                jnp.zeros((bp, G, P), f32), prm)
        hs, st_s = hybrid_layer(hs, p_sample[l], state_mlstm_C[l], state_mlstm_n[l], state_mlstm_m[l],
                                state_ssm_re[l], state_ssm_im[l], prm)
        for j in range(5):
            outs_p[j].append(st_p[j])
            outs_s[j].append(st_s[j])
    y_prompt = rmsnorm(hp, norm_final)
    y_sample = rmsnorm(hs, norm_final)
    new_C_p, new_n_p, new_m_p, new_re_p, new_im_p = (jnp.stack(t) for t in outs_p)
    new_C_s, new_n_s, new_m_s, new_re_s, new_im_s = (jnp.stack(t) for t in outs_s)
    return (y_prompt, y_sample, new_C_p, new_n_p, new_m_p, new_re_p, new_im_p,
            new_C_s, new_n_s, new_m_s, new_re_s, new_im_s)
```

```python
import functools

import jax
import jax.numpy as jnp
from jax import lax
from jax.experimental import pallas as pl
from jax.experimental.pallas import tpu as pltpu

F32 = jnp.float32
BF16 = jnp.bfloat16
I32 = jnp.int32

EPS = 1e-6
D_MODEL = 2048
HEADS = 4
HEAD_DIM = 256
MLSTM_WIDTH = HEADS * HEAD_DIM
SSM_WIDTH = 1024
SSM_GROUPS = 64
SSM_STATE = 64
SSM_GROUP = 16
GROUPS_PER_BLOCK = 16
SSM_BLOCKS = SSM_GROUPS // GROUPS_PER_BLOCK
BLOCK_CH = GROUPS_PER_BLOCK * SSM_GROUP
BLOCK_ST = GROUPS_PER_BLOCK * SSM_STATE
PEER_HEADS = 8
PEER_KEYS = 128
PEER_TOPK = 16
PEER_HALF = 64
PEER_EXPERTS = PEER_KEYS * PEER_KEYS

Z_Q, Z_K, Z_V, Z_O, Z_U, Z_G = 0, 1024, 2048, 3072, 4096, 5120
Z_COLS = Z_G + 128

MLSTM_CHUNK = 256
K_SCALE = HEAD_DIM ** -0.5
NEG_INF = float("-inf")

VMEM_LIMIT = 56 * 1024 * 1024


def _cparams(semantics):
    return pltpu.CompilerParams(dimension_semantics=semantics, vmem_limit_bytes=VMEM_LIMIT)


def _resident(shape, index_map):
    return pl.BlockSpec(shape, index_map, pipeline_mode=pl.Buffered(1))


def _sigmoid(x):
    return 1.0 / (1.0 + jnp.exp(-x))


def _gelu(x):
    return x * (0.5 * (1.0 + jnp.tanh(0.7978845608028654 * (x + 0.044715 * (x * x * x)))))


def _rmsnorm(x, gain):
    return x * lax.rsqrt(jnp.mean(x * x, axis=-1, keepdims=True) + EPS) * gain


def _dot(a, b):
    return jnp.dot(a, b, preferred_element_type=F32)


def _dot_nt(a, b):
    return lax.dot_general(a, b, (((1,), (1,)), ((), ())), preferred_element_type=F32)


def _dot_tn(a, b):
    return lax.dot_general(a, b, (((0,), (0,)), ((), ())), preferred_element_type=F32)


def _in_proj_kernel(x_ref, gain_ref, w_ref, gbias_ref, z_ref):
    a = _rmsnorm(x_ref[...], gain_ref[...]).astype(BF16)
    for c0 in range(0, Z_G, 512):
        z_ref[:, c0:c0 + 512] = _dot(a, w_ref[:, c0:c0 + 512])
    gz = _dot(a, w_ref[:, Z_G:]) + gbias_ref[...]
    lane = lax.broadcasted_iota(I32, gz.shape, 1)
    log_f = jnp.minimum(gz, 0.0) - jnp.log(1.0 + jnp.exp(-jnp.abs(gz)))
    z_ref[:, Z_G:] = jnp.where(lane < HEADS, gz, log_f)


def _in_proj(x_all, gain, w_cat, gbias, tm=256):
    t = x_all.shape[0]
    return pl.pallas_call(
        _in_proj_kernel,
        grid=(t // tm,),
        in_specs=[
            pl.BlockSpec((tm, D_MODEL), lambda i: (i, 0)),
            _resident((1, D_MODEL), lambda i: (0, 0)),
            _resident((D_MODEL, Z_COLS), lambda i: (0, 0)),
            _resident((1, 128), lambda i: (0, 0)),
        ],
        out_specs=pl.BlockSpec((tm, Z_COLS), lambda i: (i, 0)),
        out_shape=jax.ShapeDtypeStruct((t, Z_COLS), F32),
        compiler_params=_cparams(("parallel",)),
        name="in_proj",
    )(x_all, gain, w_cat, gbias)


def _mlstm_chunk(q, k, v, ig_col, lf_col, c_state, n_state, m_state):
    L = q.shape[0]
    row = lax.broadcasted_iota(I32, (L, L), 0)
    col = lax.broadcasted_iota(I32, (L, L), 1)
    diag = row == col
    causal = col <= row
    lf_row = jnp.sum(jnp.where(diag, lf_col, 0.0), axis=0, keepdims=True)
    ig_row = jnp.sum(jnp.where(diag, ig_col, 0.0), axis=0, keepdims=True)
    b_col = jnp.sum(jnp.where(causal, lf_row, 0.0), axis=1, keepdims=True)
    b_row = jnp.sum(jnp.where(row <= col, lf_col, 0.0), axis=0, keepdims=True)
    r_row = ig_row - b_row
    r_col = ig_col - b_col
    run_max = jnp.max(jnp.where(causal, r_row, NEG_INF), axis=1, keepdims=True)
    m_run = jnp.maximum(m_state, run_max)
    w_intra = jnp.exp(jnp.where(causal, r_row - m_run, NEG_INF))
    w_inter = jnp.exp(m_state - m_run)

    qb = q.astype(BF16)
    vb = v.astype(BF16)
    s = _dot_nt(qb, k.astype(BF16)) * w_intra
    num = w_inter * _dot(qb, c_state.astype(BF16)) + _dot(s.astype(BF16), vb)
    qn = jnp.sum(q * n_state, axis=1, keepdims=True)
    den = w_inter * qn + jnp.sum(s, axis=1, keepdims=True)
    hb = num / jnp.maximum(jnp.abs(den), jnp.exp(-(b_col + m_run)))

    m_last = m_run[L - 1:L, :]
    m_new = b_col[L - 1:L, :] + m_last
    kw = k * jnp.exp(r_col - m_last)
    sc = jnp.exp(m_state - m_last)
    c_new = sc * c_state + _dot_tn(kw.astype(BF16), vb)
    n_new = sc * n_state + jnp.sum(kw, axis=0, keepdims=True)
    return hb, c_new, n_new, m_new


def _head_out(hb, gain, o_pre):
    return hb * lax.rsqrt(jnp.mean(hb * hb, axis=-1, keepdims=True) + EPS) * gain * _sigmoid(o_pre)


def _mlstm_prompt_kernel(q_ref, k_ref, v_ref, o_ref, g_ref, gain_ref,
                         hm_ref, c_out_ref, n_out_ref, m_out_ref, c_s, n_s, m_s):
    h = pl.program_id(1)
    c_s[...] = jnp.zeros_like(c_s)
    n_s[...] = jnp.zeros_like(n_s)
    m_s[...] = jnp.zeros_like(m_s)

    def body(ci, carry):
        rows = pl.ds(pl.multiple_of(ci * MLSTM_CHUNK, MLSTM_CHUNK), MLSTM_CHUNK)
        gates = g_ref[rows, :]
        lane = lax.broadcasted_iota(I32, gates.shape, 1)
        ig_col = jnp.sum(jnp.where(lane == h, gates, 0.0), axis=1, keepdims=True)
        lf_col = jnp.sum(jnp.where(lane == h + HEADS, gates, 0.0), axis=1, keepdims=True)
        hb, c_new, n_new, m_new = _mlstm_chunk(
            q_ref[rows, :], k_ref[rows, :] * K_SCALE, v_ref[rows, :],
            ig_col, lf_col, c_s[...], n_s[...], m_s[:, 0:1])
        c_s[...] = c_new
        n_s[...] = n_new
        m_s[...] = jnp.broadcast_to(m_new, m_s.shape)
        hm_ref[rows, :] = _head_out(hb, gain_ref[...], o_ref[rows, :])
        return carry

    lax.fori_loop(0, q_ref.shape[0] // MLSTM_CHUNK, body, 0)
    c_out_ref[...] = c_s[...]
    n_out_ref[...] = n_s[...]
    m_out_ref[...] = m_s[...]


def _mlstm_prompt(z, gain, n_seq, seq_len):
    hpb = MLSTM_WIDTH // HEAD_DIM

    def sec(off):
        return pl.BlockSpec((seq_len, HEAD_DIM), lambda b, h, o=off // HEAD_DIM: (b, o + h))

    nbh = n_seq * HEADS
    return pl.pallas_call(
        _mlstm_prompt_kernel,
        grid=(n_seq, HEADS),
        in_specs=[
            sec(Z_Q), sec(Z_K), sec(Z_V), sec(Z_O),
            pl.BlockSpec((seq_len, 128), lambda b, h: (b, Z_G // 128)),
            pl.BlockSpec((1, HEAD_DIM), lambda b, h: (0, h)),
        ],
        out_specs=[
            pl.BlockSpec((seq_len, HEAD_DIM), lambda b, h: (b, h)),
            pl.BlockSpec((None, HEAD_DIM, HEAD_DIM), lambda b, h: (b * hpb + h, 0, 0)),
            pl.BlockSpec((None, 1, HEAD_DIM), lambda b, h: (b * hpb + h, 0, 0)),
            pl.BlockSpec((None, 1, 128), lambda b, h: (b * hpb + h, 0, 0)),
        ],
        out_shape=[
            jax.ShapeDtypeStruct((n_seq * seq_len, MLSTM_WIDTH), F32),
            jax.ShapeDtypeStruct((nbh, HEAD_DIM, HEAD_DIM), F32),
            jax.ShapeDtypeStruct((nbh, 1, HEAD_DIM), F32),
            jax.ShapeDtypeStruct((nbh, 1, 128), F32),
        ],
        scratch_shapes=[
            pltpu.VMEM((HEAD_DIM, HEAD_DIM), F32),
            pltpu.VMEM((1, HEAD_DIM), F32),
            pltpu.VMEM((1, 128), F32),
        ],
        compiler_params=_cparams(("parallel", "parallel")),
        name="mlstm_prompt",
    )(z, z, z, z, z, gain)


SAMPLE_SEQ = 4
SAMPLE_ROWS = 16


def _mlstm_sample_kernel(z_ref, g_ref, gain_ref, c_in_ref, n_in_ref, m_in_ref,
                         hm_ref, c_out_ref, n_out_ref, m_out_ref):
    for grp in range(SAMPLE_ROWS // 8):
        r0 = grp * 8
        gates = g_ref[r0:r0 + 8, :]
        row = lax.broadcasted_iota(I32, (8, 1), 0)
        for half in range(2):
            lo = half * SAMPLE_SEQ
            live = jnp.logical_and(row >= lo, row < lo + SAMPLE_SEQ)
            for h in range(HEADS):
                j = (grp * 2 + half) * HEADS + h
                cs = slice(h * HEAD_DIM, (h + 1) * HEAD_DIM)
                q = z_ref[r0:r0 + 8, Z_Q + h * HEAD_DIM:Z_Q + (h + 1) * HEAD_DIM]
                k = z_ref[r0:r0 + 8, Z_K + h * HEAD_DIM:Z_K + (h + 1) * HEAD_DIM]
                v = z_ref[r0:r0 + 8, Z_V + h * HEAD_DIM:Z_V + (h + 1) * HEAD_DIM]
                o = z_ref[r0:r0 + 8, Z_O + h * HEAD_DIM:Z_O + (h + 1) * HEAD_DIM]
                k = jnp.where(live, k * K_SCALE, 0.0)
                v = jnp.where(live, v, 0.0)
                ig_col = jnp.where(live, gates[:, h:h + 1], NEG_INF)
                lf_col = jnp.where(live, gates[:, HEADS + h:HEADS + h + 1], 0.0)
                hb, c_new, n_new, m_new = _mlstm_chunk(
                    q, k, v, ig_col, lf_col, c_in_ref[j], n_in_ref[j], m_in_ref[j][:, 0:1])
                out = _head_out(hb, gain_ref[:, cs], o)
                hm_ref[r0 + lo:r0 + lo + SAMPLE_SEQ, cs] = out[lo:lo + SAMPLE_SEQ, :]
                c_out_ref[j] = c_new
                n_out_ref[j] = n_new
                m_out_ref[j] = jnp.broadcast_to(m_new, (1, 128))


def _mlstm_sample(z, gain, c0, n0, m0, row0, n_rows):
    nbh = c0.shape[0]
    per = SAMPLE_ROWS // SAMPLE_SEQ * HEADS
    rb0 = row0 // SAMPLE_ROWS
    return pl.pallas_call(
        _mlstm_sample_kernel,
        grid=(n_rows // SAMPLE_ROWS,),
        in_specs=[
            pl.BlockSpec((SAMPLE_ROWS, Z_U), lambda i: (rb0 + i, 0)),
            pl.BlockSpec((SAMPLE_ROWS, 128), lambda i: (rb0 + i, Z_G // 128)),
            pl.BlockSpec((1, MLSTM_WIDTH), lambda i: (0, 0)),
            pl.BlockSpec((per, HEAD_DIM, HEAD_DIM), lambda i: (i, 0, 0)),
            pl.BlockSpec((per, 1, HEAD_DIM), lambda i: (i, 0, 0)),
            pl.BlockSpec((per, 1, 128), lambda i: (i, 0, 0)),
        ],
        out_specs=[
            pl.BlockSpec((SAMPLE_ROWS, MLSTM_WIDTH), lambda i: (i, 0)),
            pl.BlockSpec((per, HEAD_DIM, HEAD_DIM), lambda i: (i, 0, 0)),
            pl.BlockSpec((per, 1, HEAD_DIM), lambda i: (i, 0, 0)),
            pl.BlockSpec((per, 1, 128), lambda i: (i, 0, 0)),
        ],
        out_shape=[
            jax.ShapeDtypeStruct((n_rows, MLSTM_WIDTH), F32),
            jax.ShapeDtypeStruct((nbh, HEAD_DIM, HEAD_DIM), F32),
            jax.ShapeDtypeStruct((nbh, 1, HEAD_DIM), F32),
            jax.ShapeDtypeStruct((nbh, 1, 128), F32),
        ],
        compiler_params=_cparams(("parallel",)),
        name="mlstm_sample",
    )(z, z, gain, c0, n0, m0)


def _s5_discretise(a_re, a_im, log_dt):
    dt = jnp.exp(log_dt)
    mag = jnp.exp(dt * a_re)
    ab_re = mag * jnp.cos(dt * a_im)
    ab_im = mag * jnp.sin(dt * a_im)
    den = a_re * a_re + a_im * a_im
    xr = ab_re - 1.0
    f_re = (xr * a_re + ab_im * a_im) / den
    f_im = (ab_im * a_re - xr * a_im) / den
    return ab_re, ab_im, f_re, f_im


def _s5_param_kernel(are_e, aim_e, ldt_e, b_re, b_im, are_r, aim_r, ldt_r,
                     bb_re_o, bb_im_o, ab_re_o, ab_im_o):
    _, _, f_re, f_im = _s5_discretise(are_e[...], aim_e[...], ldt_e[...])
    bb_re_o[...] = f_re * b_re[...] - f_im * b_im[...]
    bb_im_o[...] = f_re * b_im[...] + f_im * b_re[...]
    ab_re, ab_im, _, _ = _s5_discretise(are_r[...], aim_r[...], ldt_r[...])
    ab_re_o[...] = ab_re
    ab_im_o[...] = ab_im


def _s5_params(a_re, a_im, log_dt, b_re, b_im):
    g, p = a_re.shape
    c = b_re.shape[-1]
    rep = lambda t: jnp.repeat(t, c, axis=-1)
    ldt_e = jnp.broadcast_to(log_dt[:, None], (g, p * c))
    ldt_r = jnp.broadcast_to(log_dt[:, None], (g, p)).reshape(1, g * p)
    flat = jax.ShapeDtypeStruct((g, p * c), F32)
    rowv = jax.ShapeDtypeStruct((1, g * p), F32)
    return pl.pallas_call(
        _s5_param_kernel,
        out_shape=[flat, flat, rowv, rowv],
        name="s5_params",
    )(rep(a_re), rep(a_im), ldt_e, b_re.reshape(g, p * c), b_im.reshape(g, p * c),
      a_re.reshape(1, g * p), a_im.reshape(1, g * p), ldt_r)


def _blockdiag_in(bb_re, bb_im):
    bb = jnp.stack([bb_re, bb_im]).reshape(2, SSM_BLOCKS, GROUPS_PER_BLOCK, SSM_STATE, SSM_GROUP)
    t = jnp.transpose(bb, (1, 2, 4, 0, 3))
    eye = jnp.eye(GROUPS_PER_BLOCK, dtype=bool)
    w = jnp.where(eye[None, :, None, None, :, None], t[:, :, :, :, None, :], 0.0)
    return w.reshape(SSM_BLOCKS, BLOCK_CH, 2 * BLOCK_ST).astype(BF16)


def _blockdiag_out(cm):
    t = jnp.transpose(cm.reshape(SSM_BLOCKS, GROUPS_PER_BLOCK, SSM_GROUP, SSM_STATE), (0, 1, 3, 2))
    eye = jnp.eye(GROUPS_PER_BLOCK, dtype=bool)
    w = jnp.where(eye[None, :, None, :, None], t[:, :, :, None, :], 0.0)
    return w.reshape(SSM_BLOCKS, BLOCK_ST, BLOCK_CH).astype(BF16)


S5_TILE = 256
S5_SEQS = 4


def _s5_prompt_kernel(u0, u1, u2, u3, wb_ref, wcre_ref, wcim_ref, abre_ref, abim_ref, d_ref,
                      y_ref, sre_ref, sim_ref, u_tm, bu, y_tm, st):
    i = pl.program_id(1)
    half = BLOCK_CH // 2

    @pl.when(i == 0)
    def _():
        st[...] = jnp.zeros_like(st)

    for b, u in enumerate((u0, u1, u2, u3)):
        u_tm[0, pl.ds(b, S5_TILE, stride=S5_SEQS), :] = u[:, :half]
        u_tm[1, pl.ds(b, S5_TILE, stride=S5_SEQS), :] = u[:, half:]
    u_all = jnp.concatenate([u_tm[0], u_tm[1]], axis=1)
    bu[...] = _dot(u_all.astype(BF16), wb_ref[...])
    a_re = abre_ref[...]
    a_im = abim_ref[...]

    first = lax.broadcasted_iota(I32, (2 * S5_SEQS, BLOCK_ST), 0) < S5_SEQS

    def body(t2, carry):
        s_re, s_im = carry
        rows = pl.ds(pl.multiple_of(t2 * 2 * S5_SEQS, 2 * S5_SEQS), 2 * S5_SEQS)
        x_re = bu[rows, :BLOCK_ST]
        x_im = bu[rows, BLOCK_ST:]
        p_re = a_re * s_re - a_im * s_im + x_re
        p_im = a_re * s_im + a_im * s_re + x_im
        r_re = pltpu.roll(p_re, S5_SEQS, 0)
        r_im = pltpu.roll(p_im, S5_SEQS, 0)
        q_re = a_re * r_re - a_im * r_im + x_re
        q_im = a_re * r_im + a_im * r_re + x_im
        bu[rows, :BLOCK_ST] = jnp.where(first, p_re, q_re)
        bu[rows, BLOCK_ST:] = jnp.where(first, p_im, q_im)
        return pltpu.roll(q_re, S5_SEQS, 0), pltpu.roll(q_im, S5_SEQS, 0)

    s_re, s_im = lax.fori_loop(0, S5_TILE // 2, body, (st[:, :BLOCK_ST], st[:, BLOCK_ST:]), unroll=2)
    st[:, :BLOCK_ST] = s_re
    st[:, BLOCK_ST:] = s_im
    y = (_dot(bu[:, :BLOCK_ST].astype(BF16), wcre_ref[...])
         - _dot(bu[:, BLOCK_ST:].astype(BF16), wcim_ref[...]) + d_ref[...] * u_all)
    y = _gelu(y)
    y_tm[0] = y[:, :half]
    y_tm[1] = y[:, half:]
    for b in range(S5_SEQS):
        y_ref[b, :, :half] = y_tm[0, pl.ds(b, S5_TILE, stride=S5_SEQS), :]
        y_ref[b, :, half:] = y_tm[1, pl.ds(b, S5_TILE, stride=S5_SEQS), :]

    @pl.when(i == pl.num_programs(1) - 1)
    def _():
        sre_ref[...] = s_re[:S5_SEQS, :]
        sim_ref[...] = s_im[:S5_SEQS, :]


def _s5_prompt(z, wb, wcre, wcim, ab_re, ab_im, d_row, seq_len):
    nt = seq_len // S5_TILE
    ucol = Z_U // BLOCK_CH

    def u_spec(b):
        return pl.BlockSpec((S5_TILE, BLOCK_CH), lambda j, i, b=b: (b * nt + i, ucol + j))

    rows = S5_TILE * S5_SEQS
    return pl.pallas_call(
        _s5_prompt_kernel,
        grid=(SSM_BLOCKS, nt),
        in_specs=[u_spec(b) for b in range(S5_SEQS)] + [
            pl.BlockSpec((None, BLOCK_CH, 2 * BLOCK_ST), lambda j, i: (j, 0, 0)),
            pl.BlockSpec((None, BLOCK_ST, BLOCK_CH), lambda j, i: (j, 0, 0)),
            pl.BlockSpec((None, BLOCK_ST, BLOCK_CH), lambda j, i: (j, 0, 0)),
            pl.BlockSpec((1, BLOCK_ST), lambda j, i: (0, j)),
            pl.BlockSpec((1, BLOCK_ST), lambda j, i: (0, j)),
            pl.BlockSpec((1, BLOCK_CH), lambda j, i: (0, j)),
        ],
        out_specs=[
            pl.BlockSpec((S5_SEQS, S5_TILE, BLOCK_CH), lambda j, i: (0, i, j)),
            pl.BlockSpec((S5_SEQS, BLOCK_ST), lambda j, i: (0, j)),
            pl.BlockSpec((S5_SEQS, BLOCK_ST), lambda j, i: (0, j)),
        ],
        out_shape=[
            jax.ShapeDtypeStruct((S5_SEQS, seq_len, SSM_WIDTH), F32),
            jax.ShapeDtypeStruct((S5_SEQS, SSM_GROUPS * SSM_STATE), F32),
            jax.ShapeDtypeStruct((S5_SEQS, SSM_GROUPS * SSM_STATE), F32),
        ],
        scratch_shapes=[
            pltpu.VMEM((2, rows, 128), F32),
            pltpu.VMEM((rows, 2 * BLOCK_ST), F32),
            pltpu.VMEM((2, rows, 128), F32),
            pltpu.VMEM((2 * S5_SEQS, 2 * BLOCK_ST), F32),
        ],
        compiler_params=_cparams(("parallel", "arbitrary")),
        name="s5_prompt",
    )(z, z, z, z, wb, wcre, wcim, ab_re, ab_im, d_row)


def _s5_sample_kernel(u_ref, s0re_ref, s0im_ref, wb_ref, wcre_ref, wcim_ref, abre_ref, abim_ref,
                      d_ref, y_ref, sre_ref, sim_ref, u_sl, y_sl):
    half = BLOCK_CH // 2
    n_seq = s0re_ref.shape[0]
    u_sl[0] = u_ref[:, :half]
    u_sl[1] = u_ref[:, half:]
    a_re = abre_ref[...]
    a_im = abim_ref[...]
    s_re = s0re_ref[...]
    s_im = s0im_ref[...]
    for t in range(SAMPLE_SEQ):
        rows = pl.ds(t, n_seq, stride=SAMPLE_SEQ)
        u_t = jnp.concatenate([u_sl[0, rows, :], u_sl[1, rows, :]], axis=1)
        bu = _dot(u_t.astype(BF16), wb_ref[...])
        n_re = a_re * s_re - a_im * s_im + bu[:, :BLOCK_ST]
        n_im = a_re * s_im + a_im * s_re + bu[:, BLOCK_ST:]
        s_re, s_im = n_re, n_im
        y = (_dot(s_re.astype(BF16), wcre_ref[...]) - _dot(s_im.astype(BF16), wcim_ref[...])
             + d_ref[...] * u_t)
        y = _gelu(y)
        y_sl[0, rows, :] = y[:, :half]
        y_sl[1, rows, :] = y[:, half:]
    y_ref[:, :half] = y_sl[0]
    y_ref[:, half:] = y_sl[1]
    sre_ref[...] = s_re
    sim_ref[...] = s_im


def _s5_sample(z, s0_re, s0_im, wb, wcre, wcim, ab_re, ab_im, d_row, row0, n_rows):
    n_seq = s0_re.shape[0]
    ucol = Z_U // BLOCK_CH
    st_spec = pl.BlockSpec((n_seq, BLOCK_ST), lambda j: (0, j))
    return pl.pallas_call(
        _s5_sample_kernel,
        grid=(SSM_BLOCKS,),
        in_specs=[
            pl.BlockSpec((n_rows, BLOCK_CH), lambda j: (row0 // n_rows, ucol + j)),
            st_spec, st_spec,
            pl.BlockSpec((None, BLOCK_CH, 2 * BLOCK_ST), lambda j: (j, 0, 0)),
            pl.BlockSpec((None, BLOCK_ST, BLOCK_CH), lambda j: (j, 0, 0)),
            pl.BlockSpec((None, BLOCK_ST, BLOCK_CH), lambda j: (j, 0, 0)),
            pl.BlockSpec((1, BLOCK_ST), lambda j: (0, j)),
            pl.BlockSpec((1, BLOCK_ST), lambda j: (0, j)),
            pl.BlockSpec((1, BLOCK_CH), lambda j: (0, j)),
        ],
        out_specs=[pl.BlockSpec((n_rows, BLOCK_CH), lambda j: (0, j)), st_spec, st_spec],
        out_shape=[
            jax.ShapeDtypeStruct((n_rows, SSM_WIDTH), F32),
            jax.ShapeDtypeStruct(s0_re.shape, F32),
            jax.ShapeDtypeStruct(s0_im.shape, F32),
        ],
        scratch_shapes=[pltpu.VMEM((2, n_rows, 128), F32), pltpu.VMEM((2, n_rows, 128), F32)],
        compiler_params=_cparams(("parallel",)),
        name="s5_sample",
    )(z, s0_re, s0_im, wb, wcre, wcim, ab_re, ab_im, d_row)


def _postmix_kernel(hm_ref, ys_ref, x_ref, wglu_ref, bglu_ref, wout_ref, gain_ref, wq_ref, keys_ref,
                    h1_ref, c_ref, s_ref):
    ys = ys_ref[...]
    glu = ys * _sigmoid(_dot(ys.astype(BF16), wglu_ref[...]) + bglu_ref[...])
    mix = (_dot(hm_ref[...].astype(BF16), wout_ref[:MLSTM_WIDTH, :])
           + _dot(glu.astype(BF16), wout_ref[MLSTM_WIDTH:, :]))
    h1 = x_ref[...] + mix
    h1_ref[...] = h1
    c = _rmsnorm(h1, gain_ref[...]).astype(BF16)
    c_ref[...] = c
    qp = _dot(c, wq_ref[...])
    for j in range(2 * PEER_HEADS):
        qh = qp[:, j * PEER_HALF:(j + 1) * PEER_HALF].astype(BF16)
        s_ref[j] = _dot_nt(keys_ref[j], qh)


def _postmix(hm, ys, x_all, wglu, bglu, wout, gain, wq, keys, tm=256):
    t = x_all.shape[0]
    nk = 2 * PEER_HEADS
    return pl.pallas_call(
        _postmix_kernel,
        grid=(t // tm,),
        in_specs=[
            pl.BlockSpec((tm, MLSTM_WIDTH), lambda i: (i, 0)),
            pl.BlockSpec((tm, SSM_WIDTH), lambda i: (i, 0)),
            pl.BlockSpec((tm, D_MODEL), lambda i: (i, 0)),
            _resident((SSM_WIDTH, SSM_WIDTH), lambda i: (0, 0)),
            _resident((1, SSM_WIDTH), lambda i: (0, 0)),
            _resident((D_MODEL, D_MODEL), lambda i: (0, 0)),
            _resident((1, D_MODEL), lambda i: (0, 0)),
            _resident((D_MODEL, PEER_HEADS * 2 * PEER_HALF), lambda i: (0, 0)),
            _resident((nk, PEER_KEYS, PEER_HALF), lambda i: (0, 0, 0)),
        ],
        out_specs=[
            pl.BlockSpec((tm, D_MODEL), lambda i: (i, 0)),
            pl.BlockSpec((tm, D_MODEL), lambda i: (i, 0)),
            pl.BlockSpec((nk, PEER_KEYS, tm), lambda i: (0, 0, i)),
        ],
        out_shape=[
            jax.ShapeDtypeStruct((t, D_MODEL), F32),
            jax.ShapeDtypeStruct((t, D_MODEL), BF16),
            jax.ShapeDtypeStruct((nk, PEER_KEYS, t), F32),
        ],
        compiler_params=_cparams(("parallel",)),
        name="postmix",
    )(hm, ys, x_all, wglu, bglu, wout, gain, wq, keys)


ID_NONE = 1 << 20


def _extract_topk(s, ids, k):
    vals, idxs = [], []
    for _ in range(k):
        m = jnp.max(s, axis=0, keepdims=True)
        i = jnp.min(jnp.where(s == m, ids, ID_NONE), axis=0, keepdims=True)
        vals.append(m)
        idxs.append(i)
        s = jnp.where(ids == i, NEG_INF, s)
    return vals, idxs


def _pair_limit(i):
    return PEER_TOPK // (i + 1)


def _topk_kernel(s_ref, e1_ref, e2_ref, g_ref, v_s, i_s):
    tb = s_ref.shape[-1]
    ids = lax.broadcasted_iota(I32, (PEER_KEYS, tb), 0)
    for half in range(2):
        vals, idxs = _extract_topk(s_ref[half], ids, PEER_TOPK)
        for k in range(PEER_TOPK):
            v_s[half, k:k + 1, :] = vals[k]
            i_s[half, k:k + 1, :] = idxs[k]
    v1, v2 = v_s[0], v_s[1]
    i1, i2 = i_s[0], i_s[1]

    r16 = lax.broadcasted_iota(I32, (PEER_TOPK, tb), 0)
    r8 = lax.broadcasted_iota(I32, (8, tb), 0)
    blocks = [v1[0:1, :] + v2]
    bids = [r16]
    for i in range(1, 8):
        keep = r8 < _pair_limit(i)
        blocks.append(jnp.where(keep, v1[i:i + 1, :] + v2[0:8, :], NEG_INF))
        bids.append(jnp.where(keep, r8 + i * PEER_TOPK, ID_NONE))
    blocks.append(v1[8:16, :] + v2[0:1, :])
    bids.append((r8 + 8) * PEER_TOPK)
    cand = jnp.concatenate(blocks, axis=0)
    cid = jnp.concatenate(bids, axis=0)
    vs, ic = _extract_topk(cand, cid, PEER_TOPK)

    for k in range(PEER_TOPK):
        a = lax.shift_right_logical(ic[k], 4)
        b = jnp.bitwise_and(ic[k], PEER_TOPK - 1)
        e1_ref[k:k + 1, :] = jnp.sum(jnp.where(r16 == a, i1, 0), axis=0, keepdims=True)
        e2_ref[k:k + 1, :] = jnp.sum(jnp.where(r16 == b, i2, 0), axis=0, keepdims=True)
        v_s[0, k:k + 1, :] = vs[k]
    ex = jnp.exp(v_s[0] - vs[0])
    g_ref[...] = ex / jnp.sum(ex, axis=0, keepdims=True)


def _topk(scores, tb=256):
    t = scores.shape[-1]
    r = PEER_HEADS * PEER_TOPK
    o_spec = pl.BlockSpec((PEER_TOPK, tb), lambda i, h: (h, i))
    return pl.pallas_call(
        _topk_kernel,
        grid=(t // tb, PEER_HEADS),
        in_specs=[pl.BlockSpec((2, PEER_KEYS, tb), lambda i, h: (h, 0, i))],
        out_specs=[o_spec, o_spec, o_spec],
        out_shape=[
            jax.ShapeDtypeStruct((r, t), I32),
            jax.ShapeDtypeStruct((r, t), I32),
            jax.ShapeDtypeStruct((r, t), F32),
        ],
        scratch_shapes=[pltpu.VMEM((2, PEER_TOPK, tb), F32), pltpu.VMEM((2, PEER_TOPK, tb), I32)],
        compiler_params=_cparams(("parallel", "parallel")),
        name="topk",
    )(scores)


WB_GROUP = 16
WB_PITCH = 132


def _wbuild_kernel(e1_ref, e2_ref, g_ref, w_ref, stage):
    sub = lax.broadcasted_iota(I32, (PEER_KEYS, PEER_KEYS), 0)

    def group(gi, carry):
        t0 = pl.multiple_of(gi * WB_GROUP, WB_GROUP)
        for tt in range(WB_GROUP):
            row = pl.ds(t0 + tt, 1)
            onehot1 = jnp.where(sub == e1_ref[row, :], 1.0, 0.0).astype(BF16)
            gated2 = jnp.where(sub == e2_ref[row, :], g_ref[row, :], 0.0).astype(BF16)
            stage[tt * WB_PITCH:tt * WB_PITCH + PEER_KEYS, :] = _dot_nt(onehot1, gated2)
        for e in range(PEER_KEYS):
            blk = stage[pl.ds(e, WB_GROUP, stride=WB_PITCH), :]
            w_ref[e, pl.ds(t0, WB_GROUP), :] = blk.astype(BF16)
        return carry

    lax.fori_loop(0, e1_ref.shape[0] // WB_GROUP, group, 0)


def _wbuild(e1, e2, g, tw=128):
    t = e1.shape[0]
    i_spec = pl.BlockSpec((tw, PEER_KEYS), lambda i: (i, 0))
    return pl.pallas_call(
        _wbuild_kernel,
        grid=(t // tw,),
        in_specs=[i_spec, i_spec, i_spec],
        out_specs=pl.BlockSpec((PEER_KEYS, tw, PEER_KEYS), lambda i: (0, i, 0)),
        out_shape=jax.ShapeDtypeStruct((PEER_KEYS, t, PEER_KEYS), BF16),
        scratch_shapes=[pltpu.VMEM((WB_GROUP * WB_PITCH, PEER_KEYS), F32)],
        compiler_params=_cparams(("parallel",)),
        name="wbuild",
    )(e1, e2, g)


PEER_EB = 512


def _peer_kernel(c_ref, u_ref, v_ref, w_ref, o_ref):
    @pl.when(pl.program_id(1) == 0)
    def _():
        o_ref[...] = jnp.zeros_like(o_ref)

    act = _gelu(_dot_nt(c_ref[...], u_ref[...]))
    parts = [act[:, k * PEER_KEYS:(k + 1) * PEER_KEYS] * w_ref[k].astype(F32)
             for k in range(PEER_EB // PEER_KEYS)]
    wact = jnp.concatenate(parts, axis=1).astype(BF16)
    o_ref[...] += _dot(wact, v_ref[...])


def _peer(c, u, v, w3, n_tiles=8):
    t = c.shape[0]
    tm = t // n_tiles
    return pl.pallas_call(
        _peer_kernel,
        grid=(n_tiles, PEER_EXPERTS // PEER_EB),
        in_specs=[
            pl.BlockSpec((tm, D_MODEL), lambda i, j: (i, 0)),
            pl.BlockSpec((PEER_EB, D_MODEL), lambda i, j: (j, 0)),
            pl.BlockSpec((PEER_EB, D_MODEL), lambda i, j: (j, 0)),
            pl.BlockSpec((PEER_EB // PEER_KEYS, tm, PEER_KEYS), lambda i, j: (j, i, 0)),
        ],
        out_specs=pl.BlockSpec((tm, D_MODEL), lambda i, j: (i, 0)),
        out_shape=jax.ShapeDtypeStruct((t, D_MODEL), F32),
        compiler_params=_cparams(("parallel", "arbitrary")),
        name="peer",
    )(c, u, v, w3)


def _tail_kernel(h1_ref, peer_ref, p_ref, gple_ref, wgate_ref, wproj_ref, gfin_ref,
                 yp_ref, ys_ref, *, prompt_tiles):
    i = pl.program_id(0)
    h2 = h1_ref[...] + peer_ref[...]
    gate = _sigmoid(_dot(_rmsnorm(h2, gple_ref[...]).astype(BF16), wgate_ref[...]))
    e = _dot(p_ref[...].astype(BF16), wproj_ref[...])
    y = _rmsnorm(h2 + e * gate, gfin_ref[...])

    @pl.when(i < prompt_tiles)
    def _():
        yp_ref[...] = y

    @pl.when(i >= prompt_tiles)
    def _():
        ys_ref[...] = y


def _tail(h1, peer, p_all, gple, wgate, wproj, gfin, n_prompt, tm=256):
    t = h1.shape[0]
    pt = n_prompt // tm
    ple = p_all.shape[1]
    return pl.pallas_call(
        functools.partial(_tail_kernel, prompt_tiles=pt),
        grid=(t // tm,),
        in_specs=[
            pl.BlockSpec((tm, D_MODEL), lambda i: (i, 0)),
            pl.BlockSpec((tm, D_MODEL), lambda i: (i, 0)),
            pl.BlockSpec((tm, ple), lambda i: (i, 0)),
            _resident((1, D_MODEL), lambda i: (0, 0)),
            _resident((D_MODEL, D_MODEL), lambda i: (0, 0)),
            _resident((ple, D_MODEL), lambda i: (0, 0)),
            _resident((1, D_MODEL), lambda i: (0, 0)),
        ],
        out_specs=[
            pl.BlockSpec((tm, D_MODEL), lambda i: (jnp.minimum(i, pt - 1), 0)),
            pl.BlockSpec((tm, D_MODEL), lambda i: (jnp.maximum(i - pt, 0), 0)),
        ],
        out_shape=[
            jax.ShapeDtypeStruct((n_prompt, D_MODEL), F32),
            jax.ShapeDtypeStruct((t - n_prompt, D_MODEL), F32),
        ],
        compiler_params=_cparams(("arbitrary",)),
        name="tail",
    )(h1, peer, p_all, gple, wgate, wproj, gfin)


def kernel(x_prompt, x_sample, state_mlstm_C, state_mlstm_n, state_mlstm_m, state_ssm_re, state_ssm_im, p_prompt, p_sample, norm_mix, w_in, b_igate, b_fgate, mlstm_norm, ssm_A_re, ssm_A_im, ssm_B_re, ssm_B_im, ssm_C_re, ssm_C_im, ssm_D, ssm_log_dt, w_glu, b_glu, w_out, norm_ffn, peer_w_q, peer_keys, peer_u, peer_v, norm_ple, w_ple_gate, w_ple_proj, norm_final):
    n_pseq, p_len, _ = x_prompt.shape
    n_sseq, s_len, _ = x_sample.shape
    assert s_len == SAMPLE_SEQ and n_pseq == S5_SEQS and w_in.shape[0] == 1
    n_prompt = n_pseq * p_len
    n_sample = n_sseq * s_len
    row = lambda t: t.reshape(1, -1)

    x_all = jnp.concatenate([x_prompt.reshape(n_prompt, D_MODEL), x_sample.reshape(n_sample, D_MODEL)])
    p_all = jnp.concatenate([p_prompt[0].reshape(n_prompt, -1), p_sample[0].reshape(n_sample, -1)])

    w = w_in[0]
    n_gate = 2 * HEADS
    w_cat = jnp.concatenate(
        [w[:, :4 * MLSTM_WIDTH], w[:, 4 * MLSTM_WIDTH + n_gate:],
         w[:, 4 * MLSTM_WIDTH:4 * MLSTM_WIDTH + n_gate],
         jnp.zeros((D_MODEL, 128 - n_gate), F32)], axis=1).astype(BF16)
    gbias = jnp.concatenate([b_igate[0], b_fgate[0], jnp.zeros((128 - n_gate,), F32)]).reshape(1, 128)
    z = _in_proj(x_all, row(norm_mix[0]), w_cat, gbias)

    gain_m = row(mlstm_norm[0])
    hm_p, c_p, n_p, m_p = _mlstm_prompt(z, gain_m, n_pseq, p_len)
    nbh = n_sseq * HEADS
    hm_s, c_s, n_s, m_s = _mlstm_sample(
        z, gain_m,
        state_mlstm_C[0].reshape(nbh, HEAD_DIM, HEAD_DIM),
        state_mlstm_n[0].reshape(nbh, 1, HEAD_DIM),
        jnp.broadcast_to(state_mlstm_m[0].reshape(nbh, 1, 1), (nbh, 1, 128)),
        n_prompt, n_sample)

    bb_re, bb_im, ab_re, ab_im = _s5_params(ssm_A_re[0], ssm_A_im[0], ssm_log_dt[0], ssm_B_re[0], ssm_B_im[0])
    wb = _blockdiag_in(bb_re, bb_im)
    wcre = _blockdiag_out(ssm_C_re[0])
    wcim = _blockdiag_out(ssm_C_im[0])
    d_row = row(ssm_D[0])
    ys_p, sre_p, sim_p = _s5_prompt(z, wb, wcre, wcim, ab_re, ab_im, d_row, p_len)
    n_st = SSM_GROUPS * SSM_STATE
    ys_s, sre_s, sim_s = _s5_sample(
        z, state_ssm_re[0].reshape(n_sseq, n_st), state_ssm_im[0].reshape(n_sseq, n_st),
        wb, wcre, wcim, ab_re, ab_im, d_row, n_prompt, n_sample)

    hm = jnp.concatenate([hm_p, hm_s])
    ys = jnp.concatenate([ys_p.reshape(n_prompt, SSM_WIDTH), ys_s])

    keys = peer_keys[0].reshape(2 * PEER_HEADS, PEER_KEYS, PEER_HALF).astype(BF16)
    h1, c, scores = _postmix(hm, ys, x_all, w_glu[0].astype(BF16), row(b_glu[0]), w_out[0].astype(BF16),
                             row(norm_ffn[0]), peer_w_q[0].astype(BF16), keys)
    e1t, e2t, gt = _topk(scores)
    w3 = _wbuild(e1t.T, e2t.T, gt.T)
    peer = _peer(c, peer_u[0].astype(BF16), peer_v[0].astype(BF16), w3)

    y_p, y_s = _tail(h1, peer, p_all, row(norm_ple[0]), w_ple_gate[0].astype(BF16),
                     w_ple_proj[0].astype(BF16), row(norm_final), n_prompt)

    st_shape = (1, -1, SSM_GROUPS, SSM_STATE)
    return (
        y_p.reshape(x_prompt.shape), y_s.reshape(x_sample.shape),
        c_p.reshape(1, n_pseq, HEADS, HEAD_DIM, HEAD_DIM), n_p.reshape(1, n_pseq, HEADS, HEAD_DIM),
        m_p[:, 0, 0].reshape(1, n_pseq, HEADS),
        sre_p.reshape(st_shape), sim_p.reshape(st_shape),
        c_s.reshape(1, n_sseq, HEADS, HEAD_DIM, HEAD_DIM), n_s.reshape(1, n_sseq, HEADS, HEAD_DIM),
        m_s[:, 0, 0].reshape(1, n_sseq, HEADS),
        sre_s.reshape(st_shape), sim_s.reshape(st_shape),
    )
```

```python
import functools

import jax
import jax.numpy as jnp
from jax import lax
from jax.experimental import pallas as pl
from jax.experimental.pallas import tpu as pltpu

F32 = jnp.float32
BF16 = jnp.bfloat16
I32 = jnp.int32

EPS = 1e-6
D_MODEL = 2048
HEADS = 4
HEAD_DIM = 256
MLSTM_WIDTH = HEADS * HEAD_DIM
SSM_WIDTH = 1024
SSM_GROUPS = 64
SSM_STATE = 64
SSM_GROUP = 16
GROUPS_PER_BLOCK = 16
SSM_BLOCKS = SSM_GROUPS // GROUPS_PER_BLOCK
BLOCK_CH = GROUPS_PER_BLOCK * SSM_GROUP
BLOCK_ST = GROUPS_PER_BLOCK * SSM_STATE
PEER_HEADS = 8
PEER_KEYS = 128
PEER_TOPK = 16
PEER_HALF = 64
PEER_EXPERTS = PEER_KEYS * PEER_KEYS

Z_Q, Z_K, Z_V, Z_O, Z_U, Z_G = 0, 1024, 2048, 3072, 4096, 5120
Z_COLS = Z_G + 128

MLSTM_CHUNK = 256
K_SCALE = HEAD_DIM ** -0.5
NEG_INF = float("-inf")

VMEM_LIMIT = 56 * 1024 * 1024


def _cparams(semantics):
    return pltpu.CompilerParams(dimension_semantics=semantics, vmem_limit_bytes=VMEM_LIMIT)


def _resident(shape, index_map):
    return pl.BlockSpec(shape, index_map, pipeline_mode=pl.Buffered(1))


def _sigmoid(x):
    return 1.0 / (1.0 + jnp.exp(-x))


def _gelu(x):
    return x * (0.5 * (1.0 + jnp.tanh(0.7978845608028654 * (x + 0.044715 * (x * x * x)))))


def _rmsnorm(x, gain):
    return x * lax.rsqrt(jnp.mean(x * x, axis=-1, keepdims=True) + EPS) * gain


def _dot(a, b):
    return jnp.dot(a, b, preferred_element_type=F32)


def _dot_nt(a, b):
    return lax.dot_general(a, b, (((1,), (1,)), ((), ())), preferred_element_type=F32)


def _dot_tn(a, b):
    return lax.dot_general(a, b, (((0,), (0,)), ((), ())), preferred_element_type=F32)


def _group_specs(tm, width, prompt_tiles):
    return [
        pl.BlockSpec((tm, width), lambda i: (jnp.minimum(i, prompt_tiles - 1), 0)),
        pl.BlockSpec((tm, width), lambda i: (jnp.maximum(i - prompt_tiles, 0), 0)),
    ]


def _group_pick(prompt_ref, sample_ref, prompt_tiles):
    return jnp.where(pl.program_id(0) < prompt_tiles, prompt_ref[...], sample_ref[...])


def _in_proj_kernel(xp_ref, xs_ref, gain_ref, w_ref, gbias_ref, z_ref, *, prompt_tiles):
    a = _rmsnorm(_group_pick(xp_ref, xs_ref, prompt_tiles), gain_ref[...]).astype(BF16)
    for c0 in range(0, Z_G, 512):
        z_ref[:, c0:c0 + 512] = _dot(a, w_ref[:, c0:c0 + 512])
    gz = _dot(a, w_ref[:, Z_G:]) + gbias_ref[...]
    lane = lax.broadcasted_iota(I32, gz.shape, 1)
    log_f = jnp.minimum(gz, 0.0) - jnp.log(1.0 + jnp.exp(-jnp.abs(gz)))
    z_ref[:, Z_G:] = jnp.where(lane < HEADS, gz, log_f)


def _in_proj(x_p, x_s, gain, w_cat, gbias, tm=256):
    t = x_p.shape[0] + x_s.shape[0]
    pt = x_p.shape[0] // tm
    return pl.pallas_call(
        functools.partial(_in_proj_kernel, prompt_tiles=pt),
        grid=(t // tm,),
        in_specs=_group_specs(tm, D_MODEL, pt) + [
            _resident((1, D_MODEL), lambda i: (0, 0)),
            _resident((D_MODEL, Z_COLS), lambda i: (0, 0)),
            _resident((1, 128), lambda i: (0, 0)),
        ],
        out_specs=pl.BlockSpec((tm, Z_COLS), lambda i: (i, 0)),
        out_shape=jax.ShapeDtypeStruct((t, Z_COLS), F32),
        compiler_params=_cparams(("arbitrary",)),
        name="in_proj",
    )(x_p, x_s, gain, w_cat, gbias)


def _mlstm_chunk(q, k, v, ig_col, lf_col, c_state, n_state, m_state):
    L = q.shape[0]
    row = lax.broadcasted_iota(I32, (L, L), 0)
    col = lax.broadcasted_iota(I32, (L, L), 1)
    diag = row == col
    causal = col <= row
    lf_row = jnp.sum(jnp.where(diag, lf_col, 0.0), axis=0, keepdims=True)
    ig_row = jnp.sum(jnp.where(diag, ig_col, 0.0), axis=0, keepdims=True)
    b_col = jnp.sum(jnp.where(causal, lf_row, 0.0), axis=1, keepdims=True)
    b_row = jnp.sum(jnp.where(row <= col, lf_col, 0.0), axis=0, keepdims=True)
    r_row = ig_row - b_row
    r_col = ig_col - b_col
    run_max = jnp.max(jnp.where(causal, r_row, NEG_INF), axis=1, keepdims=True)
    m_run = jnp.maximum(m_state, run_max)
    w_intra = jnp.exp(jnp.where(causal, r_row - m_run, NEG_INF))
    w_inter = jnp.exp(m_state - m_run)

    qb = q.astype(BF16)
    vb = v.astype(BF16)
    s = _dot_nt(qb, k.astype(BF16)) * w_intra
    num = w_inter * _dot(qb, c_state.astype(BF16)) + _dot(s.astype(BF16), vb)
    qn = jnp.sum(q * n_state, axis=1, keepdims=True)
    den = w_inter * qn + jnp.sum(s, axis=1, keepdims=True)
    hb = num / jnp.maximum(jnp.abs(den), jnp.exp(-(b_col + m_run)))

    m_last = m_run[L - 1:L, :]
    m_new = b_col[L - 1:L, :] + m_last
    kw = k * jnp.exp(r_col - m_last)
    sc = jnp.exp(m_state - m_last)
    c_new = sc * c_state + _dot_tn(kw.astype(BF16), vb)
    n_new = sc * n_state + jnp.sum(kw, axis=0, keepdims=True)
    return hb, c_new, n_new, m_new


def _head_out(hb, gain, o_pre):
    return hb * lax.rsqrt(jnp.mean(hb * hb, axis=-1, keepdims=True) + EPS) * gain * _sigmoid(o_pre)


def _mlstm_prompt_kernel(q_ref, k_ref, v_ref, o_ref, g_ref, gain_ref,
                         hm_ref, c_out_ref, n_out_ref, m_out_ref, c_s, n_s, m_s):
    h = pl.program_id(1)
    c_s[...] = jnp.zeros_like(c_s)
    n_s[...] = jnp.zeros_like(n_s)
    m_s[...] = jnp.zeros_like(m_s)

    def body(ci, carry):
        rows = pl.ds(pl.multiple_of(ci * MLSTM_CHUNK, MLSTM_CHUNK), MLSTM_CHUNK)
        gates = g_ref[rows, :]
        lane = lax.broadcasted_iota(I32, gates.shape, 1)
        ig_col = jnp.sum(jnp.where(lane == h, gates, 0.0), axis=1, keepdims=True)
        lf_col = jnp.sum(jnp.where(lane == h + HEADS, gates, 0.0), axis=1, keepdims=True)
        hb, c_new, n_new, m_new = _mlstm_chunk(
            q_ref[rows, :], k_ref[rows, :] * K_SCALE, v_ref[rows, :],
            ig_col, lf_col, c_s[...], n_s[...], m_s[:, 0:1])
        c_s[...] = c_new
        n_s[...] = n_new
        m_s[...] = jnp.broadcast_to(m_new, m_s.shape)
        hm_ref[rows, :] = _head_out(hb, gain_ref[...], o_ref[rows, :])
        return carry

    lax.fori_loop(0, q_ref.shape[0] // MLSTM_CHUNK, body, 0)
    c_out_ref[...] = c_s[...]
    n_out_ref[...] = n_s[...]
    m_out_ref[...] = m_s[...]


def _mlstm_prompt(z, gain, n_seq, seq_len):
    hpb = MLSTM_WIDTH // HEAD_DIM

    def sec(off):
        return pl.BlockSpec((seq_len, HEAD_DIM), lambda b, h, o=off // HEAD_DIM: (b, o + h))

    nbh = n_seq * HEADS
    return pl.pallas_call(
        _mlstm_prompt_kernel,
        grid=(n_seq, HEADS),
        in_specs=[
            sec(Z_Q), sec(Z_K), sec(Z_V), sec(Z_O),
            pl.BlockSpec((seq_len, 128), lambda b, h: (b, Z_G // 128)),
            pl.BlockSpec((1, HEAD_DIM), lambda b, h: (0, h)),
        ],
        out_specs=[
            pl.BlockSpec((seq_len, HEAD_DIM), lambda b, h: (b, h)),
            pl.BlockSpec((None, HEAD_DIM, HEAD_DIM), lambda b, h: (b * hpb + h, 0, 0)),
            pl.BlockSpec((None, 1, HEAD_DIM), lambda b, h: (b * hpb + h, 0, 0)),
            pl.BlockSpec((None, 1, 128), lambda b, h: (b * hpb + h, 0, 0)),
        ],
        out_shape=[
            jax.ShapeDtypeStruct((n_seq * seq_len, MLSTM_WIDTH), F32),
            jax.ShapeDtypeStruct((nbh, HEAD_DIM, HEAD_DIM), F32),
            jax.ShapeDtypeStruct((nbh, 1, HEAD_DIM), F32),
            jax.ShapeDtypeStruct((nbh, 1, 128), F32),
        ],
        scratch_shapes=[
            pltpu.VMEM((HEAD_DIM, HEAD_DIM), F32),
            pltpu.VMEM((1, HEAD_DIM), F32),
            pltpu.VMEM((1, 128), F32),
        ],
        compiler_params=_cparams(("parallel", "parallel")),
        name="mlstm_prompt",
    )(z, z, z, z, z, gain)


SAMPLE_SEQ = 4
SAMPLE_ROWS = 16


def _mlstm_sample_kernel(z_ref, g_ref, gain_ref, c_in_ref, n_in_ref, m_in_ref,
                         hm_ref, c_out_ref, n_out_ref, m_out_ref):
    for grp in range(SAMPLE_ROWS // 8):
        r0 = grp * 8
        gates = g_ref[r0:r0 + 8, :]
        row = lax.broadcasted_iota(I32, (8, 1), 0)
        for half in range(2):
            lo = half * SAMPLE_SEQ
            live = jnp.logical_and(row >= lo, row < lo + SAMPLE_SEQ)
            for h in range(HEADS):
                j = (grp * 2 + half) * HEADS + h
                cs = slice(h * HEAD_DIM, (h + 1) * HEAD_DIM)
                q = z_ref[r0:r0 + 8, Z_Q + h * HEAD_DIM:Z_Q + (h + 1) * HEAD_DIM]
                k = z_ref[r0:r0 + 8, Z_K + h * HEAD_DIM:Z_K + (h + 1) * HEAD_DIM]
                v = z_ref[r0:r0 + 8, Z_V + h * HEAD_DIM:Z_V + (h + 1) * HEAD_DIM]
                o = z_ref[r0:r0 + 8, Z_O + h * HEAD_DIM:Z_O + (h + 1) * HEAD_DIM]
                k = jnp.where(live, k * K_SCALE, 0.0)
                v = jnp.where(live, v, 0.0)
                ig_col = jnp.where(live, gates[:, h:h + 1], NEG_INF)
                lf_col = jnp.where(live, gates[:, HEADS + h:HEADS + h + 1], 0.0)
                hb, c_new, n_new, m_new = _mlstm_chunk(
                    q, k, v, ig_col, lf_col, c_in_ref[j], n_in_ref[j], m_in_ref[j][:, 0:1])
                out = _head_out(hb, gain_ref[:, cs], o)
                hm_ref[r0 + lo:r0 + lo + SAMPLE_SEQ, cs] = out[lo:lo + SAMPLE_SEQ, :]
                c_out_ref[j] = c_new
                n_out_ref[j] = n_new
                m_out_ref[j] = jnp.broadcast_to(m_new, (1, 128))


def _mlstm_sample(z, gain, c0, n0, m0, row0, n_rows):
    nbh = c0.shape[0]
    per = SAMPLE_ROWS // SAMPLE_SEQ * HEADS
    rb0 = row0 // SAMPLE_ROWS
    return pl.pallas_call(
        _mlstm_sample_kernel,
        grid=(n_rows // SAMPLE_ROWS,),
        in_specs=[
            pl.BlockSpec((SAMPLE_ROWS, Z_U), lambda i: (rb0 + i, 0)),
            pl.BlockSpec((SAMPLE_ROWS, 128), lambda i: (rb0 + i, Z_G // 128)),
            pl.BlockSpec((1, MLSTM_WIDTH), lambda i: (0, 0)),
            pl.BlockSpec((per, HEAD_DIM, HEAD_DIM), lambda i: (i, 0, 0)),
            pl.BlockSpec((per, 1, HEAD_DIM), lambda i: (i, 0, 0)),
            pl.BlockSpec((per, 1, 128), lambda i: (i, 0, 0)),
        ],
        out_specs=[
            pl.BlockSpec((SAMPLE_ROWS, MLSTM_WIDTH), lambda i: (i, 0)),
            pl.BlockSpec((per, HEAD_DIM, HEAD_DIM), lambda i: (i, 0, 0)),
            pl.BlockSpec((per, 1, HEAD_DIM), lambda i: (i, 0, 0)),
            pl.BlockSpec((per, 1, 128), lambda i: (i, 0, 0)),
        ],
        out_shape=[
            jax.ShapeDtypeStruct((n_rows, MLSTM_WIDTH), F32),
            jax.ShapeDtypeStruct((nbh, HEAD_DIM, HEAD_DIM), F32),
            jax.ShapeDtypeStruct((nbh, 1, HEAD_DIM), F32),
            jax.ShapeDtypeStruct((nbh, 1, 128), F32),
        ],
        compiler_params=_cparams(("parallel",)),
        name="mlstm_sample",
    )(z, z, gain, c0, n0, m0)


def _s5_discretise(a_re, a_im, log_dt):
    dt = jnp.exp(log_dt)
    mag = jnp.exp(dt * a_re)
    ab_re = mag * jnp.cos(dt * a_im)
    ab_im = mag * jnp.sin(dt * a_im)
    den = a_re * a_re + a_im * a_im
    xr = ab_re - 1.0
    f_re = (xr * a_re + ab_im * a_im) / den
    f_im = (ab_im * a_re - xr * a_im) / den
    return ab_re, ab_im, f_re, f_im


def _s5_param_kernel(are_e, aim_e, ldt_e, b_re, b_im, are_r, aim_r, ldt_r,
                     bb_re_o, bb_im_o, ab_re_o, ab_im_o):
    _, _, f_re, f_im = _s5_discretise(are_e[...], aim_e[...], ldt_e[...])
    bb_re_o[...] = f_re * b_re[...] - f_im * b_im[...]
    bb_im_o[...] = f_re * b_im[...] + f_im * b_re[...]
    ab_re, ab_im, _, _ = _s5_discretise(are_r[...], aim_r[...], ldt_r[...])
    ab_re_o[...] = ab_re
    ab_im_o[...] = ab_im


def _s5_params(a_re, a_im, log_dt, b_re, b_im):
    g, p = a_re.shape
    c = b_re.shape[-1]
    rep = lambda t: jnp.repeat(t, c, axis=-1)
    ldt_e = jnp.broadcast_to(log_dt[:, None], (g, p * c))
    ldt_r = jnp.broadcast_to(log_dt[:, None], (g, p)).reshape(1, g * p)
    flat = jax.ShapeDtypeStruct((g, p * c), F32)
    rowv = jax.ShapeDtypeStruct((1, g * p), F32)
    return pl.pallas_call(
        _s5_param_kernel,
        out_shape=[flat, flat, rowv, rowv],
        name="s5_params",
    )(rep(a_re), rep(a_im), ldt_e, b_re.reshape(g, p * c), b_im.reshape(g, p * c),
      a_re.reshape(1, g * p), a_im.reshape(1, g * p), ldt_r)


def _blockdiag_in(bb_re, bb_im):
    bb = jnp.stack([bb_re, bb_im]).reshape(2, SSM_BLOCKS, GROUPS_PER_BLOCK, SSM_STATE, SSM_GROUP)
    t = jnp.transpose(bb, (1, 2, 4, 0, 3))
    eye = jnp.eye(GROUPS_PER_BLOCK, dtype=bool)
    w = jnp.where(eye[None, :, None, None, :, None], t[:, :, :, :, None, :], 0.0)
    return w.reshape(SSM_BLOCKS, BLOCK_CH, 2 * BLOCK_ST).astype(BF16)


def _blockdiag_out(cm):
    t = jnp.transpose(cm.reshape(SSM_BLOCKS, GROUPS_PER_BLOCK, SSM_GROUP, SSM_STATE), (0, 1, 3, 2))
    eye = jnp.eye(GROUPS_PER_BLOCK, dtype=bool)
    w = jnp.where(eye[None, :, None, :, None], t[:, :, :, None, :], 0.0)
    return w.reshape(SSM_BLOCKS, BLOCK_ST, BLOCK_CH).astype(BF16)


S5_TILE = 256
S5_SEQS = 4


def _s5_prompt_kernel(u0, u1, u2, u3, wb_ref, wcre_ref, wcim_ref, abre_ref, abim_ref, d_ref,
                      y_ref, sre_ref, sim_ref, u_tm, bu, y_tm, st):
    i = pl.program_id(1)
    half = BLOCK_CH // 2

    @pl.when(i == 0)
    def _():
        st[...] = jnp.zeros_like(st)

    for b, u in enumerate((u0, u1, u2, u3)):
        u_tm[0, pl.ds(b, S5_TILE, stride=S5_SEQS), :] = u[:, :half]
        u_tm[1, pl.ds(b, S5_TILE, stride=S5_SEQS), :] = u[:, half:]
    u_all = jnp.concatenate([u_tm[0], u_tm[1]], axis=1)
    bu[...] = _dot(u_all.astype(BF16), wb_ref[...])
    a_re = abre_ref[...]
    a_im = abim_ref[...]

    first = lax.broadcasted_iota(I32, (2 * S5_SEQS, BLOCK_ST), 0) < S5_SEQS

    def body(t2, carry):
        s_re, s_im = carry
        rows = pl.ds(pl.multiple_of(t2 * 2 * S5_SEQS, 2 * S5_SEQS), 2 * S5_SEQS)
        x_re = bu[rows, :BLOCK_ST]
        x_im = bu[rows, BLOCK_ST:]
        p_re = a_re * s_re - a_im * s_im + x_re
        p_im = a_re * s_im + a_im * s_re + x_im
        r_re = pltpu.roll(p_re, S5_SEQS, 0)
        r_im = pltpu.roll(p_im, S5_SEQS, 0)
        q_re = a_re * r_re - a_im * r_im + x_re
        q_im = a_re * r_im + a_im * r_re + x_im
        bu[rows, :BLOCK_ST] = jnp.where(first, p_re, q_re)
        bu[rows, BLOCK_ST:] = jnp.where(first, p_im, q_im)
        return pltpu.roll(q_re, S5_SEQS, 0), pltpu.roll(q_im, S5_SEQS, 0)

    s_re, s_im = lax.fori_loop(0, S5_TILE // 2, body, (st[:, :BLOCK_ST], st[:, BLOCK_ST:]), unroll=2)
    st[:, :BLOCK_ST] = s_re
    st[:, BLOCK_ST:] = s_im
    y = (_dot(bu[:, :BLOCK_ST].astype(BF16), wcre_ref[...])
         - _dot(bu[:, BLOCK_ST:].astype(BF16), wcim_ref[...]) + d_ref[...] * u_all)
    y = _gelu(y)
    y_tm[0] = y[:, :half]
    y_tm[1] = y[:, half:]
    for b in range(S5_SEQS):
        y_ref[b, :, :half] = y_tm[0, pl.ds(b, S5_TILE, stride=S5_SEQS), :]
        y_ref[b, :, half:] = y_tm[1, pl.ds(b, S5_TILE, stride=S5_SEQS), :]

    @pl.when(i == pl.num_programs(1) - 1)
    def _():
        sre_ref[...] = s_re[:S5_SEQS, :]
        sim_ref[...] = s_im[:S5_SEQS, :]


def _s5_prompt(z, wb, wcre, wcim, ab_re, ab_im, d_row, seq_len):
    nt = seq_len // S5_TILE
    ucol = Z_U // BLOCK_CH

    def u_spec(b):
        return pl.BlockSpec((S5_TILE, BLOCK_CH), lambda j, i, b=b: (b * nt + i, ucol + j))

    rows = S5_TILE * S5_SEQS
    return pl.pallas_call(
        _s5_prompt_kernel,
        grid=(SSM_BLOCKS, nt),
        in_specs=[u_spec(b) for b in range(S5_SEQS)] + [
            pl.BlockSpec((None, BLOCK_CH, 2 * BLOCK_ST), lambda j, i: (j, 0, 0)),
            pl.BlockSpec((None, BLOCK_ST, BLOCK_CH), lambda j, i: (j, 0, 0)),
            pl.BlockSpec((None, BLOCK_ST, BLOCK_CH), lambda j, i: (j, 0, 0)),
            pl.BlockSpec((1, BLOCK_ST), lambda j, i: (0, j)),
            pl.BlockSpec((1, BLOCK_ST), lambda j, i: (0, j)),
            pl.BlockSpec((1, BLOCK_CH), lambda j, i: (0, j)),
        ],
        out_specs=[
            pl.BlockSpec((S5_SEQS, S5_TILE, BLOCK_CH), lambda j, i: (0, i, j)),
            pl.BlockSpec((S5_SEQS, BLOCK_ST), lambda j, i: (0, j)),
            pl.BlockSpec((S5_SEQS, BLOCK_ST), lambda j, i: (0, j)),
        ],
        out_shape=[
            jax.ShapeDtypeStruct((S5_SEQS, seq_len, SSM_WIDTH), F32),
            jax.ShapeDtypeStruct((S5_SEQS, SSM_GROUPS * SSM_STATE), F32),
            jax.ShapeDtypeStruct((S5_SEQS, SSM_GROUPS * SSM_STATE), F32),
        ],
        scratch_shapes=[
            pltpu.VMEM((2, rows, 128), F32),
            pltpu.VMEM((rows, 2 * BLOCK_ST), F32),
            pltpu.VMEM((2, rows, 128), F32),
            pltpu.VMEM((2 * S5_SEQS, 2 * BLOCK_ST), F32),
        ],
        compiler_params=_cparams(("parallel", "arbitrary")),
        name="s5_prompt",
    )(z, z, z, z, wb, wcre, wcim, ab_re, ab_im, d_row)


def _s5_sample_kernel(u_ref, s0re_ref, s0im_ref, wb_ref, wcre_ref, wcim_ref, abre_ref, abim_ref,
                      d_ref, y_ref, sre_ref, sim_ref, u_sl, y_sl):
    half = BLOCK_CH // 2
    n_seq = s0re_ref.shape[0]
    u_sl[0] = u_ref[:, :half]
    u_sl[1] = u_ref[:, half:]
    a_re = abre_ref[...]
    a_im = abim_ref[...]
    s_re = s0re_ref[...]
    s_im = s0im_ref[...]
    for t in range(SAMPLE_SEQ):
        rows = pl.ds(t, n_seq, stride=SAMPLE_SEQ)
        u_t = jnp.concatenate([u_sl[0, rows, :], u_sl[1, rows, :]], axis=1)
        bu = _dot(u_t.astype(BF16), wb_ref[...])
        n_re = a_re * s_re - a_im * s_im + bu[:, :BLOCK_ST]
        n_im = a_re * s_im + a_im * s_re + bu[:, BLOCK_ST:]
        s_re, s_im = n_re, n_im
        y = (_dot(s_re.astype(BF16), wcre_ref[...]) - _dot(s_im.astype(BF16), wcim_ref[...])
             + d_ref[...] * u_t)
        y = _gelu(y)
        y_sl[0, rows, :] = y[:, :half]
        y_sl[1, rows, :] = y[:, half:]
    y_ref[:, :half] = y_sl[0]
    y_ref[:, half:] = y_sl[1]
    sre_ref[...] = s_re
    sim_ref[...] = s_im


def _s5_sample(z, s0_re, s0_im, wb, wcre, wcim, ab_re, ab_im, d_row, row0, n_rows):
    n_seq = s0_re.shape[0]
    ucol = Z_U // BLOCK_CH
    st_spec = pl.BlockSpec((n_seq, BLOCK_ST), lambda j: (0, j))
    return pl.pallas_call(
        _s5_sample_kernel,
        grid=(SSM_BLOCKS,),
        in_specs=[
            pl.BlockSpec((n_rows, BLOCK_CH), lambda j: (row0 // n_rows, ucol + j)),
            st_spec, st_spec,
            pl.BlockSpec((None, BLOCK_CH, 2 * BLOCK_ST), lambda j: (j, 0, 0)),
            pl.BlockSpec((None, BLOCK_ST, BLOCK_CH), lambda j: (j, 0, 0)),
            pl.BlockSpec((None, BLOCK_ST, BLOCK_CH), lambda j: (j, 0, 0)),
            pl.BlockSpec((1, BLOCK_ST), lambda j: (0, j)),
            pl.BlockSpec((1, BLOCK_ST), lambda j: (0, j)),
            pl.BlockSpec((1, BLOCK_CH), lambda j: (0, j)),
        ],
        out_specs=[pl.BlockSpec((n_rows, BLOCK_CH), lambda j: (0, j)), st_spec, st_spec],
        out_shape=[
            jax.ShapeDtypeStruct((n_rows, SSM_WIDTH), F32),
            jax.ShapeDtypeStruct(s0_re.shape, F32),
            jax.ShapeDtypeStruct(s0_im.shape, F32),
        ],
        scratch_shapes=[pltpu.VMEM((2, n_rows, 128), F32), pltpu.VMEM((2, n_rows, 128), F32)],
        compiler_params=_cparams(("parallel",)),
        name="s5_sample",
    )(z, s0_re, s0_im, wb, wcre, wcim, ab_re, ab_im, d_row)


def _postmix_kernel(hmp_ref, hms_ref, ysp_ref, yss_ref, xp_ref, xs_ref,
                    wglu_ref, bglu_ref, wout_ref, gain_ref, wq_ref, keys_ref,
                    h1_ref, c_ref, sp_ref, ss_ref, *, prompt_tiles):
    i = pl.program_id(0)
    ys = _group_pick(ysp_ref, yss_ref, prompt_tiles)
    hm = _group_pick(hmp_ref, hms_ref, prompt_tiles)
    glu = ys * _sigmoid(_dot(ys.astype(BF16), wglu_ref[...]) + bglu_ref[...])
    mix = (_dot(hm.astype(BF16), wout_ref[:MLSTM_WIDTH, :])
           + _dot(glu.astype(BF16), wout_ref[MLSTM_WIDTH:, :]))
    h1 = _group_pick(xp_ref, xs_ref, prompt_tiles) + mix
    h1_ref[...] = h1
    c = _rmsnorm(h1, gain_ref[...]).astype(BF16)
    c_ref[...] = c
    qp = _dot(c, wq_ref[...])
    scores = [_dot_nt(keys_ref[j], qp[:, j * PEER_HALF:(j + 1) * PEER_HALF].astype(BF16))
              for j in range(2 * PEER_HEADS)]

    @pl.when(i < prompt_tiles)
    def _():
        for j, s in enumerate(scores):
            sp_ref[j] = s

    @pl.when(i >= prompt_tiles)
    def _():
        for j, s in enumerate(scores):
            ss_ref[j] = s


def _postmix(hm_p, hm_s, ys_p, ys_s, x_p, x_s, wglu, bglu, wout, gain, wq, keys, tm=256):
    n_p, n_s = x_p.shape[0], x_s.shape[0]
    t = n_p + n_s
    pt = n_p // tm
    nk = 2 * PEER_HEADS
    return pl.pallas_call(
        functools.partial(_postmix_kernel, prompt_tiles=pt),
        grid=(t // tm,),
        in_specs=_group_specs(tm, MLSTM_WIDTH, pt) + _group_specs(tm, SSM_WIDTH, pt)
        + _group_specs(tm, D_MODEL, pt) + [
            _resident((SSM_WIDTH, SSM_WIDTH), lambda i: (0, 0)),
            _resident((1, SSM_WIDTH), lambda i: (0, 0)),
            _resident((D_MODEL, D_MODEL), lambda i: (0, 0)),
            _resident((1, D_MODEL), lambda i: (0, 0)),
            _resident((D_MODEL, PEER_HEADS * 2 * PEER_HALF), lambda i: (0, 0)),
            _resident((nk, PEER_KEYS, PEER_HALF), lambda i: (0, 0, 0)),
        ],
        out_specs=[
            pl.BlockSpec((tm, D_MODEL), lambda i: (i, 0)),
            pl.BlockSpec((tm, D_MODEL), lambda i: (i, 0)),
            pl.BlockSpec((nk, PEER_KEYS, tm), lambda i: (0, 0, jnp.minimum(i, pt - 1))),
            pl.BlockSpec((nk, PEER_KEYS, tm), lambda i: (0, 0, jnp.maximum(i - pt, 0))),
        ],
        out_shape=[
            jax.ShapeDtypeStruct((t, D_MODEL), F32),
            jax.ShapeDtypeStruct((t, D_MODEL), BF16),
            jax.ShapeDtypeStruct((nk, PEER_KEYS, n_p), F32),
            jax.ShapeDtypeStruct((nk, PEER_KEYS, n_s), F32),
        ],
        compiler_params=_cparams(("arbitrary",)),
        name="postmix",
    )(hm_p, hm_s, ys_p, ys_s, x_p, x_s, wglu, bglu, wout, gain, wq, keys)


ID_NONE = 1 << 20
TOPK_SUB = 8


def _sort_network(n):
    pairs = []
    p = 1
    while p < n:
        k = p
        while k >= 1:
            for j in range(k % p, n - k, 2 * k):
                for i in range(min(k, n - j - k)):
                    if (i + j) // (2 * p) == (i + j + k) // (2 * p):
                        pairs.append((i + j, i + j + k))
            k //= 2
        p *= 2
    return pairs


_SORT16 = _sort_network(PEER_TOPK)


def _precedes(b, a):
    (vb, ib), (va, ia) = b, a
    return jnp.logical_or(vb > va, jnp.logical_and(vb == va, ib < ia))


def _first_of(a, b):
    sw = _precedes(b, a)
    return jnp.where(sw, b[0], a[0]), jnp.where(sw, b[1], a[1])


def _exchange(items, i, j):
    a, b = items[i], items[j]
    sw = _precedes(b, a)
    items[i] = (jnp.where(sw, b[0], a[0]), jnp.where(sw, b[1], a[1]))
    items[j] = (jnp.where(sw, a[0], b[0]), jnp.where(sw, a[1], b[1]))


def _sort16(items):
    items = list(items)
    for i, j in _SORT16:
        _exchange(items, i, j)
    return items


def _bitonic_merge16(items):
    items = list(items)
    d = PEER_TOPK // 2
    while d >= 1:
        for i in range(PEER_TOPK):
            if i & d == 0:
                _exchange(items, i, i + d)
        d //= 2
    return items


def _merge_top16(a, b):
    return _bitonic_merge16([_first_of(a[i], b[PEER_TOPK - 1 - i]) for i in range(PEER_TOPK)])


def _top16_of_keys(s_ref, half, shape):
    best = None
    for g in range(PEER_KEYS // PEER_TOPK):
        grp = _sort16([(s_ref[half, g * PEER_TOPK + k], jnp.full(shape, g * PEER_TOPK + k, I32))
                       for k in range(PEER_TOPK)])
        best = grp if best is None else _merge_top16(best, grp)
    return best


def _topk_kernel(s_ref, e1_ref, e2_ref, g_ref):
    shape = s_ref.shape[2:]
    top1 = _top16_of_keys(s_ref, 0, shape)
    top2 = _top16_of_keys(s_ref, 1, shape)

    def pair(i, j):
        return top1[i][0] + top2[j][0], jnp.full(shape, i * PEER_TOPK + j, I32)

    pad = (jnp.full(shape, NEG_INF, F32), jnp.full(shape, ID_NONE, I32))
    g0 = [pair(0, j) for j in range(16)]
    g1 = _bitonic_merge16([pair(1, j) for j in range(8)] + [pair(i, 0) for i in range(15, 7, -1)])
    g2 = _sort16([pair(i, j) for i in range(2, 7) for j in range(PEER_TOPK // (i + 1))])
    g3 = [pair(7, 0), pair(7, 1)] + [pad] * 14
    best = _merge_top16(_merge_top16(g0, g1), _merge_top16(g2, g3))

    mx = best[0][0]
    exps = []
    for k in range(PEER_TOPK):
        v, pid = best[k]
        a = lax.shift_right_logical(pid, 4)
        b = jnp.bitwise_and(pid, PEER_TOPK - 1)
        e1 = jnp.zeros(shape, I32)
        e2 = jnp.zeros(shape, I32)
        for r in range(PEER_TOPK):
            e1 = jnp.where(a == r, top1[r][1], e1)
            e2 = jnp.where(b == r, top2[r][1], e2)
        e1_ref[k] = e1
        e2_ref[k] = e2
        exps.append(jnp.exp(v - mx))
    total = exps[0]
    for k in range(1, PEER_TOPK):
        total = total + exps[k]
    for k in range(PEER_TOPK):
        g_ref[k] = exps[k] / total


def _topk(scores):
    t = scores.shape[-1]
    ng = t // 128
    sub = min(ng, TOPK_SUB)
    r = PEER_HEADS * PEER_TOPK
    o_spec = pl.BlockSpec((PEER_TOPK, sub, 128), lambda i, h: (h, i, 0))
    outs = pl.pallas_call(
        _topk_kernel,
        grid=(ng // sub, PEER_HEADS),
        in_specs=[pl.BlockSpec((2, PEER_KEYS, sub, 128), lambda i, h: (h, 0, i, 0))],
        out_specs=[o_spec, o_spec, o_spec],
        out_shape=[
            jax.ShapeDtypeStruct((r, ng, 128), I32),
            jax.ShapeDtypeStruct((r, ng, 128), I32),
            jax.ShapeDtypeStruct((r, ng, 128), F32),
        ],
        compiler_params=_cparams(("parallel", "parallel")),
        name="topk",
    )(scores.reshape(2 * PEER_HEADS, PEER_KEYS, ng, 128))
    return [o.reshape(r, t) for o in outs]


WB_GROUP = 16
WB_PITCH = 132


def _wbuild_kernel(e1_ref, e2_ref, g_ref, w_ref, stage):
    sub = lax.broadcasted_iota(I32, (PEER_KEYS, PEER_KEYS), 0)

    def group(gi, carry):
        t0 = pl.multiple_of(gi * WB_GROUP, WB_GROUP)
        for tt in range(WB_GROUP):
            row = pl.ds(t0 + tt, 1)
            onehot1 = jnp.where(sub == e1_ref[row, :], 1.0, 0.0).astype(BF16)
            gated2 = jnp.where(sub == e2_ref[row, :], g_ref[row, :], 0.0).astype(BF16)
            stage[tt * WB_PITCH:tt * WB_PITCH + PEER_KEYS, :] = _dot_nt(onehot1, gated2)
        for e in range(PEER_KEYS):
            blk = stage[pl.ds(e, WB_GROUP, stride=WB_PITCH), :]
            w_ref[e, pl.ds(t0, WB_GROUP), :] = blk.astype(BF16)
        return carry

    lax.fori_loop(0, e1_ref.shape[0] // WB_GROUP, group, 0)


def _wbuild(e1, e2, g, tw=128):
    t = e1.shape[0]
    i_spec = pl.BlockSpec((tw, PEER_KEYS), lambda i: (i, 0))
    return pl.pallas_call(
        _wbuild_kernel,
        grid=(t // tw,),
        in_specs=[i_spec, i_spec, i_spec],
        out_specs=pl.BlockSpec((PEER_KEYS, tw, PEER_KEYS), lambda i: (0, i, 0)),
        out_shape=jax.ShapeDtypeStruct((PEER_KEYS, t, PEER_KEYS), BF16),
        scratch_shapes=[pltpu.VMEM((WB_GROUP * WB_PITCH, PEER_KEYS), F32)],
        compiler_params=_cparams(("parallel",)),
        name="wbuild",
    )(e1, e2, g)


PEER_EB = 512


def _peer_kernel(c_ref, u_ref, v_ref, w_ref, o_ref, wact_s):
    @pl.when(pl.program_id(1) == 0)
    def _():
        o_ref[...] = jnp.zeros_like(o_ref)
        wact_s[...] = jnp.zeros_like(wact_s)

    o_ref[...] += _dot(wact_s[...], v_ref[...].astype(BF16))
    act = _gelu(_dot_nt(c_ref[...], u_ref[...].astype(BF16)))
    for k in range(PEER_EB // PEER_KEYS):
        cols = slice(k * PEER_KEYS, (k + 1) * PEER_KEYS)
        wact_s[:, cols] = (act[:, cols] * w_ref[k].astype(F32)).astype(BF16)


def _peer(c, u, v, w3, n_tiles=8):
    t = c.shape[0]
    tm = t // n_tiles
    nb = PEER_EXPERTS // PEER_EB
    return pl.pallas_call(
        _peer_kernel,
        grid=(n_tiles, nb + 1),
        in_specs=[
            pl.BlockSpec((tm, D_MODEL), lambda i, j: (i, 0), pipeline_mode=pl.Buffered(1)),
            pl.BlockSpec((PEER_EB, D_MODEL), lambda i, j: (jnp.minimum(j, nb - 1), 0)),
            pl.BlockSpec((PEER_EB, D_MODEL), lambda i, j: (jnp.maximum(j - 1, 0), 0)),
            pl.BlockSpec((PEER_EB // PEER_KEYS, tm, PEER_KEYS),
                         lambda i, j: (jnp.minimum(j, nb - 1), i, 0)),
        ],
        out_specs=pl.BlockSpec((tm, D_MODEL), lambda i, j: (i, 0)),
        out_shape=jax.ShapeDtypeStruct((t, D_MODEL), F32),
        scratch_shapes=[pltpu.VMEM((tm, PEER_EB), BF16)],
        compiler_params=_cparams(("parallel", "arbitrary")),
        name="peer",
    )(c, u, v, w3)


def _tail_kernel(h1_ref, peer_ref, pp_ref, ps_ref, gple_ref, wgate_ref, wproj_ref, gfin_ref,
                 yp_ref, ys_ref, *, prompt_tiles):
    i = pl.program_id(0)
    h2 = h1_ref[...] + peer_ref[...]
    gate = _sigmoid(_dot(_rmsnorm(h2, gple_ref[...]).astype(BF16), wgate_ref[...]))
    e = _dot(_group_pick(pp_ref, ps_ref, prompt_tiles).astype(BF16), wproj_ref[...])
    y = _rmsnorm(h2 + e * gate, gfin_ref[...])

    @pl.when(i < prompt_tiles)
    def _():
        yp_ref[...] = y

    @pl.when(i >= prompt_tiles)
    def _():
        ys_ref[...] = y


def _tail(h1, peer, p_p, p_s, gple, wgate, wproj, gfin, tm=256):
    t = h1.shape[0]
    n_prompt = p_p.shape[0]
    pt = n_prompt // tm
    ple = p_p.shape[1]
    return pl.pallas_call(
        functools.partial(_tail_kernel, prompt_tiles=pt),
        grid=(t // tm,),
        in_specs=[
            pl.BlockSpec((tm, D_MODEL), lambda i: (i, 0)),
            pl.BlockSpec((tm, D_MODEL), lambda i: (i, 0)),
        ] + _group_specs(tm, ple, pt) + [
            _resident((1, D_MODEL), lambda i: (0, 0)),
            _resident((D_MODEL, D_MODEL), lambda i: (0, 0)),
            _resident((ple, D_MODEL), lambda i: (0, 0)),
            _resident((1, D_MODEL), lambda i: (0, 0)),
        ],
        out_specs=[
            pl.BlockSpec((tm, D_MODEL), lambda i: (jnp.minimum(i, pt - 1), 0)),
            pl.BlockSpec((tm, D_MODEL), lambda i: (jnp.maximum(i - pt, 0), 0)),
        ],
        out_shape=[
            jax.ShapeDtypeStruct((n_prompt, D_MODEL), F32),
            jax.ShapeDtypeStruct((t - n_prompt, D_MODEL), F32),
        ],
        compiler_params=_cparams(("arbitrary",)),
        name="tail",
    )(h1, peer, p_p, p_s, gple, wgate, wproj, gfin)


def kernel(x_prompt, x_sample, state_mlstm_C, state_mlstm_n, state_mlstm_m, state_ssm_re, state_ssm_im, p_prompt, p_sample, norm_mix, w_in, b_igate, b_fgate, mlstm_norm, ssm_A_re, ssm_A_im, ssm_B_re, ssm_B_im, ssm_C_re, ssm_C_im, ssm_D, ssm_log_dt, w_glu, b_glu, w_out, norm_ffn, peer_w_q, peer_keys, peer_u, peer_v, norm_ple, w_ple_gate, w_ple_proj, norm_final):
    n_pseq, p_len, _ = x_prompt.shape
    n_sseq, s_len, _ = x_sample.shape
    assert s_len == SAMPLE_SEQ and n_pseq == S5_SEQS and w_in.shape[0] == 1
    n_prompt = n_pseq * p_len
    n_sample = n_sseq * s_len
    row = lambda t: t.reshape(1, -1)

    x_p = x_prompt.reshape(n_prompt, D_MODEL)
    x_s = x_sample.reshape(n_sample, D_MODEL)

    w = w_in[0]
    n_gate = 2 * HEADS
    w_cat = jnp.concatenate(
        [w[:, :4 * MLSTM_WIDTH], w[:, 4 * MLSTM_WIDTH + n_gate:],
         w[:, 4 * MLSTM_WIDTH:4 * MLSTM_WIDTH + n_gate],
         jnp.zeros((D_MODEL, 128 - n_gate), F32)], axis=1).astype(BF16)
    gbias = jnp.concatenate([b_igate[0], b_fgate[0], jnp.zeros((128 - n_gate,), F32)]).reshape(1, 128)
    z = _in_proj(x_p, x_s, row(norm_mix[0]), w_cat, gbias)

    gain_m = row(mlstm_norm[0])
    hm_p, c_p, n_p, m_p = _mlstm_prompt(z, gain_m, n_pseq, p_len)
    nbh = n_sseq * HEADS
    hm_s, c_s, n_s, m_s = _mlstm_sample(
        z, gain_m,
        state_mlstm_C[0].reshape(nbh, HEAD_DIM, HEAD_DIM),
        state_mlstm_n[0].reshape(nbh, 1, HEAD_DIM),
        jnp.broadcast_to(state_mlstm_m[0].reshape(nbh, 1, 1), (nbh, 1, 128)),
        n_prompt, n_sample)

    bb_re, bb_im, ab_re, ab_im = _s5_params(ssm_A_re[0], ssm_A_im[0], ssm_log_dt[0], ssm_B_re[0], ssm_B_im[0])
    wb = _blockdiag_in(bb_re, bb_im)
    wcre = _blockdiag_out(ssm_C_re[0])
    wcim = _blockdiag_out(ssm_C_im[0])
    d_row = row(ssm_D[0])
    ys_p, sre_p, sim_p = _s5_prompt(z, wb, wcre, wcim, ab_re, ab_im, d_row, p_len)
    n_st = SSM_GROUPS * SSM_STATE
    ys_s, sre_s, sim_s = _s5_sample(
        z, state_ssm_re[0].reshape(n_sseq, n_st), state_ssm_im[0].reshape(n_sseq, n_st),
        wb, wcre, wcim, ab_re, ab_im, d_row, n_prompt, n_sample)

    keys = peer_keys[0].reshape(2 * PEER_HEADS, PEER_KEYS, PEER_HALF).astype(BF16)
    h1, c, scores_p, scores_s = _postmix(
        hm_p, hm_s, ys_p.reshape(n_prompt, SSM_WIDTH), ys_s, x_p, x_s,
        w_glu[0].astype(BF16), row(b_glu[0]), w_out[0].astype(BF16),
        row(norm_ffn[0]), peer_w_q[0].astype(BF16), keys)
    routing = [jnp.concatenate([rp, rs], axis=1).T for rp, rs in zip(_topk(scores_p), _topk(scores_s))]
    w3 = _wbuild(*routing)
    peer = _peer(c, peer_u[0], peer_v[0], w3)

    y_p, y_s = _tail(h1, peer, p_prompt[0].reshape(n_prompt, -1), p_sample[0].reshape(n_sample, -1),
                     row(norm_ple[0]), w_ple_gate[0].astype(BF16), w_ple_proj[0].astype(BF16),
                     row(norm_final))

    st_shape = (1, -1, SSM_GROUPS, SSM_STATE)
    return (
        y_p.reshape(x_prompt.shape), y_s.reshape(x_sample.shape),
        c_p.reshape(1, n_pseq, HEADS, HEAD_DIM, HEAD_DIM), n_p.reshape(1, n_pseq, HEADS, HEAD_DIM),
        m_p[:, 0, 0].reshape(1, n_pseq, HEADS),
        sre_p.reshape(st_shape), sim_p.reshape(st_shape),
        c_s.reshape(1, n_sseq, HEADS, HEAD_DIM, HEAD_DIM), n_s.reshape(1, n_sseq, HEADS, HEAD_DIM),
        m_s[:, 0, 0].reshape(1, n_sseq, HEADS),
        sre_s.reshape(st_shape), sim_s.reshape(st_shape),
    )
```

```python
import functools

import jax
import jax.numpy as jnp
from jax import lax
from jax.experimental import pallas as pl
from jax.experimental.pallas import tpu as pltpu

F32 = jnp.float32
BF16 = jnp.bfloat16
I32 = jnp.int32

EPS = 1e-6
D_MODEL = 2048
HEADS = 4
HEAD_DIM = 256
MLSTM_WIDTH = HEADS * HEAD_DIM
SSM_WIDTH = 1024
SSM_GROUPS = 64
SSM_STATE = 64
SSM_GROUP = 16
GROUPS_PER_BLOCK = 16
SSM_BLOCKS = SSM_GROUPS // GROUPS_PER_BLOCK
BLOCK_CH = GROUPS_PER_BLOCK * SSM_GROUP
BLOCK_ST = GROUPS_PER_BLOCK * SSM_STATE
PEER_HEADS = 8
PEER_KEYS = 128
PEER_TOPK = 16
PEER_HALF = 64
PEER_EXPERTS = PEER_KEYS * PEER_KEYS

Z_Q, Z_K, Z_V, Z_O, Z_U = 0, 1024, 2048, 3072, 4096
ZT_G = 1024
ZT_COLS = ZT_G + 128
IN_PROJ_COLS = 2048

MLSTM_CHUNK = 256
K_SCALE = HEAD_DIM ** -0.5
NEG_INF = float("-inf")

VMEM_LIMIT = 56 * 1024 * 1024


def _cparams(semantics):
    return pltpu.CompilerParams(dimension_semantics=semantics, vmem_limit_bytes=VMEM_LIMIT)


def _resident(shape, index_map):
    return pl.BlockSpec(shape, index_map, pipeline_mode=pl.Buffered(1))


def _sigmoid(x):
    return 1.0 / (1.0 + jnp.exp(-x))


def _gelu(x):
    return x * (0.5 * (1.0 + jnp.tanh(0.7978845608028654 * (x + 0.044715 * (x * x * x)))))


def _rmsnorm(x, gain):
    return x * lax.rsqrt(jnp.mean(x * x, axis=-1, keepdims=True) + EPS) * gain


def _dot(a, b):
    return jnp.dot(a, b, preferred_element_type=F32)


def _dot_nt(a, b):
    return lax.dot_general(a, b, (((1,), (1,)), ((), ())), preferred_element_type=F32)


def _dot_tn(a, b):
    return lax.dot_general(a, b, (((0,), (0,)), ((), ())), preferred_element_type=F32)


def _group_specs(tm, width, prompt_tiles):
    return [
        pl.BlockSpec((tm, width), lambda i: (jnp.minimum(i, prompt_tiles - 1), 0)),
        pl.BlockSpec((tm, width), lambda i: (jnp.maximum(i - prompt_tiles, 0), 0)),
    ]


def _group_pick(prompt_ref, sample_ref, prompt_tiles):
    return jnp.where(pl.program_id(0) < prompt_tiles, prompt_ref[...], sample_ref[...])


def _in_proj_kernel(xp_ref, xs_ref, gain_ref, wm_ref, wt_ref, gbias_ref, zm_ref, zt_ref, a_s,
                    *, prompt_tiles, main_steps):
    j = pl.program_id(1)

    @pl.when(j == 0)
    def _():
        a_s[...] = _rmsnorm(_group_pick(xp_ref, xs_ref, prompt_tiles), gain_ref[...]).astype(BF16)

    @pl.when(j < main_steps)
    def _():
        for c0 in range(0, IN_PROJ_COLS, 512):
            zm_ref[:, c0:c0 + 512] = _dot(a_s[...], wm_ref[:, c0:c0 + 512])

    @pl.when(j == main_steps)
    def _():
        a = a_s[...]
        for c0 in range(0, ZT_G, 512):
            zt_ref[:, c0:c0 + 512] = _dot(a, wt_ref[:, c0:c0 + 512])
        gz = _dot(a, wt_ref[:, ZT_G:]) + gbias_ref[...]
        lane = lax.broadcasted_iota(I32, gz.shape, 1)
        log_f = jnp.minimum(gz, 0.0) - jnp.log(1.0 + jnp.exp(-jnp.abs(gz)))
        zt_ref[:, ZT_G:] = jnp.where(lane < HEADS, gz, log_f)


def _in_proj(x_p, x_s, gain, w_bf, w_tail, gbias, tm=512):
    t = x_p.shape[0] + x_s.shape[0]
    pt = x_p.shape[0] // tm
    ms = Z_U // IN_PROJ_COLS
    return pl.pallas_call(
        functools.partial(_in_proj_kernel, prompt_tiles=pt, main_steps=ms),
        grid=(t // tm, ms + 1),
        in_specs=[
            pl.BlockSpec((tm, D_MODEL), lambda i, j: (jnp.minimum(i, pt - 1), 0)),
            pl.BlockSpec((tm, D_MODEL), lambda i, j: (jnp.maximum(i - pt, 0), 0),
                         pipeline_mode=pl.Buffered(1)),
            _resident((1, D_MODEL), lambda i, j: (0, 0)),
            pl.BlockSpec((D_MODEL, IN_PROJ_COLS), lambda i, j: (0, jnp.minimum(j, ms - 1))),
            _resident((D_MODEL, ZT_COLS), lambda i, j: (0, 0)),
            _resident((1, 128), lambda i, j: (0, 0)),
        ],
        out_specs=[
            pl.BlockSpec((tm, IN_PROJ_COLS), lambda i, j: (i, jnp.minimum(j, ms - 1))),
            pl.BlockSpec((tm, ZT_COLS), lambda i, j: (i, 0)),
        ],
        out_shape=[
            jax.ShapeDtypeStruct((t, Z_U), F32),
            jax.ShapeDtypeStruct((t, ZT_COLS), F32),
        ],
        scratch_shapes=[pltpu.VMEM((tm, D_MODEL), BF16)],
        compiler_params=_cparams(("arbitrary", "arbitrary")),
        name="in_proj",
    )(x_p, x_s, gain, w_bf, w_tail, gbias)


def _mlstm_chunk(q, k, v, ig_col, lf_col, c_state, n_state, m_state):
    L = q.shape[0]
    row = lax.broadcasted_iota(I32, (L, L), 0)
    col = lax.broadcasted_iota(I32, (L, L), 1)
    diag = row == col
    causal = col <= row
    lf_row = jnp.sum(jnp.where(diag, lf_col, 0.0), axis=0, keepdims=True)
    ig_row = jnp.sum(jnp.where(diag, ig_col, 0.0), axis=0, keepdims=True)
    b_col = jnp.sum(jnp.where(causal, lf_row, 0.0), axis=1, keepdims=True)
    b_row = jnp.sum(jnp.where(row <= col, lf_col, 0.0), axis=0, keepdims=True)
    r_row = ig_row - b_row
    r_col = ig_col - b_col
    run_max = jnp.max(jnp.where(causal, r_row, NEG_INF), axis=1, keepdims=True)
    m_run = jnp.maximum(m_state, run_max)
    w_intra = jnp.exp(jnp.where(causal, r_row - m_run, NEG_INF))
    w_inter = jnp.exp(m_state - m_run)

    qb = q.astype(BF16)
    vb = v.astype(BF16)
    s = _dot_nt(qb, k.astype(BF16)) * w_intra
    num = w_inter * _dot(qb, c_state.astype(BF16)) + _dot(s.astype(BF16), vb)
    qn = jnp.sum(q * n_state, axis=1, keepdims=True)
    den = w_inter * qn + jnp.sum(s, axis=1, keepdims=True)
    hb = num / jnp.maximum(jnp.abs(den), jnp.exp(-(b_col + m_run)))

    m_last = m_run[L - 1:L, :]
    m_new = b_col[L - 1:L, :] + m_last
    kw = k * jnp.exp(r_col - m_last)
    sc = jnp.exp(m_state - m_last)
    c_new = sc * c_state + _dot_tn(kw.astype(BF16), vb)
    n_new = sc * n_state + jnp.sum(kw, axis=0, keepdims=True)
    return hb, c_new, n_new, m_new


def _head_out(hb, gain, o_pre):
    return hb * lax.rsqrt(jnp.mean(hb * hb, axis=-1, keepdims=True) + EPS) * gain * _sigmoid(o_pre)


def _mlstm_prompt_kernel(q_ref, k_ref, v_ref, o_ref, g_ref, gain_ref,
                         hm_ref, c_out_ref, n_out_ref, m_out_ref, c_s, n_s, m_s):
    h = pl.program_id(1)
    c_s[...] = jnp.zeros_like(c_s)
    n_s[...] = jnp.zeros_like(n_s)
    m_s[...] = jnp.zeros_like(m_s)

    def body(ci, carry):
        rows = pl.ds(pl.multiple_of(ci * MLSTM_CHUNK, MLSTM_CHUNK), MLSTM_CHUNK)
        gates = g_ref[rows, :]
        lane = lax.broadcasted_iota(I32, gates.shape, 1)
        ig_col = jnp.sum(jnp.where(lane == h, gates, 0.0), axis=1, keepdims=True)
        lf_col = jnp.sum(jnp.where(lane == h + HEADS, gates, 0.0), axis=1, keepdims=True)
        hb, c_new, n_new, m_new = _mlstm_chunk(
            q_ref[rows, :], k_ref[rows, :] * K_SCALE, v_ref[rows, :],
            ig_col, lf_col, c_s[...], n_s[...], m_s[:, 0:1])
        c_s[...] = c_new
        n_s[...] = n_new
        m_s[...] = jnp.broadcast_to(m_new, m_s.shape)
        hm_ref[rows, :] = _head_out(hb, gain_ref[...], o_ref[rows, :])
        return carry

    lax.fori_loop(0, q_ref.shape[0] // MLSTM_CHUNK, body, 0)
    c_out_ref[...] = c_s[...]
    n_out_ref[...] = n_s[...]
    m_out_ref[...] = m_s[...]


def _mlstm_prompt(z, zt, gain, n_seq, seq_len):
    hpb = MLSTM_WIDTH // HEAD_DIM

    def sec(off):
        return pl.BlockSpec((seq_len, HEAD_DIM), lambda b, h, o=off // HEAD_DIM: (b, o + h))

    nbh = n_seq * HEADS
    return pl.pallas_call(
        _mlstm_prompt_kernel,
        grid=(n_seq, HEADS),
        in_specs=[
            sec(Z_Q), sec(Z_K), sec(Z_V), sec(Z_O),
            pl.BlockSpec((seq_len, 128), lambda b, h: (b, ZT_G // 128)),
            pl.BlockSpec((1, HEAD_DIM), lambda b, h: (0, h)),
        ],
        out_specs=[
            pl.BlockSpec((seq_len, HEAD_DIM), lambda b, h: (b, h)),
            pl.BlockSpec((None, HEAD_DIM, HEAD_DIM), lambda b, h: (b * hpb + h, 0, 0)),
            pl.BlockSpec((None, 1, HEAD_DIM), lambda b, h: (b * hpb + h, 0, 0)),
            pl.BlockSpec((None, 1, 128), lambda b, h: (b * hpb + h, 0, 0)),
        ],
        out_shape=[
            jax.ShapeDtypeStruct((n_seq * seq_len, MLSTM_WIDTH), F32),
            jax.ShapeDtypeStruct((nbh, HEAD_DIM, HEAD_DIM), F32),
            jax.ShapeDtypeStruct((nbh, 1, HEAD_DIM), F32),
            jax.ShapeDtypeStruct((nbh, 1, 128), F32),
        ],
        scratch_shapes=[
            pltpu.VMEM((HEAD_DIM, HEAD_DIM), F32),
            pltpu.VMEM((1, HEAD_DIM), F32),
            pltpu.VMEM((1, 128), F32),
        ],
        compiler_params=_cparams(("parallel", "parallel")),
        name="mlstm_prompt",
    )(z, z, z, z, zt, gain)


SAMPLE_SEQ = 4
SAMPLE_ROWS = 16


def _mlstm_sample_kernel(z_ref, g_ref, gain_ref, c_in_ref, n_in_ref, m_in_ref,
                         hm_ref, c_out_ref, n_out_ref, m_out_ref):
    for grp in range(SAMPLE_ROWS // 8):
        r0 = grp * 8
        gates = g_ref[r0:r0 + 8, :]
        row = lax.broadcasted_iota(I32, (8, 1), 0)
        for half in range(2):
            lo = half * SAMPLE_SEQ
            live = jnp.logical_and(row >= lo, row < lo + SAMPLE_SEQ)
            for h in range(HEADS):
                j = (grp * 2 + half) * HEADS + h
                cs = slice(h * HEAD_DIM, (h + 1) * HEAD_DIM)
                q = z_ref[r0:r0 + 8, Z_Q + h * HEAD_DIM:Z_Q + (h + 1) * HEAD_DIM]
                k = z_ref[r0:r0 + 8, Z_K + h * HEAD_DIM:Z_K + (h + 1) * HEAD_DIM]
                v = z_ref[r0:r0 + 8, Z_V + h * HEAD_DIM:Z_V + (h + 1) * HEAD_DIM]
                o = z_ref[r0:r0 + 8, Z_O + h * HEAD_DIM:Z_O + (h + 1) * HEAD_DIM]
                k = jnp.where(live, k * K_SCALE, 0.0)
                v = jnp.where(live, v, 0.0)
                ig_col = jnp.where(live, gates[:, h:h + 1], NEG_INF)
                lf_col = jnp.where(live, gates[:, HEADS + h:HEADS + h + 1], 0.0)
                hb, c_new, n_new, m_new = _mlstm_chunk(
                    q, k, v, ig_col, lf_col, c_in_ref[j], n_in_ref[j], m_in_ref[j][:, 0:1])
                out = _head_out(hb, gain_ref[:, cs], o)
                hm_ref[r0 + lo:r0 + lo + SAMPLE_SEQ, cs] = out[lo:lo + SAMPLE_SEQ, :]
                c_out_ref[j] = c_new
                n_out_ref[j] = n_new
                m_out_ref[j] = jnp.broadcast_to(m_new, (1, 128))


def _mlstm_sample(z, zt, gain, c0, n0, m0, row0, n_rows):
    nbh = c0.shape[0]
    per = SAMPLE_ROWS // SAMPLE_SEQ * HEADS
    rb0 = row0 // SAMPLE_ROWS
    return pl.pallas_call(
        _mlstm_sample_kernel,
        grid=(n_rows // SAMPLE_ROWS,),
        in_specs=[
            pl.BlockSpec((SAMPLE_ROWS, Z_U), lambda i: (rb0 + i, 0)),
            pl.BlockSpec((SAMPLE_ROWS, 128), lambda i: (rb0 + i, ZT_G // 128)),
            pl.BlockSpec((1, MLSTM_WIDTH), lambda i: (0, 0)),
            pl.BlockSpec((per, HEAD_DIM, HEAD_DIM), lambda i: (i, 0, 0)),
            pl.BlockSpec((per, 1, HEAD_DIM), lambda i: (i, 0, 0)),
            pl.BlockSpec((per, 1, 128), lambda i: (i, 0, 0)),
        ],
        out_specs=[
            pl.BlockSpec((SAMPLE_ROWS, MLSTM_WIDTH), lambda i: (i, 0)),
            pl.BlockSpec((per, HEAD_DIM, HEAD_DIM), lambda i: (i, 0, 0)),
            pl.BlockSpec((per, 1, HEAD_DIM), lambda i: (i, 0, 0)),
            pl.BlockSpec((per, 1, 128), lambda i: (i, 0, 0)),
        ],
        out_shape=[
            jax.ShapeDtypeStruct((n_rows, MLSTM_WIDTH), F32),
            jax.ShapeDtypeStruct((nbh, HEAD_DIM, HEAD_DIM), F32),
            jax.ShapeDtypeStruct((nbh, 1, HEAD_DIM), F32),
            jax.ShapeDtypeStruct((nbh, 1, 128), F32),
        ],
        compiler_params=_cparams(("parallel",)),
        name="mlstm_sample",
    )(z, zt, gain, c0, n0, m0)


def _s5_discretise(a_re, a_im, log_dt):
    dt = jnp.exp(log_dt)
    mag = jnp.exp(dt * a_re)
    ab_re = mag * jnp.cos(dt * a_im)
    ab_im = mag * jnp.sin(dt * a_im)
    den = a_re * a_re + a_im * a_im
    xr = ab_re - 1.0
    f_re = (xr * a_re + ab_im * a_im) / den
    f_im = (ab_im * a_re - xr * a_im) / den
    return ab_re, ab_im, f_re, f_im


def _s5_param_kernel(are_e, aim_e, ldt_e, b_re, b_im, are_r, aim_r, ldt_r,
                     bb_re_o, bb_im_o, ab_re_o, ab_im_o):
    _, _, f_re, f_im = _s5_discretise(are_e[...], aim_e[...], ldt_e[...])
    bb_re_o[...] = f_re * b_re[...] - f_im * b_im[...]
    bb_im_o[...] = f_re * b_im[...] + f_im * b_re[...]
    ab_re, ab_im, _, _ = _s5_discretise(are_r[...], aim_r[...], ldt_r[...])
    ab_re_o[...] = ab_re
    ab_im_o[...] = ab_im


def _s5_params(a_re, a_im, log_dt, b_re, b_im):
    g, p = a_re.shape
    c = b_re.shape[-1]
    rep = lambda t: jnp.repeat(t, c, axis=-1)
    ldt_e = jnp.broadcast_to(log_dt[:, None], (g, p * c))
    ldt_r = jnp.broadcast_to(log_dt[:, None], (g, p)).reshape(1, g * p)
    flat = jax.ShapeDtypeStruct((g, p * c), F32)
    rowv = jax.ShapeDtypeStruct((1, g * p), F32)
    return pl.pallas_call(
        _s5_param_kernel,
        out_shape=[flat, flat, rowv, rowv],
        name="s5_params",
    )(rep(a_re), rep(a_im), ldt_e, b_re.reshape(g, p * c), b_im.reshape(g, p * c),
      a_re.reshape(1, g * p), a_im.reshape(1, g * p), ldt_r)


def _blockdiag_in(bb_re, bb_im):
    bb = jnp.stack([bb_re, bb_im]).reshape(2, SSM_BLOCKS, GROUPS_PER_BLOCK, SSM_STATE, SSM_GROUP)
    t = jnp.transpose(bb, (1, 2, 4, 0, 3))
    eye = jnp.eye(GROUPS_PER_BLOCK, dtype=bool)
    w = jnp.where(eye[None, :, None, None, :, None], t[:, :, :, :, None, :], 0.0)
    return w.reshape(SSM_BLOCKS, BLOCK_CH, 2 * BLOCK_ST).astype(BF16)


def _blockdiag_out(cm):
    t = jnp.transpose(cm.reshape(SSM_BLOCKS, GROUPS_PER_BLOCK, SSM_GROUP, SSM_STATE), (0, 1, 3, 2))
    eye = jnp.eye(GROUPS_PER_BLOCK, dtype=bool)
    w = jnp.where(eye[None, :, None, :, None], t[:, :, :, None, :], 0.0)
    return w.reshape(SSM_BLOCKS, BLOCK_ST, BLOCK_CH).astype(BF16)


S5_TILE = 256
S5_SEQS = 4


def _s5_prompt_kernel(u0, u1, u2, u3, wb_ref, wcre_ref, wcim_ref, abre_ref, abim_ref, d_ref,
                      y_ref, sre_ref, sim_ref, u_tm, bu, y_tm, st):
    i = pl.program_id(1)
    half = BLOCK_CH // 2

    @pl.when(i == 0)
    def _():
        st[...] = jnp.zeros_like(st)

    for b, u in enumerate((u0, u1, u2, u3)):
        u_tm[0, pl.ds(b, S5_TILE, stride=S5_SEQS), :] = u[:, :half]
        u_tm[1, pl.ds(b, S5_TILE, stride=S5_SEQS), :] = u[:, half:]
    u_all = jnp.concatenate([u_tm[0], u_tm[1]], axis=1)
    bu[...] = _dot(u_all.astype(BF16), wb_ref[...])
    a_re = abre_ref[...]
    a_im = abim_ref[...]

    first = lax.broadcasted_iota(I32, (2 * S5_SEQS, BLOCK_ST), 0) < S5_SEQS

    def body(t2, carry):
        s_re, s_im = carry
        rows = pl.ds(pl.multiple_of(t2 * 2 * S5_SEQS, 2 * S5_SEQS), 2 * S5_SEQS)
        x_re = bu[rows, :BLOCK_ST]
        x_im = bu[rows, BLOCK_ST:]
        p_re = a_re * s_re - a_im * s_im + x_re
        p_im = a_re * s_im + a_im * s_re + x_im
        r_re = pltpu.roll(p_re, S5_SEQS, 0)
        r_im = pltpu.roll(p_im, S5_SEQS, 0)
        q_re = a_re * r_re - a_im * r_im + x_re
        q_im = a_re * r_im + a_im * r_re + x_im
        bu[rows, :BLOCK_ST] = jnp.where(first, p_re, q_re)
        bu[rows, BLOCK_ST:] = jnp.where(first, p_im, q_im)
        return pltpu.roll(q_re, S5_SEQS, 0), pltpu.roll(q_im, S5_SEQS, 0)

    s_re, s_im = lax.fori_loop(0, S5_TILE // 2, body, (st[:, :BLOCK_ST], st[:, BLOCK_ST:]), unroll=2)
    st[:, :BLOCK_ST] = s_re
    st[:, BLOCK_ST:] = s_im
    y = (_dot(bu[:, :BLOCK_ST].astype(BF16), wcre_ref[...])
         - _dot(bu[:, BLOCK_ST:].astype(BF16), wcim_ref[...]) + d_ref[...] * u_all)
    y = _gelu(y)
    y_tm[0] = y[:, :half]
    y_tm[1] = y[:, half:]
    for b in range(S5_SEQS):
        y_ref[b, :, :half] = y_tm[0, pl.ds(b, S5_TILE, stride=S5_SEQS), :]
        y_ref[b, :, half:] = y_tm[1, pl.ds(b, S5_TILE, stride=S5_SEQS), :]

    @pl.when(i == pl.num_programs(1) - 1)
    def _():
        sre_ref[...] = s_re[:S5_SEQS, :]
        sim_ref[...] = s_im[:S5_SEQS, :]


def _s5_prompt(z, wb, wcre, wcim, ab_re, ab_im, d_row, seq_len):
    nt = seq_len // S5_TILE

    def u_spec(b):
        return pl.BlockSpec((S5_TILE, BLOCK_CH), lambda j, i, b=b: (b * nt + i, j))

    rows = S5_TILE * S5_SEQS
    return pl.pallas_call(
        _s5_prompt_kernel,
        grid=(SSM_BLOCKS, nt),
        in_specs=[u_spec(b) for b in range(S5_SEQS)] + [
            pl.BlockSpec((None, BLOCK_CH, 2 * BLOCK_ST), lambda j, i: (j, 0, 0)),
            pl.BlockSpec((None, BLOCK_ST, BLOCK_CH), lambda j, i: (j, 0, 0)),
            pl.BlockSpec((None, BLOCK_ST, BLOCK_CH), lambda j, i: (j, 0, 0)),
            pl.BlockSpec((1, BLOCK_ST), lambda j, i: (0, j)),
            pl.BlockSpec((1, BLOCK_ST), lambda j, i: (0, j)),
            pl.BlockSpec((1, BLOCK_CH), lambda j, i: (0, j)),
        ],
        out_specs=[
            pl.BlockSpec((S5_SEQS, S5_TILE, BLOCK_CH), lambda j, i: (0, i, j)),
            pl.BlockSpec((S5_SEQS, BLOCK_ST), lambda j, i: (0, j)),
            pl.BlockSpec((S5_SEQS, BLOCK_ST), lambda j, i: (0, j)),
        ],
        out_shape=[
            jax.ShapeDtypeStruct((S5_SEQS, seq_len, SSM_WIDTH), F32),
            jax.ShapeDtypeStruct((S5_SEQS, SSM_GROUPS * SSM_STATE), F32),
            jax.ShapeDtypeStruct((S5_SEQS, SSM_GROUPS * SSM_STATE), F32),
        ],
        scratch_shapes=[
            pltpu.VMEM((2, rows, 128), F32),
            pltpu.VMEM((rows, 2 * BLOCK_ST), F32),
            pltpu.VMEM((2, rows, 128), F32),
            pltpu.VMEM((2 * S5_SEQS, 2 * BLOCK_ST), F32),
        ],
        compiler_params=_cparams(("parallel", "arbitrary")),
        name="s5_prompt",
    )(z, z, z, z, wb, wcre, wcim, ab_re, ab_im, d_row)


def _s5_sample_kernel(u_ref, s0re_ref, s0im_ref, wb_ref, wcre_ref, wcim_ref, abre_ref, abim_ref,
                      d_ref, y_ref, sre_ref, sim_ref, u_sl, y_sl):
    half = BLOCK_CH // 2
    n_seq = s0re_ref.shape[0]
    u_sl[0] = u_ref[:, :half]
    u_sl[1] = u_ref[:, half:]
    a_re = abre_ref[...]
    a_im = abim_ref[...]
    s_re = s0re_ref[...]
    s_im = s0im_ref[...]
    for t in range(SAMPLE_SEQ):
        rows = pl.ds(t, n_seq, stride=SAMPLE_SEQ)
        u_t = jnp.concatenate([u_sl[0, rows, :], u_sl[1, rows, :]], axis=1)
        bu = _dot(u_t.astype(BF16), wb_ref[...])
        n_re = a_re * s_re - a_im * s_im + bu[:, :BLOCK_ST]
        n_im = a_re * s_im + a_im * s_re + bu[:, BLOCK_ST:]
        s_re, s_im = n_re, n_im
        y = (_dot(s_re.astype(BF16), wcre_ref[...]) - _dot(s_im.astype(BF16), wcim_ref[...])
             + d_ref[...] * u_t)
        y = _gelu(y)
        y_sl[0, rows, :] = y[:, :half]
        y_sl[1, rows, :] = y[:, half:]
    y_ref[:, :half] = y_sl[0]
    y_ref[:, half:] = y_sl[1]
    sre_ref[...] = s_re
    sim_ref[...] = s_im


def _s5_sample(z, s0_re, s0_im, wb, wcre, wcim, ab_re, ab_im, d_row, row0, n_rows):
    n_seq = s0_re.shape[0]
    st_spec = pl.BlockSpec((n_seq, BLOCK_ST), lambda j: (0, j))
    return pl.pallas_call(
        _s5_sample_kernel,
        grid=(SSM_BLOCKS,),
        in_specs=[
            pl.BlockSpec((n_rows, BLOCK_CH), lambda j: (row0 // n_rows, j)),
            st_spec, st_spec,
            pl.BlockSpec((None, BLOCK_CH, 2 * BLOCK_ST), lambda j: (j, 0, 0)),
            pl.BlockSpec((None, BLOCK_ST, BLOCK_CH), lambda j: (j, 0, 0)),
            pl.BlockSpec((None, BLOCK_ST, BLOCK_CH), lambda j: (j, 0, 0)),
            pl.BlockSpec((1, BLOCK_ST), lambda j: (0, j)),
            pl.BlockSpec((1, BLOCK_ST), lambda j: (0, j)),
            pl.BlockSpec((1, BLOCK_CH), lambda j: (0, j)),
        ],
        out_specs=[pl.BlockSpec((n_rows, BLOCK_CH), lambda j: (0, j)), st_spec, st_spec],
        out_shape=[
            jax.ShapeDtypeStruct((n_rows, SSM_WIDTH), F32),
            jax.ShapeDtypeStruct(s0_re.shape, F32),
            jax.ShapeDtypeStruct(s0_im.shape, F32),
        ],
        scratch_shapes=[pltpu.VMEM((2, n_rows, 128), F32), pltpu.VMEM((2, n_rows, 128), F32)],
        compiler_params=_cparams(("parallel",)),
        name="s5_sample",
    )(z, s0_re, s0_im, wb, wcre, wcim, ab_re, ab_im, d_row)


def _postmix_kernel(hmp_ref, hms_ref, ysp_ref, yss_ref, xp_ref, xs_ref,
                    wglu_ref, bglu_ref, wout_ref, gain_ref, wq_ref, keys_ref,
                    h1_ref, c_ref, sp_ref, ss_ref, *, prompt_tiles):
    i = pl.program_id(0)
    ys = _group_pick(ysp_ref, yss_ref, prompt_tiles)
    hm = _group_pick(hmp_ref, hms_ref, prompt_tiles)
    glu = ys * _sigmoid(_dot(ys.astype(BF16), wglu_ref[...]) + bglu_ref[...])
    mix = (_dot(hm.astype(BF16), wout_ref[:MLSTM_WIDTH, :])
           + _dot(glu.astype(BF16), wout_ref[MLSTM_WIDTH:, :]))
    h1 = _group_pick(xp_ref, xs_ref, prompt_tiles) + mix
    h1_ref[...] = h1
    c = _rmsnorm(h1, gain_ref[...]).astype(BF16)
    c_ref[...] = c
    qp = _dot(c, wq_ref[...])
    scores = [_dot_nt(keys_ref[j], qp[:, j * PEER_HALF:(j + 1) * PEER_HALF].astype(BF16))
              for j in range(2 * PEER_HEADS)]

    @pl.when(i < prompt_tiles)
    def _():
        for j, s in enumerate(scores):
            sp_ref[j] = s

    @pl.when(i >= prompt_tiles)
    def _():
        for j, s in enumerate(scores):
            ss_ref[j] = s


def _postmix(hm_p, hm_s, ys_p, ys_s, x_p, x_s, wglu, bglu, wout, gain, wq, keys, tm=256):
    n_p, n_s = x_p.shape[0], x_s.shape[0]
    t = n_p + n_s
    pt = n_p // tm
    nk = 2 * PEER_HEADS
    return pl.pallas_call(
        functools.partial(_postmix_kernel, prompt_tiles=pt),
        grid=(t // tm,),
        in_specs=_group_specs(tm, MLSTM_WIDTH, pt) + _group_specs(tm, SSM_WIDTH, pt)
        + _group_specs(tm, D_MODEL, pt) + [
            _resident((SSM_WIDTH, SSM_WIDTH), lambda i: (0, 0)),
            _resident((1, SSM_WIDTH), lambda i: (0, 0)),
            _resident((D_MODEL, D_MODEL), lambda i: (0, 0)),
            _resident((1, D_MODEL), lambda i: (0, 0)),
            _resident((D_MODEL, PEER_HEADS * 2 * PEER_HALF), lambda i: (0, 0)),
            _resident((nk, PEER_KEYS, PEER_HALF), lambda i: (0, 0, 0)),
        ],
        out_specs=[
            pl.BlockSpec((tm, D_MODEL), lambda i: (i, 0)),
            pl.BlockSpec((tm, D_MODEL), lambda i: (i, 0)),
            pl.BlockSpec((nk, PEER_KEYS, tm), lambda i: (0, 0, jnp.minimum(i, pt - 1))),
            pl.BlockSpec((nk, PEER_KEYS, tm), lambda i: (0, 0, jnp.maximum(i - pt, 0))),
        ],
        out_shape=[
            jax.ShapeDtypeStruct((t, D_MODEL), F32),
            jax.ShapeDtypeStruct((t, D_MODEL), BF16),
            jax.ShapeDtypeStruct((nk, PEER_KEYS, n_p), F32),
            jax.ShapeDtypeStruct((nk, PEER_KEYS, n_s), F32),
        ],
        compiler_params=_cparams(("arbitrary",)),
        name="postmix",
    )(hm_p, hm_s, ys_p, ys_s, x_p, x_s, wglu, bglu, wout, gain, wq, keys)


ID_NONE = 1 << 20
TOPK_SUB = 8


def _sort_network(n):
    pairs = []
    p = 1
    while p < n:
        k = p
        while k >= 1:
            for j in range(k % p, n - k, 2 * k):
                for i in range(min(k, n - j - k)):
                    if (i + j) // (2 * p) == (i + j + k) // (2 * p):
                        pairs.append((i + j, i + j + k))
            k //= 2
        p *= 2
    return pairs


_SORT16 = _sort_network(PEER_TOPK)


def _precedes(b, a):
    (vb, ib), (va, ia) = b, a
    return jnp.logical_or(vb > va, jnp.logical_and(vb == va, ib < ia))


def _first_of(a, b):
    sw = _precedes(b, a)
    return jnp.where(sw, b[0], a[0]), jnp.where(sw, b[1], a[1])


def _exchange(items, i, j):
    a, b = items[i], items[j]
    sw = _precedes(b, a)
    items[i] = (jnp.where(sw, b[0], a[0]), jnp.where(sw, b[1], a[1]))
    items[j] = (jnp.where(sw, a[0], b[0]), jnp.where(sw, a[1], b[1]))


def _sort16(items):
    items = list(items)
    for i, j in _SORT16:
        _exchange(items, i, j)
    return items


def _bitonic_merge16(items):
    items = list(items)
    d = PEER_TOPK // 2
    while d >= 1:
        for i in range(PEER_TOPK):
            if i & d == 0:
                _exchange(items, i, i + d)
        d //= 2
    return items


def _merge_top16(a, b):
    return _bitonic_merge16([_first_of(a[i], b[PEER_TOPK - 1 - i]) for i in range(PEER_TOPK)])


def _top16_of_keys(s_ref, half, shape):
    best = None
    for g in range(PEER_KEYS // PEER_TOPK):
        grp = _sort16([(s_ref[half, g * PEER_TOPK + k], jnp.full(shape, g * PEER_TOPK + k, I32))
                       for k in range(PEER_TOPK)])
        best = grp if best is None else _merge_top16(best, grp)
    return best


def _topk_kernel(s_ref, e1_ref, e2_ref, g_ref):
    shape = s_ref.shape[2:]
    top1 = _top16_of_keys(s_ref, 0, shape)
    top2 = _top16_of_keys(s_ref, 1, shape)

    def pair(i, j):
        return top1[i][0] + top2[j][0], jnp.full(shape, i * PEER_TOPK + j, I32)

    pad = (jnp.full(shape, NEG_INF, F32), jnp.full(shape, ID_NONE, I32))
    g0 = [pair(0, j) for j in range(16)]
    g1 = _bitonic_merge16([pair(1, j) for j in range(8)] + [pair(i, 0) for i in range(15, 7, -1)])
    g2 = _sort16([pair(i, j) for i in range(2, 7) for j in range(PEER_TOPK // (i + 1))])
    g3 = [pair(7, 0), pair(7, 1)] + [pad] * 14
    best = _merge_top16(_merge_top16(g0, g1), _merge_top16(g2, g3))

    mx = best[0][0]
    exps = []
    for k in range(PEER_TOPK):
        v, pid = best[k]
        a = lax.shift_right_logical(pid, 4)
        b = jnp.bitwise_and(pid, PEER_TOPK - 1)
        e1 = jnp.zeros(shape, I32)
        e2 = jnp.zeros(shape, I32)
        for r in range(PEER_TOPK):
            e1 = jnp.where(a == r, top1[r][1], e1)
            e2 = jnp.where(b == r, top2[r][1], e2)
        e1_ref[k] = e1
        e2_ref[k] = e2
        exps.append(jnp.exp(v - mx))
    total = exps[0]
    for k in range(1, PEER_TOPK):
        total = total + exps[k]
    for k in range(PEER_TOPK):
        g_ref[k] = exps[k] / total


def _topk(scores):
    t = scores.shape[-1]
    ng = t // 128
    sub = min(ng, TOPK_SUB)
    r = PEER_HEADS * PEER_TOPK
    o_spec = pl.BlockSpec((PEER_TOPK, sub, 128), lambda i, h: (h, i, 0))
    outs = pl.pallas_call(
        _topk_kernel,
        grid=(ng // sub, PEER_HEADS),
        in_specs=[pl.BlockSpec((2, PEER_KEYS, sub, 128), lambda i, h: (h, 0, i, 0))],
        out_specs=[o_spec, o_spec, o_spec],
        out_shape=[
            jax.ShapeDtypeStruct((r, ng, 128), I32),
            jax.ShapeDtypeStruct((r, ng, 128), I32),
            jax.ShapeDtypeStruct((r, ng, 128), F32),
        ],
        compiler_params=_cparams(("parallel", "parallel")),
        name="topk",
    )(scores.reshape(2 * PEER_HEADS, PEER_KEYS, ng, 128))
    return [o.reshape(r, t) for o in outs]


WB_GROUP = 16
WB_PITCH = 132


def _wbuild_kernel(e1_ref, e2_ref, g_ref, w_ref, stage):
    sub = lax.broadcasted_iota(I32, (PEER_KEYS, PEER_KEYS), 0)

    def group(gi, carry):
        t0 = pl.multiple_of(gi * WB_GROUP, WB_GROUP)
        for tt in range(WB_GROUP):
            row = pl.ds(t0 + tt, 1)
            onehot1 = jnp.where(sub == e1_ref[row, :], 1.0, 0.0).astype(BF16)
            gated2 = jnp.where(sub == e2_ref[row, :], g_ref[row, :], 0.0).astype(BF16)
            stage[tt * WB_PITCH:tt * WB_PITCH + PEER_KEYS, :] = _dot_nt(onehot1, gated2)
        for e in range(PEER_KEYS):
            blk = stage[pl.ds(e, WB_GROUP, stride=WB_PITCH), :]
            w_ref[e, pl.ds(t0, WB_GROUP), :] = blk.astype(BF16)
        return carry

    lax.fori_loop(0, e1_ref.shape[0] // WB_GROUP, group, 0)


def _wbuild(e1, e2, g, tw=128):
    t = e1.shape[0]
    i_spec = pl.BlockSpec((tw, PEER_KEYS), lambda i: (i, 0))
    return pl.pallas_call(
        _wbuild_kernel,
        grid=(t // tw,),
        in_specs=[i_spec, i_spec, i_spec],
        out_specs=pl.BlockSpec((PEER_KEYS, tw, PEER_KEYS), lambda i: (0, i, 0)),
        out_shape=jax.ShapeDtypeStruct((PEER_KEYS, t, PEER_KEYS), BF16),
        scratch_shapes=[pltpu.VMEM((WB_GROUP * WB_PITCH, PEER_KEYS), F32)],
        compiler_params=_cparams(("parallel",)),
        name="wbuild",
    )(e1, e2, g)


PEER_EB = 512


def _peer_kernel(c_ref, u_ref, v_ref, w_ref, o_ref, s_s):
    @pl.when(pl.program_id(1) == 0)
    def _():
        o_ref[...] = jnp.zeros_like(o_ref)
        s_s[...] = jnp.zeros_like(s_s)

    parts = [_gelu(s_s[:, k * PEER_KEYS:(k + 1) * PEER_KEYS]) * w_ref[k].astype(F32)
             for k in range(PEER_EB // PEER_KEYS)]
    wact = jnp.concatenate(parts, axis=1).astype(BF16)
    s_s[...] = _dot_nt(c_ref[...], u_ref[...].astype(BF16))
    o_ref[...] += _dot(wact, v_ref[...].astype(BF16))


def _peer(c, u, v, w3, n_tiles=8):
    t = c.shape[0]
    tm = t // n_tiles
    nb = PEER_EXPERTS // PEER_EB
    return pl.pallas_call(
        _peer_kernel,
        grid=(n_tiles, nb + 1),
        in_specs=[
            pl.BlockSpec((tm, D_MODEL), lambda i, j: (i, 0), pipeline_mode=pl.Buffered(1)),
            pl.BlockSpec((PEER_EB, D_MODEL), lambda i, j: (jnp.minimum(j, nb - 1), 0)),
            pl.BlockSpec((PEER_EB, D_MODEL), lambda i, j: (jnp.maximum(j - 1, 0), 0)),
            pl.BlockSpec((PEER_EB // PEER_KEYS, tm, PEER_KEYS),
                         lambda i, j: (jnp.maximum(j - 1, 0), i, 0)),
        ],
        out_specs=pl.BlockSpec((tm, D_MODEL), lambda i, j: (i, 0)),
        out_shape=jax.ShapeDtypeStruct((t, D_MODEL), F32),
        scratch_shapes=[pltpu.VMEM((tm, PEER_EB), F32)],
        compiler_params=_cparams(("parallel", "arbitrary")),
        name="peer",
    )(c, u, v, w3)


def _tail_kernel(h1_ref, peer_ref, pp_ref, ps_ref, gple_ref, wgate_ref, wproj_ref, gfin_ref,
                 yp_ref, ys_ref, *, prompt_tiles):
    i = pl.program_id(0)
    h2 = h1_ref[...] + peer_ref[...]
    gate = _sigmoid(_dot(_rmsnorm(h2, gple_ref[...]).astype(BF16), wgate_ref[...]))
    e = _dot(_group_pick(pp_ref, ps_ref, prompt_tiles).astype(BF16), wproj_ref[...])
    y = _rmsnorm(h2 + e * gate, gfin_ref[...])

    @pl.when(i < prompt_tiles)
    def _():
        yp_ref[...] = y

    @pl.when(i >= prompt_tiles)
    def _():
        ys_ref[...] = y


def _tail(h1, peer, p_p, p_s, gple, wgate, wproj, gfin, tm=256):
    t = h1.shape[0]
    n_prompt = p_p.shape[0]
    pt = n_prompt // tm
    ple = p_p.shape[1]
    return pl.pallas_call(
        functools.partial(_tail_kernel, prompt_tiles=pt),
        grid=(t // tm,),
        in_specs=[
            pl.BlockSpec((tm, D_MODEL), lambda i: (i, 0)),
            pl.BlockSpec((tm, D_MODEL), lambda i: (i, 0)),
        ] + _group_specs(tm, ple, pt) + [
            _resident((1, D_MODEL), lambda i: (0, 0)),
            _resident((D_MODEL, D_MODEL), lambda i: (0, 0)),
            _resident((ple, D_MODEL), lambda i: (0, 0)),
            _resident((1, D_MODEL), lambda i: (0, 0)),
        ],
        out_specs=[
            pl.BlockSpec((tm, D_MODEL), lambda i: (jnp.minimum(i, pt - 1), 0)),
            pl.BlockSpec((tm, D_MODEL), lambda i: (jnp.maximum(i - pt, 0), 0)),
        ],
        out_shape=[
            jax.ShapeDtypeStruct((n_prompt, D_MODEL), F32),
            jax.ShapeDtypeStruct((t - n_prompt, D_MODEL), F32),
        ],
        compiler_params=_cparams(("arbitrary",)),
        name="tail",
    )(h1, peer, p_p, p_s, gple, wgate, wproj, gfin)


def kernel(x_prompt, x_sample, state_mlstm_C, state_mlstm_n, state_mlstm_m, state_ssm_re, state_ssm_im, p_prompt, p_sample, norm_mix, w_in, b_igate, b_fgate, mlstm_norm, ssm_A_re, ssm_A_im, ssm_B_re, ssm_B_im, ssm_C_re, ssm_C_im, ssm_D, ssm_log_dt, w_glu, b_glu, w_out, norm_ffn, peer_w_q, peer_keys, peer_u, peer_v, norm_ple, w_ple_gate, w_ple_proj, norm_final):
    n_pseq, p_len, _ = x_prompt.shape
    n_sseq, s_len, _ = x_sample.shape
    assert s_len == SAMPLE_SEQ and n_pseq == S5_SEQS and w_in.shape[0] == 1
    n_prompt = n_pseq * p_len
    n_sample = n_sseq * s_len
    row = lambda t: t.reshape(1, -1)

    x_p = x_prompt.reshape(n_prompt, D_MODEL)
    x_s = x_sample.reshape(n_sample, D_MODEL)

    w_bf = w_in[0].astype(BF16)
    n_gate = 2 * HEADS
    w_tail = jnp.concatenate(
        [w_bf[:, Z_U + n_gate:], w_bf[:, Z_U:Z_U + n_gate],
         jnp.zeros((D_MODEL, 128 - n_gate), BF16)], axis=1)
    gbias = jnp.concatenate([b_igate[0], b_fgate[0], jnp.zeros((128 - n_gate,), F32)]).reshape(1, 128)
    z, zt = _in_proj(x_p, x_s, row(norm_mix[0]), w_bf, w_tail, gbias)

    gain_m = row(mlstm_norm[0])
    hm_p, c_p, n_p, m_p = _mlstm_prompt(z, zt, gain_m, n_pseq, p_len)
    nbh = n_sseq * HEADS
    hm_s, c_s, n_s, m_s = _mlstm_sample(
        z, zt, gain_m,
        state_mlstm_C[0].reshape(nbh, HEAD_DIM, HEAD_DIM),
        state_mlstm_n[0].reshape(nbh, 1, HEAD_DIM),
        jnp.broadcast_to(state_mlstm_m[0].reshape(nbh, 1, 1), (nbh, 1, 128)),
        n_prompt, n_sample)

    bb_re, bb_im, ab_re, ab_im = _s5_params(ssm_A_re[0], ssm_A_im[0], ssm_log_dt[0], ssm_B_re[0], ssm_B_im[0])
    wb = _blockdiag_in(bb_re, bb_im)
    wcre = _blockdiag_out(ssm_C_re[0])
    wcim = _blockdiag_out(ssm_C_im[0])
    d_row = row(ssm_D[0])
    ys_p, sre_p, sim_p = _s5_prompt(zt, wb, wcre, wcim, ab_re, ab_im, d_row, p_len)
    n_st = SSM_GROUPS * SSM_STATE
    ys_s, sre_s, sim_s = _s5_sample(
        zt, state_ssm_re[0].reshape(n_sseq, n_st), state_ssm_im[0].reshape(n_sseq, n_st),
        wb, wcre, wcim, ab_re, ab_im, d_row, n_prompt, n_sample)

    keys = peer_keys[0].reshape(2 * PEER_HEADS, PEER_KEYS, PEER_HALF).astype(BF16)
    h1, c, scores_p, scores_s = _postmix(
        hm_p, hm_s, ys_p.reshape(n_prompt, SSM_WIDTH), ys_s, x_p, x_s,
        w_glu[0].astype(BF16), row(b_glu[0]), w_out[0].astype(BF16),
        row(norm_ffn[0]), peer_w_q[0].astype(BF16), keys)
    routing = [jnp.concatenate([rp, rs], axis=1).T for rp, rs in zip(_topk(scores_p), _topk(scores_s))]
    w3 = _wbuild(*routing)
    peer = _peer(c, peer_u[0], peer_v[0], w3)

    y_p, y_s = _tail(h1, peer, p_prompt[0].reshape(n_prompt, -1), p_sample[0].reshape(n_sample, -1),
                     row(norm_ple[0]), w_ple_gate[0].astype(BF16), w_ple_proj[0].astype(BF16),
                     row(norm_final))

    st_shape = (1, -1, SSM_GROUPS, SSM_STATE)
    return (
        y_p.reshape(x_prompt.shape), y_s.reshape(x_sample.shape),
        c_p.reshape(1, n_pseq, HEADS, HEAD_DIM, HEAD_DIM), n_p.reshape(1, n_pseq, HEADS, HEAD_DIM),
        m_p[:, 0, 0].reshape(1, n_pseq, HEADS),
        sre_p.reshape(st_shape), sim_p.reshape(st_shape),
        c_s.reshape(1, n_sseq, HEADS, HEAD_DIM, HEAD_DIM), n_s.reshape(1, n_sseq, HEADS, HEAD_DIM),
        m_s[:, 0, 0].reshape(1, n_sseq, HEADS),
        sre_s.reshape(st_shape), sim_s.reshape(st_shape),
    )
```

```python
import functools

import jax
import jax.numpy as jnp
from jax import lax
from jax.experimental import pallas as pl
from jax.experimental.pallas import tpu as pltpu

F32 = jnp.float32
BF16 = jnp.bfloat16
I32 = jnp.int32

EPS = 1e-6
D_MODEL = 2048
HEADS = 4
HEAD_DIM = 256
MLSTM_WIDTH = HEADS * HEAD_DIM
SSM_WIDTH = 1024
SSM_GROUPS = 64
SSM_STATE = 64
SSM_GROUP = 16
GROUPS_PER_BLOCK = 16
SSM_BLOCKS = SSM_GROUPS // GROUPS_PER_BLOCK
BLOCK_CH = GROUPS_PER_BLOCK * SSM_GROUP
BLOCK_ST = GROUPS_PER_BLOCK * SSM_STATE
PEER_HEADS = 8
PEER_KEYS = 128
PEER_TOPK = 16
PEER_HALF = 64
PEER_EXPERTS = PEER_KEYS * PEER_KEYS

Z_Q, Z_K, Z_V, Z_O, Z_U = 0, 1024, 2048, 3072, 4096
ZT_G = 1024
ZT_COLS = ZT_G + 128
IN_PROJ_COLS = 2048

MLSTM_CHUNK = 256
K_SCALE = HEAD_DIM ** -0.5
NEG_INF = float("-inf")

VMEM_LIMIT = 56 * 1024 * 1024


def _cparams(semantics):
    return pltpu.CompilerParams(dimension_semantics=semantics, vmem_limit_bytes=VMEM_LIMIT)


def _resident(shape, index_map):
    return pl.BlockSpec(shape, index_map, pipeline_mode=pl.Buffered(1))


def _sigmoid(x):
    return 1.0 / (1.0 + jnp.exp(-x))


def _gelu(x):
    return x * (0.5 * (1.0 + jnp.tanh(0.7978845608028654 * (x + 0.044715 * (x * x * x)))))


def _rmsnorm(x, gain):
    return x * lax.rsqrt(jnp.mean(x * x, axis=-1, keepdims=True) + EPS) * gain


def _dot(a, b):
    return jnp.dot(a, b, preferred_element_type=F32)


def _dot_nt(a, b):
    return lax.dot_general(a, b, (((1,), (1,)), ((), ())), preferred_element_type=F32)


def _dot_tn(a, b):
    return lax.dot_general(a, b, (((0,), (0,)), ((), ())), preferred_element_type=F32)


def _group_specs(tm, width, prompt_tiles):
    return [
        pl.BlockSpec((tm, width), lambda i: (jnp.minimum(i, prompt_tiles - 1), 0)),
        pl.BlockSpec((tm, width), lambda i: (jnp.maximum(i - prompt_tiles, 0), 0)),
    ]


def _group_pick(prompt_ref, sample_ref, prompt_tiles):
    return jnp.where(pl.program_id(0) < prompt_tiles, prompt_ref[...], sample_ref[...])


def _in_proj_kernel(xp_ref, xs_ref, gain_ref, wm_ref, wt_ref, gbias_ref, zm_ref, zt_ref, a_s,
                    *, prompt_tiles, main_steps):
    j = pl.program_id(1)

    @pl.when(j == 0)
    def _():
        a_s[...] = _rmsnorm(_group_pick(xp_ref, xs_ref, prompt_tiles), gain_ref[...]).astype(BF16)

    for step in range(main_steps):
        @pl.when(j == step)
        def _(base=step * IN_PROJ_COLS):
            for c0 in range(0, IN_PROJ_COLS, 512):
                zm_ref[:, c0:c0 + 512] = _dot(a_s[...], wm_ref[:, base + c0:base + c0 + 512])

    @pl.when(j == main_steps)
    def _():
        a = a_s[...]
        for c0 in range(0, ZT_G, 512):
            zt_ref[:, c0:c0 + 512] = _dot(a, wt_ref[:, c0:c0 + 512])
        gz = _dot(a, wt_ref[:, ZT_G:]) + gbias_ref[...]
        lane = lax.broadcasted_iota(I32, gz.shape, 1)
        log_f = jnp.minimum(gz, 0.0) - jnp.log(1.0 + jnp.exp(-jnp.abs(gz)))
        zt_ref[:, ZT_G:] = jnp.where(lane < HEADS, gz, log_f)


def _in_proj(x_p, x_s, gain, w_bf, w_tail, gbias, tm=512):
    t = x_p.shape[0] + x_s.shape[0]
    pt = x_p.shape[0] // tm
    ms = Z_U // IN_PROJ_COLS
    return pl.pallas_call(
        functools.partial(_in_proj_kernel, prompt_tiles=pt, main_steps=ms),
        grid=(t // tm, ms + 1),
        in_specs=[
            pl.BlockSpec((tm, D_MODEL), lambda i, j: (jnp.minimum(i, pt - 1), 0)),
            pl.BlockSpec((tm, D_MODEL), lambda i, j: (jnp.maximum(i - pt, 0), 0),
                         pipeline_mode=pl.Buffered(1)),
            _resident((1, D_MODEL), lambda i, j: (0, 0)),
            _resident((D_MODEL, Z_U), lambda i, j: (0, 0)),
            _resident((D_MODEL, ZT_COLS), lambda i, j: (0, 0)),
            _resident((1, 128), lambda i, j: (0, 0)),
        ],
        out_specs=[
            pl.BlockSpec((tm, IN_PROJ_COLS), lambda i, j: (i, jnp.minimum(j, ms - 1))),
            pl.BlockSpec((tm, ZT_COLS), lambda i, j: (i, 0)),
        ],
        out_shape=[
            jax.ShapeDtypeStruct((t, Z_U), F32),
            jax.ShapeDtypeStruct((t, ZT_COLS), F32),
        ],
        scratch_shapes=[pltpu.VMEM((tm, D_MODEL), BF16)],
        compiler_params=_cparams(("arbitrary", "arbitrary")),
        name="in_proj",
    )(x_p, x_s, gain, w_bf, w_tail, gbias)


def _mlstm_chunk(q, k, v, ig_col, lf_col, c_state, n_state, m_state):
    L = q.shape[0]
    row = lax.broadcasted_iota(I32, (L, L), 0)
    col = lax.broadcasted_iota(I32, (L, L), 1)
    diag = row == col
    causal = col <= row
    lf_row = jnp.sum(jnp.where(diag, lf_col, 0.0), axis=0, keepdims=True)
    ig_row = jnp.sum(jnp.where(diag, ig_col, 0.0), axis=0, keepdims=True)
    b_col = jnp.sum(jnp.where(causal, lf_row, 0.0), axis=1, keepdims=True)
    b_row = jnp.sum(jnp.where(row <= col, lf_col, 0.0), axis=0, keepdims=True)
    r_row = ig_row - b_row
    r_col = ig_col - b_col
    run_max = jnp.max(jnp.where(causal, r_row, NEG_INF), axis=1, keepdims=True)
    m_run = jnp.maximum(m_state, run_max)
    w_intra = jnp.exp(jnp.where(causal, r_row - m_run, NEG_INF))
    w_inter = jnp.exp(m_state - m_run)

    qb = q.astype(BF16)
    vb = v.astype(BF16)
    s = _dot_nt(qb, k.astype(BF16)) * w_intra
    num = w_inter * _dot(qb, c_state.astype(BF16)) + _dot(s.astype(BF16), vb)
    qn = jnp.sum(q * n_state, axis=1, keepdims=True)
    den = w_inter * qn + jnp.sum(s, axis=1, keepdims=True)
    hb = num / jnp.maximum(jnp.abs(den), jnp.exp(-(b_col + m_run)))

    m_last = m_run[L - 1:L, :]
    m_new = b_col[L - 1:L, :] + m_last
    kw = k * jnp.exp(r_col - m_last)
    sc = jnp.exp(m_state - m_last)
    c_new = sc * c_state + _dot_tn(kw.astype(BF16), vb)
    n_new = sc * n_state + jnp.sum(kw, axis=0, keepdims=True)
    return hb, c_new, n_new, m_new


def _head_out(hb, gain, o_pre):
    return hb * lax.rsqrt(jnp.mean(hb * hb, axis=-1, keepdims=True) + EPS) * gain * _sigmoid(o_pre)


def _mlstm_prompt_kernel(q_ref, k_ref, v_ref, o_ref, g_ref, gain_ref,
                         hm_ref, c_out_ref, n_out_ref, m_out_ref, c_s, n_s, m_s):
    h0 = pl.program_id(1) * PROMPT_HEADS
    c_s[...] = jnp.zeros_like(c_s)
    n_s[...] = jnp.zeros_like(n_s)
    m_s[...] = jnp.zeros_like(m_s)

    def body(ci, carry):
        rows = pl.ds(pl.multiple_of(ci * MLSTM_CHUNK, MLSTM_CHUNK), MLSTM_CHUNK)
        gates = g_ref[rows, :]
        lane = lax.broadcasted_iota(I32, gates.shape, 1)
        for hh in range(PROMPT_HEADS):
            cs = slice(hh * HEAD_DIM, (hh + 1) * HEAD_DIM)
            ig_col = jnp.sum(jnp.where(lane == h0 + hh, gates, 0.0), axis=1, keepdims=True)
            lf_col = jnp.sum(jnp.where(lane == h0 + hh + HEADS, gates, 0.0), axis=1, keepdims=True)
            hb, c_new, n_new, m_new = _mlstm_chunk(
                q_ref[rows, cs], k_ref[rows, cs] * K_SCALE, v_ref[rows, cs],
                ig_col, lf_col, c_s[hh], n_s[hh], m_s[hh][:, 0:1])
            c_s[hh] = c_new
            n_s[hh] = n_new
            m_s[hh] = jnp.broadcast_to(m_new, (1, 128))
            hm_ref[rows, cs] = _head_out(hb, gain_ref[:, cs], o_ref[rows, cs])
        return carry

    lax.fori_loop(0, q_ref.shape[0] // MLSTM_CHUNK, body, 0)
    c_out_ref[...] = c_s[...]
    n_out_ref[...] = n_s[...]
    m_out_ref[...] = m_s[...]


PROMPT_HEADS = 2


def _mlstm_prompt(z, zt, gain, n_seq, seq_len):
    width = PROMPT_HEADS * HEAD_DIM
    hsteps = HEADS // PROMPT_HEADS

    def sec(off):
        return pl.BlockSpec((seq_len, width), lambda b, h, o=off // width: (b, o + h))

    def state(*tail):
        return pl.BlockSpec((PROMPT_HEADS,) + tail, lambda b, h: (b * hsteps + h,) + (0,) * len(tail))

    nbh = n_seq * HEADS
    return pl.pallas_call(
        _mlstm_prompt_kernel,
        grid=(n_seq, hsteps),
        in_specs=[
            sec(Z_Q), sec(Z_K), sec(Z_V), sec(Z_O),
            pl.BlockSpec((seq_len, 128), lambda b, h: (b, ZT_G // 128)),
            pl.BlockSpec((1, width), lambda b, h: (0, h)),
        ],
        out_specs=[
            pl.BlockSpec((seq_len, width), lambda b, h: (b, h)),
            state(HEAD_DIM, HEAD_DIM), state(1, HEAD_DIM), state(1, 128),
        ],
        out_shape=[
            jax.ShapeDtypeStruct((n_seq * seq_len, MLSTM_WIDTH), F32),
            jax.ShapeDtypeStruct((nbh, HEAD_DIM, HEAD_DIM), F32),
            jax.ShapeDtypeStruct((nbh, 1, HEAD_DIM), F32),
            jax.ShapeDtypeStruct((nbh, 1, 128), F32),
        ],
        scratch_shapes=[
            pltpu.VMEM((PROMPT_HEADS, HEAD_DIM, HEAD_DIM), F32),
            pltpu.VMEM((PROMPT_HEADS, 1, HEAD_DIM), F32),
            pltpu.VMEM((PROMPT_HEADS, 1, 128), F32),
        ],
        compiler_params=_cparams(("parallel", "parallel")),
        name="mlstm_prompt",
    )(z, z, z, z, zt, gain)


SAMPLE_SEQ = 4
SAMPLE_ROWS = 16


def _mlstm_sample_kernel(z_ref, g_ref, gain_ref, c_in_ref, n_in_ref, m_in_ref,
                         hm_ref, c_out_ref, n_out_ref, m_out_ref):
    for grp in range(SAMPLE_ROWS // 8):
        r0 = grp * 8
        gates = g_ref[r0:r0 + 8, :]
        row = lax.broadcasted_iota(I32, (8, 1), 0)
        for half in range(2):
            lo = half * SAMPLE_SEQ
            live = jnp.logical_and(row >= lo, row < lo + SAMPLE_SEQ)
            for h in range(HEADS):
                j = (grp * 2 + half) * HEADS + h
                cs = slice(h * HEAD_DIM, (h + 1) * HEAD_DIM)
                q = z_ref[r0:r0 + 8, Z_Q + h * HEAD_DIM:Z_Q + (h + 1) * HEAD_DIM]
                k = z_ref[r0:r0 + 8, Z_K + h * HEAD_DIM:Z_K + (h + 1) * HEAD_DIM]
                v = z_ref[r0:r0 + 8, Z_V + h * HEAD_DIM:Z_V + (h + 1) * HEAD_DIM]
                o = z_ref[r0:r0 + 8, Z_O + h * HEAD_DIM:Z_O + (h + 1) * HEAD_DIM]
                k = jnp.where(live, k * K_SCALE, 0.0)
                v = jnp.where(live, v, 0.0)
                ig_col = jnp.where(live, gates[:, h:h + 1], NEG_INF)
                lf_col = jnp.where(live, gates[:, HEADS + h:HEADS + h + 1], 0.0)
                hb, c_new, n_new, m_new = _mlstm_chunk(
                    q, k, v, ig_col, lf_col, c_in_ref[j], n_in_ref[j], m_in_ref[j][:, 0:1])
                out = _head_out(hb, gain_ref[:, cs], o)
                hm_ref[r0 + lo:r0 + lo + SAMPLE_SEQ, cs] = out[lo:lo + SAMPLE_SEQ, :]
                c_out_ref[j] = c_new
                n_out_ref[j] = n_new
                m_out_ref[j] = jnp.broadcast_to(m_new, (1, 128))


def _mlstm_sample(z, zt, gain, c0, n0, m0, row0, n_rows):
    nbh = c0.shape[0]
    per = SAMPLE_ROWS // SAMPLE_SEQ * HEADS
    rb0 = row0 // SAMPLE_ROWS
    return pl.pallas_call(
        _mlstm_sample_kernel,
        grid=(n_rows // SAMPLE_ROWS,),
        in_specs=[
            pl.BlockSpec((SAMPLE_ROWS, Z_U), lambda i: (rb0 + i, 0)),
            pl.BlockSpec((SAMPLE_ROWS, 128), lambda i: (rb0 + i, ZT_G // 128)),
            pl.BlockSpec((1, MLSTM_WIDTH), lambda i: (0, 0)),
            pl.BlockSpec((per, HEAD_DIM, HEAD_DIM), lambda i: (i, 0, 0)),
            pl.BlockSpec((per, 1, HEAD_DIM), lambda i: (i, 0, 0)),
            pl.BlockSpec((per, 1, 128), lambda i: (i, 0, 0)),
        ],
        out_specs=[
            pl.BlockSpec((SAMPLE_ROWS, MLSTM_WIDTH), lambda i: (i, 0)),
            pl.BlockSpec((per, HEAD_DIM, HEAD_DIM), lambda i: (i, 0, 0)),
            pl.BlockSpec((per, 1, HEAD_DIM), lambda i: (i, 0, 0)),
            pl.BlockSpec((per, 1, 128), lambda i: (i, 0, 0)),
        ],
        out_shape=[
            jax.ShapeDtypeStruct((n_rows, MLSTM_WIDTH), F32),
            jax.ShapeDtypeStruct((nbh, HEAD_DIM, HEAD_DIM), F32),
            jax.ShapeDtypeStruct((nbh, 1, HEAD_DIM), F32),
            jax.ShapeDtypeStruct((nbh, 1, 128), F32),
        ],
        compiler_params=_cparams(("parallel",)),
        name="mlstm_sample",
    )(z, zt, gain, c0, n0, m0)


def _s5_discretise(a_re, a_im, log_dt):
    dt = jnp.exp(log_dt)
    mag = jnp.exp(dt * a_re)
    ab_re = mag * jnp.cos(dt * a_im)
    ab_im = mag * jnp.sin(dt * a_im)
    den = a_re * a_re + a_im * a_im
    xr = ab_re - 1.0
    f_re = (xr * a_re + ab_im * a_im) / den
    f_im = (ab_im * a_re - xr * a_im) / den
    return ab_re, ab_im, f_re, f_im


def _s5_param_kernel(are_e, aim_e, ldt_e, b_re, b_im, are_r, aim_r, ldt_r,
                     bb_re_o, bb_im_o, ab_re_o, ab_im_o):
    _, _, f_re, f_im = _s5_discretise(are_e[...], aim_e[...], ldt_e[...])
    bb_re_o[...] = f_re * b_re[...] - f_im * b_im[...]
    bb_im_o[...] = f_re * b_im[...] + f_im * b_re[...]
    ab_re, ab_im, _, _ = _s5_discretise(are_r[...], aim_r[...], ldt_r[...])
    ab_re_o[...] = ab_re
    ab_im_o[...] = ab_im


def _s5_params(a_re, a_im, log_dt, b_re, b_im):
    g, p = a_re.shape
    c = b_re.shape[-1]
    rep = lambda t: jnp.repeat(t, c, axis=-1)
    ldt_e = jnp.broadcast_to(log_dt[:, None], (g, p * c))
    ldt_r = jnp.broadcast_to(log_dt[:, None], (g, p)).reshape(1, g * p)
    flat = jax.ShapeDtypeStruct((g, p * c), F32)
    rowv = jax.ShapeDtypeStruct((1, g * p), F32)
    return pl.pallas_call(
        _s5_param_kernel,
        out_shape=[flat, flat, rowv, rowv],
        name="s5_params",
    )(rep(a_re), rep(a_im), ldt_e, b_re.reshape(g, p * c), b_im.reshape(g, p * c),
      a_re.reshape(1, g * p), a_im.reshape(1, g * p), ldt_r)


def _blockdiag_in(bb_re, bb_im):
    bb = jnp.stack([bb_re, bb_im]).reshape(2, SSM_BLOCKS, GROUPS_PER_BLOCK, SSM_STATE, SSM_GROUP)
    t = jnp.transpose(bb, (1, 2, 4, 0, 3))
    eye = jnp.eye(GROUPS_PER_BLOCK, dtype=bool)
    w = jnp.where(eye[None, :, None, None, :, None], t[:, :, :, :, None, :], 0.0)
    return w.reshape(SSM_BLOCKS, BLOCK_CH, 2 * BLOCK_ST).astype(BF16)


def _blockdiag_out(cm):
    t = jnp.transpose(cm.reshape(SSM_BLOCKS, GROUPS_PER_BLOCK, SSM_GROUP, SSM_STATE), (0, 1, 3, 2))
    eye = jnp.eye(GROUPS_PER_BLOCK, dtype=bool)
    w = jnp.where(eye[None, :, None, :, None], t[:, :, :, None, :], 0.0)
    return w.reshape(SSM_BLOCKS, BLOCK_ST, BLOCK_CH).astype(BF16)


S5_TILE = 256
S5_SEQS = 4


def _s5_prompt_kernel(u0, u1, u2, u3, wb_ref, wcre_ref, wcim_ref, abre_ref, abim_ref, d_ref,
                      y_ref, sre_ref, sim_ref, u_tm, bu, y_tm, st):
    i = pl.program_id(1)
    half = BLOCK_CH // 2

    @pl.when(i == 0)
    def _():
        st[...] = jnp.zeros_like(st)

    for b, u in enumerate((u0, u1, u2, u3)):
        u_tm[0, pl.ds(b, S5_TILE, stride=S5_SEQS), :] = u[:, :half]
        u_tm[1, pl.ds(b, S5_TILE, stride=S5_SEQS), :] = u[:, half:]
    u_all = jnp.concatenate([u_tm[0], u_tm[1]], axis=1)
    bu[...] = _dot(u_all.astype(BF16), wb_ref[...])
    a_re = abre_ref[...]
    a_im = abim_ref[...]

    first = lax.broadcasted_iota(I32, (2 * S5_SEQS, BLOCK_ST), 0) < S5_SEQS

    def body(t2, carry):
        s_re, s_im = carry
        rows = pl.ds(pl.multiple_of(t2 * 2 * S5_SEQS, 2 * S5_SEQS), 2 * S5_SEQS)
        x_re = bu[rows, :BLOCK_ST]
        x_im = bu[rows, BLOCK_ST:]
        p_re = a_re * s_re - a_im * s_im + x_re
        p_im = a_re * s_im + a_im * s_re + x_im
        r_re = pltpu.roll(p_re, S5_SEQS, 0)
        r_im = pltpu.roll(p_im, S5_SEQS, 0)
        q_re = a_re * r_re - a_im * r_im + x_re
        q_im = a_re * r_im + a_im * r_re + x_im
        bu[rows, :BLOCK_ST] = jnp.where(first, p_re, q_re)
        bu[rows, BLOCK_ST:] = jnp.where(first, p_im, q_im)
        return pltpu.roll(q_re, S5_SEQS, 0), pltpu.roll(q_im, S5_SEQS, 0)

    s_re, s_im = lax.fori_loop(0, S5_TILE // 2, body, (st[:, :BLOCK_ST], st[:, BLOCK_ST:]), unroll=2)
    st[:, :BLOCK_ST] = s_re
    st[:, BLOCK_ST:] = s_im
    y = (_dot(bu[:, :BLOCK_ST].astype(BF16), wcre_ref[...])
         - _dot(bu[:, BLOCK_ST:].astype(BF16), wcim_ref[...]) + d_ref[...] * u_all)
    y = _gelu(y)
    y_tm[0] = y[:, :half]
    y_tm[1] = y[:, half:]
    for b in range(S5_SEQS):
        y_ref[b, :, :half] = y_tm[0, pl.ds(b, S5_TILE, stride=S5_SEQS), :]
        y_ref[b, :, half:] = y_tm[1, pl.ds(b, S5_TILE, stride=S5_SEQS), :]

    @pl.when(i == pl.num_programs(1) - 1)
    def _():
        sre_ref[...] = s_re[:S5_SEQS, :]
        sim_ref[...] = s_im[:S5_SEQS, :]


def _s5_prompt(z, wb, wcre, wcim, ab_re, ab_im, d_row, seq_len):
    nt = seq_len // S5_TILE

    def u_spec(b):
        return pl.BlockSpec((S5_TILE, BLOCK_CH), lambda j, i, b=b: (b * nt + i, j))

    rows = S5_TILE * S5_SEQS
    return pl.pallas_call(
        _s5_prompt_kernel,
        grid=(SSM_BLOCKS, nt),
        in_specs=[u_spec(b) for b in range(S5_SEQS)] + [
            pl.BlockSpec((None, BLOCK_CH, 2 * BLOCK_ST), lambda j, i: (j, 0, 0)),
            pl.BlockSpec((None, BLOCK_ST, BLOCK_CH), lambda j, i: (j, 0, 0)),
            pl.BlockSpec((None, BLOCK_ST, BLOCK_CH), lambda j, i: (j, 0, 0)),
            pl.BlockSpec((1, BLOCK_ST), lambda j, i: (0, j)),
            pl.BlockSpec((1, BLOCK_ST), lambda j, i: (0, j)),
            pl.BlockSpec((1, BLOCK_CH), lambda j, i: (0, j)),
        ],
        out_specs=[
            pl.BlockSpec((S5_SEQS, S5_TILE, BLOCK_CH), lambda j, i: (0, i, j)),
            pl.BlockSpec((S5_SEQS, BLOCK_ST), lambda j, i: (0, j)),
            pl.BlockSpec((S5_SEQS, BLOCK_ST), lambda j, i: (0, j)),
        ],
        out_shape=[
            jax.ShapeDtypeStruct((S5_SEQS, seq_len, SSM_WIDTH), F32),
            jax.ShapeDtypeStruct((S5_SEQS, SSM_GROUPS * SSM_STATE), F32),
            jax.ShapeDtypeStruct((S5_SEQS, SSM_GROUPS * SSM_STATE), F32),
        ],
        scratch_shapes=[
            pltpu.VMEM((2, rows, 128), F32),
            pltpu.VMEM((rows, 2 * BLOCK_ST), F32),
            pltpu.VMEM((2, rows, 128), F32),
            pltpu.VMEM((2 * S5_SEQS, 2 * BLOCK_ST), F32),
        ],
        compiler_params=_cparams(("parallel", "arbitrary")),
        name="s5_prompt",
    )(z, z, z, z, wb, wcre, wcim, ab_re, ab_im, d_row)


def _s5_sample_kernel(u_ref, s0re_ref, s0im_ref, wb_ref, wcre_ref, wcim_ref, abre_ref, abim_ref,
                      d_ref, y_ref, sre_ref, sim_ref, u_sl, y_sl):
    half = BLOCK_CH // 2
    n_seq = s0re_ref.shape[0]
    u_sl[0] = u_ref[:, :half]
    u_sl[1] = u_ref[:, half:]
    a_re = abre_ref[...]
    a_im = abim_ref[...]
    s_re = s0re_ref[...]
    s_im = s0im_ref[...]
    for t in range(SAMPLE_SEQ):
        rows = pl.ds(t, n_seq, stride=SAMPLE_SEQ)
        u_t = jnp.concatenate([u_sl[0, rows, :], u_sl[1, rows, :]], axis=1)
        bu = _dot(u_t.astype(BF16), wb_ref[...])
        n_re = a_re * s_re - a_im * s_im + bu[:, :BLOCK_ST]
        n_im = a_re * s_im + a_im * s_re + bu[:, BLOCK_ST:]
        s_re, s_im = n_re, n_im
        y = (_dot(s_re.astype(BF16), wcre_ref[...]) - _dot(s_im.astype(BF16), wcim_ref[...])
             + d_ref[...] * u_t)
        y = _gelu(y)
        y_sl[0, rows, :] = y[:, :half]
        y_sl[1, rows, :] = y[:, half:]
    y_ref[:, :half] = y_sl[0]
    y_ref[:, half:] = y_sl[1]
    sre_ref[...] = s_re
    sim_ref[...] = s_im


def _s5_sample(z, s0_re, s0_im, wb, wcre, wcim, ab_re, ab_im, d_row, row0, n_rows):
    n_seq = s0_re.shape[0]
    st_spec = pl.BlockSpec((n_seq, BLOCK_ST), lambda j: (0, j))
    return pl.pallas_call(
        _s5_sample_kernel,
        grid=(SSM_BLOCKS,),
        in_specs=[
            pl.BlockSpec((n_rows, BLOCK_CH), lambda j: (row0 // n_rows, j)),
            st_spec, st_spec,
            pl.BlockSpec((None, BLOCK_CH, 2 * BLOCK_ST), lambda j: (j, 0, 0)),
            pl.BlockSpec((None, BLOCK_ST, BLOCK_CH), lambda j: (j, 0, 0)),
            pl.BlockSpec((None, BLOCK_ST, BLOCK_CH), lambda j: (j, 0, 0)),
            pl.BlockSpec((1, BLOCK_ST), lambda j: (0, j)),
            pl.BlockSpec((1, BLOCK_ST), lambda j: (0, j)),
            pl.BlockSpec((1, BLOCK_CH), lambda j: (0, j)),
        ],
        out_specs=[pl.BlockSpec((n_rows, BLOCK_CH), lambda j: (0, j)), st_spec, st_spec],
        out_shape=[
            jax.ShapeDtypeStruct((n_rows, SSM_WIDTH), F32),
            jax.ShapeDtypeStruct(s0_re.shape, F32),
            jax.ShapeDtypeStruct(s0_im.shape, F32),
        ],
        scratch_shapes=[pltpu.VMEM((2, n_rows, 128), F32), pltpu.VMEM((2, n_rows, 128), F32)],
        compiler_params=_cparams(("parallel",)),
        name="s5_sample",
    )(z, s0_re, s0_im, wb, wcre, wcim, ab_re, ab_im, d_row)


def _postmix_kernel(hmp_ref, hms_ref, ysp_ref, yss_ref, xp_ref, xs_ref,
                    wglu_ref, bglu_ref, wout_ref, gain_ref, wq_ref, keys_ref,
                    h1_ref, c_ref, sp_ref, ss_ref, *, prompt_tiles):
    i = pl.program_id(0)
    ys = _group_pick(ysp_ref, yss_ref, prompt_tiles)
    hm = _group_pick(hmp_ref, hms_ref, prompt_tiles)
    glu = ys * _sigmoid(_dot(ys.astype(BF16), wglu_ref[...]) + bglu_ref[...])
    mix = (_dot(hm.astype(BF16), wout_ref[:MLSTM_WIDTH, :])
           + _dot(glu.astype(BF16), wout_ref[MLSTM_WIDTH:, :]))
    h1 = _group_pick(xp_ref, xs_ref, prompt_tiles) + mix
    h1_ref[...] = h1
    c = _rmsnorm(h1, gain_ref[...]).astype(BF16)
    c_ref[...] = c
    qp = _dot(c, wq_ref[...])
    scores = [_dot_nt(keys_ref[j], qp[:, j * PEER_HALF:(j + 1) * PEER_HALF].astype(BF16))
              for j in range(2 * PEER_HEADS)]

    @pl.when(i < prompt_tiles)
    def _():
        for j, s in enumerate(scores):
            sp_ref[j] = s

    @pl.when(i >= prompt_tiles)
    def _():
        for j, s in enumerate(scores):
            ss_ref[j] = s


def _postmix(hm_p, hm_s, ys_p, ys_s, x_p, x_s, wglu, bglu, wout, gain, wq, keys, tm=256):
    n_p, n_s = x_p.shape[0], x_s.shape[0]
    t = n_p + n_s
    pt = n_p // tm
    nk = 2 * PEER_HEADS
    return pl.pallas_call(
        functools.partial(_postmix_kernel, prompt_tiles=pt),
        grid=(t // tm,),
        in_specs=_group_specs(tm, MLSTM_WIDTH, pt) + _group_specs(tm, SSM_WIDTH, pt)
        + _group_specs(tm, D_MODEL, pt) + [
            _resident((SSM_WIDTH, SSM_WIDTH), lambda i: (0, 0)),
            _resident((1, SSM_WIDTH), lambda i: (0, 0)),
            _resident((D_MODEL, D_MODEL), lambda i: (0, 0)),
            _resident((1, D_MODEL), lambda i: (0, 0)),
            _resident((D_MODEL, PEER_HEADS * 2 * PEER_HALF), lambda i: (0, 0)),
            _resident((nk, PEER_KEYS, PEER_HALF), lambda i: (0, 0, 0)),
        ],
        out_specs=[
            pl.BlockSpec((tm, D_MODEL), lambda i: (i, 0)),
            pl.BlockSpec((tm, D_MODEL), lambda i: (i, 0)),
            pl.BlockSpec((nk, PEER_KEYS, tm), lambda i: (0, 0, jnp.minimum(i, pt - 1))),
            pl.BlockSpec((nk, PEER_KEYS, tm), lambda i: (0, 0, jnp.maximum(i - pt, 0))),
        ],
        out_shape=[
            jax.ShapeDtypeStruct((t, D_MODEL), F32),
            jax.ShapeDtypeStruct((t, D_MODEL), BF16),
            jax.ShapeDtypeStruct((nk, PEER_KEYS, n_p), F32),
            jax.ShapeDtypeStruct((nk, PEER_KEYS, n_s), F32),
        ],
        compiler_params=_cparams(("arbitrary",)),
        name="postmix",
    )(hm_p, hm_s, ys_p, ys_s, x_p, x_s, wglu, bglu, wout, gain, wq, keys)


ID_NONE = 1 << 20
TOPK_SUB = 8


def _sort_network(n):
    pairs = []
    p = 1
    while p < n:
        k = p
        while k >= 1:
            for j in range(k % p, n - k, 2 * k):
                for i in range(min(k, n - j - k)):
                    if (i + j) // (2 * p) == (i + j + k) // (2 * p):
                        pairs.append((i + j, i + j + k))
            k //= 2
        p *= 2
    return pairs


_SORT16 = _sort_network(PEER_TOPK)


def _precedes(b, a):
    (vb, ib), (va, ia) = b, a
    return jnp.logical_or(vb > va, jnp.logical_and(vb == va, ib < ia))


def _first_of(a, b):
    sw = _precedes(b, a)
    return jnp.where(sw, b[0], a[0]), jnp.where(sw, b[1], a[1])


def _exchange(items, i, j):
    a, b = items[i], items[j]
    sw = _precedes(b, a)
    items[i] = (jnp.where(sw, b[0], a[0]), jnp.where(sw, b[1], a[1]))
    items[j] = (jnp.where(sw, a[0], b[0]), jnp.where(sw, a[1], b[1]))


def _sort16(items):
    items = list(items)
    for i, j in _SORT16:
        _exchange(items, i, j)
    return items


def _bitonic_merge16(items):
    items = list(items)
    d = PEER_TOPK // 2
    while d >= 1:
        for i in range(PEER_TOPK):
            if i & d == 0:
                _exchange(items, i, i + d)
        d //= 2
    return items


def _merge_top16(a, b):
    return _bitonic_merge16([_first_of(a[i], b[PEER_TOPK - 1 - i]) for i in range(PEER_TOPK)])


def _top16_of_keys(s_ref, half, shape):
    best = None
    for g in range(PEER_KEYS // PEER_TOPK):
        grp = _sort16([(s_ref[half, g * PEER_TOPK + k], jnp.full(shape, g * PEER_TOPK + k, I32))
                       for k in range(PEER_TOPK)])
        best = grp if best is None else _merge_top16(best, grp)
    return best


def _topk_kernel(s_ref, e1_ref, e2_ref, g_ref):
    shape = s_ref.shape[2:]
    top1 = _top16_of_keys(s_ref, 0, shape)
    top2 = _top16_of_keys(s_ref, 1, shape)

    def pair(i, j):
        return top1[i][0] + top2[j][0], jnp.full(shape, i * PEER_TOPK + j, I32)

    pad = (jnp.full(shape, NEG_INF, F32), jnp.full(shape, ID_NONE, I32))
    g0 = [pair(0, j) for j in range(16)]
    g1 = _bitonic_merge16([pair(1, j) for j in range(8)] + [pair(i, 0) for i in range(15, 7, -1)])
    g2 = _sort16([pair(i, j) for i in range(2, 7) for j in range(PEER_TOPK // (i + 1))])
    g3 = [pair(7, 0), pair(7, 1)] + [pad] * 14
    best = _merge_top16(_merge_top16(g0, g1), _merge_top16(g2, g3))

    mx = best[0][0]
    exps = []
    for k in range(PEER_TOPK):
        v, pid = best[k]
        a = lax.shift_right_logical(pid, 4)
        b = jnp.bitwise_and(pid, PEER_TOPK - 1)
        e1 = jnp.zeros(shape, I32)
        e2 = jnp.zeros(shape, I32)
        for r in range(PEER_TOPK):
            e1 = jnp.where(a == r, top1[r][1], e1)
            e2 = jnp.where(b == r, top2[r][1], e2)
        e1_ref[k] = e1
        e2_ref[k] = e2
        exps.append(jnp.exp(v - mx))
    total = exps[0]
    for k in range(1, PEER_TOPK):
        total = total + exps[k]
    for k in range(PEER_TOPK):
        g_ref[k] = exps[k] / total


def _topk(scores):
    t = scores.shape[-1]
    ng = t // 128
    sub = min(ng, TOPK_SUB)
    r = PEER_HEADS * PEER_TOPK
    o_spec = pl.BlockSpec((PEER_TOPK, sub, 128), lambda i, h: (h, i, 0))
    outs = pl.pallas_call(
        _topk_kernel,
        grid=(ng // sub, PEER_HEADS),
        in_specs=[pl.BlockSpec((2, PEER_KEYS, sub, 128), lambda i, h: (h, 0, i, 0))],
        out_specs=[o_spec, o_spec, o_spec],
        out_shape=[
            jax.ShapeDtypeStruct((r, ng, 128), I32),
            jax.ShapeDtypeStruct((r, ng, 128), I32),
            jax.ShapeDtypeStruct((r, ng, 128), F32),
        ],
        compiler_params=_cparams(("parallel", "parallel")),
        name="topk",
    )(scores.reshape(2 * PEER_HEADS, PEER_KEYS, ng, 128))
    return [o.reshape(r, t) for o in outs]


WB_GROUP = 16
WB_PITCH = 132


def _wbuild_kernel(e1_ref, e2_ref, g_ref, w_ref, stage):
    sub = lax.broadcasted_iota(I32, (PEER_KEYS, PEER_KEYS), 0)

    def group(gi, carry):
        t0 = pl.multiple_of(gi * WB_GROUP, WB_GROUP)
        for tt in range(WB_GROUP):
            row = pl.ds(t0 + tt, 1)
            onehot1 = jnp.where(sub == e1_ref[row, :], 1.0, 0.0).astype(BF16)
            gated2 = jnp.where(sub == e2_ref[row, :], g_ref[row, :], 0.0).astype(BF16)
            stage[tt * WB_PITCH:tt * WB_PITCH + PEER_KEYS, :] = _dot_nt(onehot1, gated2)
        for e in range(PEER_KEYS):
            blk = stage[pl.ds(e, WB_GROUP, stride=WB_PITCH), :]
            w_ref[e, pl.ds(t0, WB_GROUP), :] = blk.astype(BF16)
        return carry

    lax.fori_loop(0, e1_ref.shape[0] // WB_GROUP, group, 0)


def _wbuild(e1, e2, g, tw=128):
    t = e1.shape[0]
    i_spec = pl.BlockSpec((tw, PEER_KEYS), lambda i: (i, 0))
    return pl.pallas_call(
        _wbuild_kernel,
        grid=(t // tw,),
        in_specs=[i_spec, i_spec, i_spec],
        out_specs=pl.BlockSpec((PEER_KEYS, tw, PEER_KEYS), lambda i: (0, i, 0)),
        out_shape=jax.ShapeDtypeStruct((PEER_KEYS, t, PEER_KEYS), BF16),
        scratch_shapes=[pltpu.VMEM((WB_GROUP * WB_PITCH, PEER_KEYS), F32)],
        compiler_params=_cparams(("parallel",)),
        name="wbuild",
    )(e1, e2, g)


PEER_EB = 512


def _peer_kernel(c_ref, u_ref, v_ref, w_ref, o_ref, s_s):
    j = pl.program_id(1)
    last = pl.num_programs(1) - 1

    def gated():
        parts = [_gelu(s_s[:, k * PEER_KEYS:(k + 1) * PEER_KEYS]) * w_ref[k].astype(F32)
                 for k in range(PEER_EB // PEER_KEYS)]
        return jnp.concatenate(parts, axis=1).astype(BF16)

    def scores():
        return _dot_nt(c_ref[...], u_ref[...].astype(BF16))

    @pl.when(j == 0)
    def _():
        o_ref[...] = jnp.zeros_like(o_ref)
        s_s[...] = scores()

    @pl.when(jnp.logical_and(j > 0, j < last))
    def _():
        wact = gated()
        s_s[...] = scores()
        o_ref[...] += _dot(wact, v_ref[...].astype(BF16))

    @pl.when(j == last)
    def _():
        o_ref[...] += _dot(gated(), v_ref[...].astype(BF16))


def _peer(c, u, v, w3, n_tiles=8):
    t = c.shape[0]
    tm = t // n_tiles
    nb = PEER_EXPERTS // PEER_EB
    return pl.pallas_call(
        _peer_kernel,
        grid=(n_tiles, nb + 1),
        in_specs=[
            pl.BlockSpec((tm, D_MODEL), lambda i, j: (i, 0), pipeline_mode=pl.Buffered(1)),
            pl.BlockSpec((PEER_EB, D_MODEL), lambda i, j: (jnp.minimum(j, nb - 1), 0)),
            pl.BlockSpec((PEER_EB, D_MODEL), lambda i, j: (jnp.maximum(j - 1, 0), 0)),
            pl.BlockSpec((PEER_EB // PEER_KEYS, tm, PEER_KEYS),
                         lambda i, j: (jnp.maximum(j - 1, 0), i, 0)),
        ],
        out_specs=pl.BlockSpec((tm, D_MODEL), lambda i, j: (i, 0)),
        out_shape=jax.ShapeDtypeStruct((t, D_MODEL), F32),
        scratch_shapes=[pltpu.VMEM((tm, PEER_EB), F32)],
        compiler_params=_cparams(("parallel", "arbitrary")),
        name="peer",
    )(c, u, v, w3)


def _tail_kernel(h1_ref, peer_ref, pp_ref, ps_ref, gple_ref, wgate_ref, wproj_ref, gfin_ref,
                 yp_ref, ys_ref, *, prompt_tiles):
    i = pl.program_id(0)
    h2 = h1_ref[...] + peer_ref[...]
    gate = _sigmoid(_dot(_rmsnorm(h2, gple_ref[...]).astype(BF16), wgate_ref[...]))
    e = _dot(_group_pick(pp_ref, ps_ref, prompt_tiles).astype(BF16), wproj_ref[...])
    y = _rmsnorm(h2 + e * gate, gfin_ref[...])

    @pl.when(i < prompt_tiles)
    def _():
        yp_ref[...] = y

    @pl.when(i >= prompt_tiles)
    def _():
        ys_ref[...] = y


def _tail(h1, peer, p_p, p_s, gple, wgate, wproj, gfin, tm=256):
    t = h1.shape[0]
    n_prompt = p_p.shape[0]
    pt = n_prompt // tm
    ple = p_p.shape[1]
    return pl.pallas_call(
        functools.partial(_tail_kernel, prompt_tiles=pt),
        grid=(t // tm,),
        in_specs=[
            pl.BlockSpec((tm, D_MODEL), lambda i: (i, 0)),
            pl.BlockSpec((tm, D_MODEL), lambda i: (i, 0)),
        ] + _group_specs(tm, ple, pt) + [
            _resident((1, D_MODEL), lambda i: (0, 0)),
            _resident((D_MODEL, D_MODEL), lambda i: (0, 0)),
            _resident((ple, D_MODEL), lambda i: (0, 0)),
            _resident((1, D_MODEL), lambda i: (0, 0)),
        ],
        out_specs=[
            pl.BlockSpec((tm, D_MODEL), lambda i: (jnp.minimum(i, pt - 1), 0)),
            pl.BlockSpec((tm, D_MODEL), lambda i: (jnp.maximum(i - pt, 0), 0)),
        ],
        out_shape=[
            jax.ShapeDtypeStruct((n_prompt, D_MODEL), F32),
            jax.ShapeDtypeStruct((t - n_prompt, D_MODEL), F32),
        ],
        compiler_params=_cparams(("arbitrary",)),
        name="tail",
    )(h1, peer, p_p, p_s, gple, wgate, wproj, gfin)


def kernel(x_prompt, x_sample, state_mlstm_C, state_mlstm_n, state_mlstm_m, state_ssm_re, state_ssm_im, p_prompt, p_sample, norm_mix, w_in, b_igate, b_fgate, mlstm_norm, ssm_A_re, ssm_A_im, ssm_B_re, ssm_B_im, ssm_C_re, ssm_C_im, ssm_D, ssm_log_dt, w_glu, b_glu, w_out, norm_ffn, peer_w_q, peer_keys, peer_u, peer_v, norm_ple, w_ple_gate, w_ple_proj, norm_final):
    n_pseq, p_len, _ = x_prompt.shape
    n_sseq, s_len, _ = x_sample.shape
    assert s_len == SAMPLE_SEQ and n_pseq == S5_SEQS and w_in.shape[0] == 1
    n_prompt = n_pseq * p_len
    n_sample = n_sseq * s_len
    row = lambda t: t.reshape(1, -1)

    x_p = x_prompt.reshape(n_prompt, D_MODEL)
    x_s = x_sample.reshape(n_sample, D_MODEL)

    w_bf = w_in[0].astype(BF16)
    n_gate = 2 * HEADS
    w_tail = jnp.concatenate(
        [w_bf[:, Z_U + n_gate:], w_bf[:, Z_U:Z_U + n_gate],
         jnp.zeros((D_MODEL, 128 - n_gate), BF16)], axis=1)
    gbias = jnp.concatenate([b_igate[0], b_fgate[0], jnp.zeros((128 - n_gate,), F32)]).reshape(1, 128)
    z, zt = _in_proj(x_p, x_s, row(norm_mix[0]), w_bf, w_tail, gbias)

    gain_m = row(mlstm_norm[0])
    hm_p, c_p, n_p, m_p = _mlstm_prompt(z, zt, gain_m, n_pseq, p_len)
    nbh = n_sseq * HEADS
    hm_s, c_s, n_s, m_s = _mlstm_sample(
        z, zt, gain_m,
        state_mlstm_C[0].reshape(nbh, HEAD_DIM, HEAD_DIM),
        state_mlstm_n[0].reshape(nbh, 1, HEAD_DIM),
        jnp.broadcast_to(state_mlstm_m[0].reshape(nbh, 1, 1), (nbh, 1, 128)),
        n_prompt, n_sample)

    bb_re, bb_im, ab_re, ab_im = _s5_params(ssm_A_re[0], ssm_A_im[0], ssm_log_dt[0], ssm_B_re[0], ssm_B_im[0])
    wb = _blockdiag_in(bb_re, bb_im)
    wcre = _blockdiag_out(ssm_C_re[0])
    wcim = _blockdiag_out(ssm_C_im[0])
    d_row = row(ssm_D[0])
    ys_p, sre_p, sim_p = _s5_prompt(zt, wb, wcre, wcim, ab_re, ab_im, d_row, p_len)
    n_st = SSM_GROUPS * SSM_STATE
    ys_s, sre_s, sim_s = _s5_sample(
        zt, state_ssm_re[0].reshape(n_sseq, n_st), state_ssm_im[0].reshape(n_sseq, n_st),
        wb, wcre, wcim, ab_re, ab_im, d_row, n_prompt, n_sample)

    keys = peer_keys[0].reshape(2 * PEER_HEADS, PEER_KEYS, PEER_HALF).astype(BF16)
    h1, c, scores_p, scores_s = _postmix(
        hm_p, hm_s, ys_p.reshape(n_prompt, SSM_WIDTH), ys_s, x_p, x_s,
        w_glu[0].astype(BF16), row(b_glu[0]), w_out[0].astype(BF16),
        row(norm_ffn[0]), peer_w_q[0].astype(BF16), keys)
    routing = [jnp.concatenate([rp, rs], axis=1).T for rp, rs in zip(_topk(scores_p), _topk(scores_s))]
    w3 = _wbuild(*routing)
    peer = _peer(c, peer_u[0], peer_v[0], w3)

    y_p, y_s = _tail(h1, peer, p_prompt[0].reshape(n_prompt, -1), p_sample[0].reshape(n_sample, -1),
                     row(norm_ple[0]), w_ple_gate[0].astype(BF16), w_ple_proj[0].astype(BF16),
                     row(norm_final))

    st_shape = (1, -1, SSM_GROUPS, SSM_STATE)
    return (
        y_p.reshape(x_prompt.shape), y_s.reshape(x_sample.shape),
        c_p.reshape(1, n_pseq, HEADS, HEAD_DIM, HEAD_DIM), n_p.reshape(1, n_pseq, HEADS, HEAD_DIM),
        m_p[:, 0, 0].reshape(1, n_pseq, HEADS),
        sre_p.reshape(st_shape), sim_p.reshape(st_shape),
        c_s.reshape(1, n_sseq, HEADS, HEAD_DIM, HEAD_DIM), n_s.reshape(1, n_sseq, HEADS, HEAD_DIM),
        m_s[:, 0, 0].reshape(1, n_sseq, HEADS),
        sre_s.reshape(st_shape), sim_s.reshape(st_shape),
    )
```

```python
import functools

import jax
import jax.numpy as jnp
from jax import lax
from jax.experimental import pallas as pl
from jax.experimental.pallas import tpu as pltpu

F32 = jnp.float32
BF16 = jnp.bfloat16
I32 = jnp.int32

EPS = 1e-6
D_MODEL = 2048
HEADS = 4
HEAD_DIM = 256
MLSTM_WIDTH = HEADS * HEAD_DIM
SSM_WIDTH = 1024
SSM_GROUPS = 64
SSM_STATE = 64
SSM_GROUP = 16
GROUPS_PER_BLOCK = 16
SSM_BLOCKS = SSM_GROUPS // GROUPS_PER_BLOCK
BLOCK_CH = GROUPS_PER_BLOCK * SSM_GROUP
BLOCK_ST = GROUPS_PER_BLOCK * SSM_STATE
PEER_HEADS = 8
PEER_KEYS = 128
PEER_TOPK = 16
PEER_HALF = 64
PEER_EXPERTS = PEER_KEYS * PEER_KEYS

Z_Q, Z_K, Z_V, Z_O, Z_U = 0, 1024, 2048, 3072, 4096
ZT_G = 1024
ZT_COLS = ZT_G + 128
IN_PROJ_COLS = 2048

MLSTM_CHUNK = 256
K_SCALE = HEAD_DIM ** -0.5
NEG_INF = float("-inf")

VMEM_LIMIT = 56 * 1024 * 1024


def _cparams(semantics):
    return pltpu.CompilerParams(dimension_semantics=semantics, vmem_limit_bytes=VMEM_LIMIT)


def _resident(shape, index_map):
    return pl.BlockSpec(shape, index_map, pipeline_mode=pl.Buffered(1))


def _sigmoid(x):
    return 1.0 / (1.0 + jnp.exp(-x))


def _gelu(x):
    return x * (0.5 * (1.0 + jnp.tanh(0.7978845608028654 * (x + 0.044715 * (x * x * x)))))


def _rmsnorm(x, gain):
    return x * lax.rsqrt(jnp.mean(x * x, axis=-1, keepdims=True) + EPS) * gain


def _dot(a, b):
    return jnp.dot(a, b, preferred_element_type=F32)


def _dot_nt(a, b):
    return lax.dot_general(a, b, (((1,), (1,)), ((), ())), preferred_element_type=F32)


def _dot_tn(a, b):
    return lax.dot_general(a, b, (((0,), (0,)), ((), ())), preferred_element_type=F32)


def _group_specs(tm, width, prompt_tiles):
    return [
        pl.BlockSpec((tm, width), lambda i: (jnp.minimum(i, prompt_tiles - 1), 0)),
        pl.BlockSpec((tm, width), lambda i: (jnp.maximum(i - prompt_tiles, 0), 0)),
    ]


def _group_pick(prompt_ref, sample_ref, prompt_tiles):
    return jnp.where(pl.program_id(0) < prompt_tiles, prompt_ref[...], sample_ref[...])


def _in_proj_kernel(xp_ref, xs_ref, gain_ref, wm_ref, wt_ref, gbias_ref, zm_ref, zt_ref, a_s,
                    *, prompt_tiles, main_steps):
    j = pl.program_id(1)

    @pl.when(j == 0)
    def _():
        a_s[...] = _rmsnorm(_group_pick(xp_ref, xs_ref, prompt_tiles), gain_ref[...]).astype(BF16)

    for step in range(main_steps):
        @pl.when(j == step)
        def _(base=step * IN_PROJ_COLS):
            for c0 in range(0, IN_PROJ_COLS, 512):
                zm_ref[:, c0:c0 + 512] = _dot(a_s[...], wm_ref[:, base + c0:base + c0 + 512])

    @pl.when(j == main_steps)
    def _():
        a = a_s[...]
        for c0 in range(0, ZT_G, 512):
            zt_ref[:, c0:c0 + 512] = _dot(a, wt_ref[:, c0:c0 + 512])
        gz = _dot(a, wt_ref[:, ZT_G:]) + gbias_ref[...]
        lane = lax.broadcasted_iota(I32, gz.shape, 1)
        log_f = jnp.minimum(gz, 0.0) - jnp.log(1.0 + jnp.exp(-jnp.abs(gz)))
        zt_ref[:, ZT_G:] = jnp.where(lane < HEADS, gz, log_f)


def _in_proj(x_p, x_s, gain, w_bf, w_tail, gbias, tm=512):
    t = x_p.shape[0] + x_s.shape[0]
    pt = x_p.shape[0] // tm
    ms = Z_U // IN_PROJ_COLS
    return pl.pallas_call(
        functools.partial(_in_proj_kernel, prompt_tiles=pt, main_steps=ms),
        grid=(t // tm, ms + 1),
        in_specs=[
            pl.BlockSpec((tm, D_MODEL), lambda i, j: (jnp.minimum(i, pt - 1), 0)),
            pl.BlockSpec((tm, D_MODEL), lambda i, j: (jnp.maximum(i - pt, 0), 0),
                         pipeline_mode=pl.Buffered(1)),
            _resident((1, D_MODEL), lambda i, j: (0, 0)),
            _resident((D_MODEL, Z_U), lambda i, j: (0, 0)),
            _resident((D_MODEL, ZT_COLS), lambda i, j: (0, 0)),
            _resident((1, 128), lambda i, j: (0, 0)),
        ],
        out_specs=[
            pl.BlockSpec((tm, IN_PROJ_COLS), lambda i, j: (i, jnp.minimum(j, ms - 1))),
            pl.BlockSpec((tm, ZT_COLS), lambda i, j: (i, 0)),
        ],
        out_shape=[
            jax.ShapeDtypeStruct((t, Z_U), F32),
            jax.ShapeDtypeStruct((t, ZT_COLS), F32),
        ],
        scratch_shapes=[pltpu.VMEM((tm, D_MODEL), BF16)],
        compiler_params=_cparams(("arbitrary", "arbitrary")),
        name="in_proj",
    )(x_p, x_s, gain, w_bf, w_tail, gbias)


def _mlstm_chunk(q, k, v, ig_col, lf_col, c_state, n_state, m_state):
    L = q.shape[0]
    row = lax.broadcasted_iota(I32, (L, L), 0)
    col = lax.broadcasted_iota(I32, (L, L), 1)
    diag = row == col
    causal = col <= row
    lf_row = jnp.sum(jnp.where(diag, lf_col, 0.0), axis=0, keepdims=True)
    ig_row = jnp.sum(jnp.where(diag, ig_col, 0.0), axis=0, keepdims=True)
    b_col = jnp.sum(jnp.where(causal, lf_row, 0.0), axis=1, keepdims=True)
    b_row = jnp.sum(jnp.where(row <= col, lf_col, 0.0), axis=0, keepdims=True)
    r_row = ig_row - b_row
    r_col = ig_col - b_col
    run_max = jnp.max(jnp.where(causal, r_row, NEG_INF), axis=1, keepdims=True)
    m_run = jnp.maximum(m_state, run_max)
    w_intra = jnp.exp(jnp.where(causal, r_row - m_run, NEG_INF))
    w_inter = jnp.exp(m_state - m_run)

    qb = q.astype(BF16)
    vb = v.astype(BF16)
    s = _dot_nt(qb, k.astype(BF16)) * w_intra
    num = w_inter * _dot(qb, c_state.astype(BF16)) + _dot(s.astype(BF16), vb)
    qn = jnp.sum(q * n_state, axis=1, keepdims=True)
    den = w_inter * qn + jnp.sum(s, axis=1, keepdims=True)
    hb = num / jnp.maximum(jnp.abs(den), jnp.exp(-(b_col + m_run)))

    m_last = m_run[L - 1:L, :]
    m_new = b_col[L - 1:L, :] + m_last
    kw = k * jnp.exp(r_col - m_last)
    sc = jnp.exp(m_state - m_last)
    c_new = sc * c_state + _dot_tn(kw.astype(BF16), vb)
    n_new = sc * n_state + jnp.sum(kw, axis=0, keepdims=True)
    return hb, c_new, n_new, m_new


def _head_out(hb, gain, o_pre):
    return hb * lax.rsqrt(jnp.mean(hb * hb, axis=-1, keepdims=True) + EPS) * gain * _sigmoid(o_pre)


def _mlstm_prompt_kernel(q_ref, k_ref, v_ref, o_ref, g_ref, gain_ref,
                         hm_ref, c_out_ref, n_out_ref, m_out_ref, c_s, n_s, m_s):
    h0 = pl.program_id(1) * PROMPT_HEADS
    c_s[...] = jnp.zeros_like(c_s)
    n_s[...] = jnp.zeros_like(n_s)
    m_s[...] = jnp.zeros_like(m_s)

    def body(ci, carry):
        rows = pl.ds(pl.multiple_of(ci * MLSTM_CHUNK, MLSTM_CHUNK), MLSTM_CHUNK)
        gates = g_ref[rows, :]
        lane = lax.broadcasted_iota(I32, gates.shape, 1)
        for hh in range(PROMPT_HEADS):
            cs = slice(hh * HEAD_DIM, (hh + 1) * HEAD_DIM)
            ig_col = jnp.sum(jnp.where(lane == h0 + hh, gates, 0.0), axis=1, keepdims=True)
            lf_col = jnp.sum(jnp.where(lane == h0 + hh + HEADS, gates, 0.0), axis=1, keepdims=True)
            hb, c_new, n_new, m_new = _mlstm_chunk(
                q_ref[rows, cs], k_ref[rows, cs] * K_SCALE, v_ref[rows, cs],
                ig_col, lf_col, c_s[hh], n_s[hh], m_s[hh][:, 0:1])
            c_s[hh] = c_new
            n_s[hh] = n_new
            m_s[hh] = jnp.broadcast_to(m_new, (1, 128))
            hm_ref[rows, cs] = _head_out(hb, gain_ref[:, cs], o_ref[rows, cs])
        return carry

    lax.fori_loop(0, q_ref.shape[0] // MLSTM_CHUNK, body, 0)
    c_out_ref[...] = c_s[...]
    n_out_ref[...] = n_s[...]
    m_out_ref[...] = m_s[...]


PROMPT_HEADS = 2


def _mlstm_prompt(z, zt, gain, n_seq, seq_len):
    width = PROMPT_HEADS * HEAD_DIM
    hsteps = HEADS // PROMPT_HEADS

    def sec(off):
        return pl.BlockSpec((seq_len, width), lambda b, h, o=off // width: (b, o + h))

    def state(*tail):
        return pl.BlockSpec((PROMPT_HEADS,) + tail, lambda b, h: (b * hsteps + h,) + (0,) * len(tail))

    nbh = n_seq * HEADS
    return pl.pallas_call(
        _mlstm_prompt_kernel,
        grid=(n_seq, hsteps),
        in_specs=[
            sec(Z_Q), sec(Z_K), sec(Z_V), sec(Z_O),
            pl.BlockSpec((seq_len, 128), lambda b, h: (b, ZT_G // 128)),
            pl.BlockSpec((1, width), lambda b, h: (0, h)),
        ],
        out_specs=[
            pl.BlockSpec((seq_len, width), lambda b, h: (b, h)),
            state(HEAD_DIM, HEAD_DIM), state(1, HEAD_DIM), state(1, 128),
        ],
        out_shape=[
            jax.ShapeDtypeStruct((n_seq * seq_len, MLSTM_WIDTH), F32),
            jax.ShapeDtypeStruct((nbh, HEAD_DIM, HEAD_DIM), F32),
            jax.ShapeDtypeStruct((nbh, 1, HEAD_DIM), F32),
            jax.ShapeDtypeStruct((nbh, 1, 128), F32),
        ],
        scratch_shapes=[
            pltpu.VMEM((PROMPT_HEADS, HEAD_DIM, HEAD_DIM), F32),
            pltpu.VMEM((PROMPT_HEADS, 1, HEAD_DIM), F32),
            pltpu.VMEM((PROMPT_HEADS, 1, 128), F32),
        ],
        compiler_params=_cparams(("parallel", "parallel")),
        name="mlstm_prompt",
    )(z, z, z, z, zt, gain)


SAMPLE_SEQ = 4
SAMPLE_ROWS = 16


def _mlstm_sample_kernel(z_ref, g_ref, gain_ref, c_in_ref, n_in_ref, m_in_ref,
                         hm_ref, c_out_ref, n_out_ref, m_out_ref):
    for grp in range(SAMPLE_ROWS // 8):
        r0 = grp * 8
        gates = g_ref[r0:r0 + 8, :]
        row = lax.broadcasted_iota(I32, (8, 1), 0)
        for half in range(2):
            lo = half * SAMPLE_SEQ
            live = jnp.logical_and(row >= lo, row < lo + SAMPLE_SEQ)
            for h in range(HEADS):
                j = (grp * 2 + half) * HEADS + h
                cs = slice(h * HEAD_DIM, (h + 1) * HEAD_DIM)
                q = z_ref[r0:r0 + 8, Z_Q + h * HEAD_DIM:Z_Q + (h + 1) * HEAD_DIM]
                k = z_ref[r0:r0 + 8, Z_K + h * HEAD_DIM:Z_K + (h + 1) * HEAD_DIM]
                v = z_ref[r0:r0 + 8, Z_V + h * HEAD_DIM:Z_V + (h + 1) * HEAD_DIM]
                o = z_ref[r0:r0 + 8, Z_O + h * HEAD_DIM:Z_O + (h + 1) * HEAD_DIM]
                k = jnp.where(live, k * K_SCALE, 0.0)
                v = jnp.where(live, v, 0.0)
                ig_col = jnp.where(live, gates[:, h:h + 1], NEG_INF)
                lf_col = jnp.where(live, gates[:, HEADS + h:HEADS + h + 1], 0.0)
                hb, c_new, n_new, m_new = _mlstm_chunk(
                    q, k, v, ig_col, lf_col, c_in_ref[j], n_in_ref[j], m_in_ref[j][:, 0:1])
                out = _head_out(hb, gain_ref[:, cs], o)
                hm_ref[r0 + lo:r0 + lo + SAMPLE_SEQ, cs] = out[lo:lo + SAMPLE_SEQ, :]
                c_out_ref[j] = c_new
                n_out_ref[j] = n_new
                m_out_ref[j] = jnp.broadcast_to(m_new, (1, 128))


def _mlstm_sample(z, zt, gain, c0, n0, m0, row0, n_rows):
    nbh = c0.shape[0]
    per = SAMPLE_ROWS // SAMPLE_SEQ * HEADS
    rb0 = row0 // SAMPLE_ROWS
    return pl.pallas_call(
        _mlstm_sample_kernel,
        grid=(n_rows // SAMPLE_ROWS,),
        in_specs=[
            pl.BlockSpec((SAMPLE_ROWS, Z_U), lambda i: (rb0 + i, 0)),
            pl.BlockSpec((SAMPLE_ROWS, 128), lambda i: (rb0 + i, ZT_G // 128)),
            pl.BlockSpec((1, MLSTM_WIDTH), lambda i: (0, 0)),
            pl.BlockSpec((per, HEAD_DIM, HEAD_DIM), lambda i: (i, 0, 0)),
            pl.BlockSpec((per, 1, HEAD_DIM), lambda i: (i, 0, 0)),
            pl.BlockSpec((per, 1, 128), lambda i: (i, 0, 0)),
        ],
        out_specs=[
            pl.BlockSpec((SAMPLE_ROWS, MLSTM_WIDTH), lambda i: (i, 0)),
            pl.BlockSpec((per, HEAD_DIM, HEAD_DIM), lambda i: (i, 0, 0)),
            pl.BlockSpec((per, 1, HEAD_DIM), lambda i: (i, 0, 0)),
            pl.BlockSpec((per, 1, 128), lambda i: (i, 0, 0)),
        ],
        out_shape=[
            jax.ShapeDtypeStruct((n_rows, MLSTM_WIDTH), F32),
            jax.ShapeDtypeStruct((nbh, HEAD_DIM, HEAD_DIM), F32),
            jax.ShapeDtypeStruct((nbh, 1, HEAD_DIM), F32),
            jax.ShapeDtypeStruct((nbh, 1, 128), F32),
        ],
        compiler_params=_cparams(("parallel",)),
        name="mlstm_sample",
    )(z, zt, gain, c0, n0, m0)


def _s5_discretise(a_re, a_im, log_dt):
    dt = jnp.exp(log_dt)
    mag = jnp.exp(dt * a_re)
    ab_re = mag * jnp.cos(dt * a_im)
    ab_im = mag * jnp.sin(dt * a_im)
    den = a_re * a_re + a_im * a_im
    xr = ab_re - 1.0
    f_re = (xr * a_re + ab_im * a_im) / den
    f_im = (ab_im * a_re - xr * a_im) / den
    return ab_re, ab_im, f_re, f_im


def _s5_param_kernel(are_e, aim_e, ldt_e, b_re, b_im, are_r, aim_r, ldt_r,
                     bb_re_o, bb_im_o, ab_re_o, ab_im_o):
    _, _, f_re, f_im = _s5_discretise(are_e[...], aim_e[...], ldt_e[...])
    bb_re_o[...] = f_re * b_re[...] - f_im * b_im[...]
    bb_im_o[...] = f_re * b_im[...] + f_im * b_re[...]
    ab_re, ab_im, _, _ = _s5_discretise(are_r[...], aim_r[...], ldt_r[...])
    ab_re_o[...] = ab_re
    ab_im_o[...] = ab_im


def _s5_params(a_re, a_im, log_dt, b_re, b_im):
    g, p = a_re.shape
    c = b_re.shape[-1]
    rep = lambda t: jnp.repeat(t, c, axis=-1)
    ldt_e = jnp.broadcast_to(log_dt[:, None], (g, p * c))
    ldt_r = jnp.broadcast_to(log_dt[:, None], (g, p)).reshape(1, g * p)
    flat = jax.ShapeDtypeStruct((g, p * c), F32)
    rowv = jax.ShapeDtypeStruct((1, g * p), F32)
    return pl.pallas_call(
        _s5_param_kernel,
        out_shape=[flat, flat, rowv, rowv],
        name="s5_params",
    )(rep(a_re), rep(a_im), ldt_e, b_re.reshape(g, p * c), b_im.reshape(g, p * c),
      a_re.reshape(1, g * p), a_im.reshape(1, g * p), ldt_r)


def _blockdiag_in(bb_re, bb_im):
    bb = jnp.stack([bb_re, bb_im]).reshape(2, SSM_BLOCKS, GROUPS_PER_BLOCK, SSM_STATE, SSM_GROUP)
    t = jnp.transpose(bb, (1, 2, 4, 0, 3))
    eye = jnp.eye(GROUPS_PER_BLOCK, dtype=bool)
    w = jnp.where(eye[None, :, None, None, :, None], t[:, :, :, :, None, :], 0.0)
    return w.reshape(SSM_BLOCKS, BLOCK_CH, 2 * BLOCK_ST).astype(BF16)


def _blockdiag_out(cm):
    t = jnp.transpose(cm.reshape(SSM_BLOCKS, GROUPS_PER_BLOCK, SSM_GROUP, SSM_STATE), (0, 1, 3, 2))
    eye = jnp.eye(GROUPS_PER_BLOCK, dtype=bool)
    w = jnp.where(eye[None, :, None, :, None], t[:, :, :, None, :], 0.0)
    return w.reshape(SSM_BLOCKS, BLOCK_ST, BLOCK_CH).astype(BF16)


S5_TILE = 256
S5_SEQS = 4


def _s5_prompt_kernel(u0, u1, u2, u3, wb_ref, wcre_ref, wcim_ref, abre_ref, abim_ref, d_ref,
                      y_ref, sre_ref, sim_ref, u_tm, bu, y_tm, st):
    i = pl.program_id(1)
    half = BLOCK_CH // 2

    @pl.when(i == 0)
    def _():
        st[...] = jnp.zeros_like(st)

    for b, u in enumerate((u0, u1, u2, u3)):
        u_tm[0, pl.ds(b, S5_TILE, stride=S5_SEQS), :] = u[:, :half]
        u_tm[1, pl.ds(b, S5_TILE, stride=S5_SEQS), :] = u[:, half:]
    u_all = jnp.concatenate([u_tm[0], u_tm[1]], axis=1)
    bu[...] = _dot(u_all.astype(BF16), wb_ref[...])
    a_re = abre_ref[...]
    a_im = abim_ref[...]

    first = lax.broadcasted_iota(I32, (2 * S5_SEQS, BLOCK_ST), 0) < S5_SEQS

    def body(t2, carry):
        s_re, s_im = carry
        rows = pl.ds(pl.multiple_of(t2 * 2 * S5_SEQS, 2 * S5_SEQS), 2 * S5_SEQS)
        x_re = bu[rows, :BLOCK_ST]
        x_im = bu[rows, BLOCK_ST:]
        p_re = a_re * s_re - a_im * s_im + x_re
        p_im = a_re * s_im + a_im * s_re + x_im
        r_re = pltpu.roll(p_re, S5_SEQS, 0)
        r_im = pltpu.roll(p_im, S5_SEQS, 0)
        q_re = a_re * r_re - a_im * r_im + x_re
        q_im = a_re * r_im + a_im * r_re + x_im
        bu[rows, :BLOCK_ST] = jnp.where(first, p_re, q_re)
        bu[rows, BLOCK_ST:] = jnp.where(first, p_im, q_im)
        return pltpu.roll(q_re, S5_SEQS, 0), pltpu.roll(q_im, S5_SEQS, 0)

    s_re, s_im = lax.fori_loop(0, S5_TILE // 2, body, (st[:, :BLOCK_ST], st[:, BLOCK_ST:]), unroll=2)
    st[:, :BLOCK_ST] = s_re
    st[:, BLOCK_ST:] = s_im
    y = (_dot(bu[:, :BLOCK_ST].astype(BF16), wcre_ref[...])
         - _dot(bu[:, BLOCK_ST:].astype(BF16), wcim_ref[...]) + d_ref[...] * u_all)
    y = _gelu(y)
    y_tm[0] = y[:, :half]
    y_tm[1] = y[:, half:]
    for b in range(S5_SEQS):
        y_ref[b, :, :half] = y_tm[0, pl.ds(b, S5_TILE, stride=S5_SEQS), :]
        y_ref[b, :, half:] = y_tm[1, pl.ds(b, S5_TILE, stride=S5_SEQS), :]

    @pl.when(i == pl.num_programs(1) - 1)
    def _():
        sre_ref[...] = s_re[:S5_SEQS, :]
        sim_ref[...] = s_im[:S5_SEQS, :]


def _s5_prompt(z, wb, wcre, wcim, ab_re, ab_im, d_row, seq_len):
    nt = seq_len // S5_TILE

    def u_spec(b):
        return pl.BlockSpec((S5_TILE, BLOCK_CH), lambda j, i, b=b: (b * nt + i, j))

    rows = S5_TILE * S5_SEQS
    return pl.pallas_call(
        _s5_prompt_kernel,
        grid=(SSM_BLOCKS, nt),
        in_specs=[u_spec(b) for b in range(S5_SEQS)] + [
            pl.BlockSpec((None, BLOCK_CH, 2 * BLOCK_ST), lambda j, i: (j, 0, 0)),
            pl.BlockSpec((None, BLOCK_ST, BLOCK_CH), lambda j, i: (j, 0, 0)),
            pl.BlockSpec((None, BLOCK_ST, BLOCK_CH), lambda j, i: (j, 0, 0)),
            pl.BlockSpec((1, BLOCK_ST), lambda j, i: (0, j)),
            pl.BlockSpec((1, BLOCK_ST), lambda j, i: (0, j)),
            pl.BlockSpec((1, BLOCK_CH), lambda j, i: (0, j)),
        ],
        out_specs=[
            pl.BlockSpec((S5_SEQS, S5_TILE, BLOCK_CH), lambda j, i: (0, i, j)),
            pl.BlockSpec((S5_SEQS, BLOCK_ST), lambda j, i: (0, j)),
            pl.BlockSpec((S5_SEQS, BLOCK_ST), lambda j, i: (0, j)),
        ],
        out_shape=[
            jax.ShapeDtypeStruct((S5_SEQS, seq_len, SSM_WIDTH), F32),
            jax.ShapeDtypeStruct((S5_SEQS, SSM_GROUPS * SSM_STATE), F32),
            jax.ShapeDtypeStruct((S5_SEQS, SSM_GROUPS * SSM_STATE), F32),
        ],
        scratch_shapes=[
            pltpu.VMEM((2, rows, 128), F32),
            pltpu.VMEM((rows, 2 * BLOCK_ST), F32),
            pltpu.VMEM((2, rows, 128), F32),
            pltpu.VMEM((2 * S5_SEQS, 2 * BLOCK_ST), F32),
        ],
        compiler_params=_cparams(("parallel", "arbitrary")),
        name="s5_prompt",
    )(z, z, z, z, wb, wcre, wcim, ab_re, ab_im, d_row)


def _s5_sample_kernel(u_ref, s0re_ref, s0im_ref, wb_ref, wcre_ref, wcim_ref, abre_ref, abim_ref,
                      d_ref, y_ref, sre_ref, sim_ref, u_sl, y_sl):
    half = BLOCK_CH // 2
    n_seq = s0re_ref.shape[0]
    u_sl[0] = u_ref[:, :half]
    u_sl[1] = u_ref[:, half:]
    a_re = abre_ref[...]
    a_im = abim_ref[...]
    s_re = s0re_ref[...]
    s_im = s0im_ref[...]
    for t in range(SAMPLE_SEQ):
        rows = pl.ds(t, n_seq, stride=SAMPLE_SEQ)
        u_t = jnp.concatenate([u_sl[0, rows, :], u_sl[1, rows, :]], axis=1)
        bu = _dot(u_t.astype(BF16), wb_ref[...])
        n_re = a_re * s_re - a_im * s_im + bu[:, :BLOCK_ST]
        n_im = a_re * s_im + a_im * s_re + bu[:, BLOCK_ST:]
        s_re, s_im = n_re, n_im
        y = (_dot(s_re.astype(BF16), wcre_ref[...]) - _dot(s_im.astype(BF16), wcim_ref[...])
             + d_ref[...] * u_t)
        y = _gelu(y)
        y_sl[0, rows, :] = y[:, :half]
        y_sl[1, rows, :] = y[:, half:]
    y_ref[:, :half] = y_sl[0]
    y_ref[:, half:] = y_sl[1]
    sre_ref[...] = s_re
    sim_ref[...] = s_im


def _s5_sample(z, s0_re, s0_im, wb, wcre, wcim, ab_re, ab_im, d_row, row0, n_rows):
    n_seq = s0_re.shape[0]
    st_spec = pl.BlockSpec((n_seq, BLOCK_ST), lambda j: (0, j))
    return pl.pallas_call(
        _s5_sample_kernel,
        grid=(SSM_BLOCKS,),
        in_specs=[
            pl.BlockSpec((n_rows, BLOCK_CH), lambda j: (row0 // n_rows, j)),
            st_spec, st_spec,
            pl.BlockSpec((None, BLOCK_CH, 2 * BLOCK_ST), lambda j: (j, 0, 0)),
            pl.BlockSpec((None, BLOCK_ST, BLOCK_CH), lambda j: (j, 0, 0)),
            pl.BlockSpec((None, BLOCK_ST, BLOCK_CH), lambda j: (j, 0, 0)),
            pl.BlockSpec((1, BLOCK_ST), lambda j: (0, j)),
            pl.BlockSpec((1, BLOCK_ST), lambda j: (0, j)),
            pl.BlockSpec((1, BLOCK_CH), lambda j: (0, j)),
        ],
        out_specs=[pl.BlockSpec((n_rows, BLOCK_CH), lambda j: (0, j)), st_spec, st_spec],
        out_shape=[
            jax.ShapeDtypeStruct((n_rows, SSM_WIDTH), F32),
            jax.ShapeDtypeStruct(s0_re.shape, F32),
            jax.ShapeDtypeStruct(s0_im.shape, F32),
        ],
        scratch_shapes=[pltpu.VMEM((2, n_rows, 128), F32), pltpu.VMEM((2, n_rows, 128), F32)],
        compiler_params=_cparams(("parallel",)),
        name="s5_sample",
    )(z, s0_re, s0_im, wb, wcre, wcim, ab_re, ab_im, d_row)


def _postmix_kernel(hmp_ref, hms_ref, ysp_ref, yss_ref, xp_ref, xs_ref,
                    wglu_ref, bglu_ref, wout_ref, gain_ref, wq_ref, keys_ref,
                    h1_ref, c_ref, sp_ref, ss_ref, *, prompt_tiles):
    i = pl.program_id(0)
    ys = _group_pick(ysp_ref, yss_ref, prompt_tiles)
    hm = _group_pick(hmp_ref, hms_ref, prompt_tiles)
    glu = ys * _sigmoid(_dot(ys.astype(BF16), wglu_ref[...]) + bglu_ref[...])
    mix = (_dot(hm.astype(BF16), wout_ref[:MLSTM_WIDTH, :])
           + _dot(glu.astype(BF16), wout_ref[MLSTM_WIDTH:, :]))
    h1 = _group_pick(xp_ref, xs_ref, prompt_tiles) + mix
    h1_ref[...] = h1
    c = _rmsnorm(h1, gain_ref[...]).astype(BF16)
    c_ref[...] = c
    qp = _dot(c, wq_ref[...])
    scores = []
    width = KEY_GROUP * PEER_HALF
    for g in range(2 * PEER_HEADS // KEY_GROUP):
        sg = _dot_nt(keys_ref[g], qp[:, g * width:(g + 1) * width].astype(BF16))
        scores += [sg[jj * PEER_KEYS:(jj + 1) * PEER_KEYS, :] for jj in range(KEY_GROUP)]

    @pl.when(i < prompt_tiles)
    def _():
        for j, s in enumerate(scores):
            sp_ref[j] = s

    @pl.when(i >= prompt_tiles)
    def _():
        for j, s in enumerate(scores):
            ss_ref[j] = s


KEY_GROUP = 4


def _blockdiag_keys(keys):
    ng = keys.shape[0] // KEY_GROUP
    k = keys.reshape(ng, KEY_GROUP, PEER_KEYS, PEER_HALF)
    eye = jnp.eye(KEY_GROUP, dtype=bool)
    w = jnp.where(eye[None, :, None, :, None], k[:, :, :, None, :], 0.0)
    return w.reshape(ng, KEY_GROUP * PEER_KEYS, KEY_GROUP * PEER_HALF).astype(BF16)


def _postmix(hm_p, hm_s, ys_p, ys_s, x_p, x_s, wglu, bglu, wout, gain, wq, keys, tm=256):
    n_p, n_s = x_p.shape[0], x_s.shape[0]
    t = n_p + n_s
    pt = n_p // tm
    nk = 2 * PEER_HEADS
    return pl.pallas_call(
        functools.partial(_postmix_kernel, prompt_tiles=pt),
        grid=(t // tm,),
        in_specs=_group_specs(tm, MLSTM_WIDTH, pt) + _group_specs(tm, SSM_WIDTH, pt)
        + _group_specs(tm, D_MODEL, pt) + [
            _resident((SSM_WIDTH, SSM_WIDTH), lambda i: (0, 0)),
            _resident((1, SSM_WIDTH), lambda i: (0, 0)),
            _resident((D_MODEL, D_MODEL), lambda i: (0, 0)),
            _resident((1, D_MODEL), lambda i: (0, 0)),
            _resident((D_MODEL, PEER_HEADS * 2 * PEER_HALF), lambda i: (0, 0)),
            _resident(keys.shape, lambda i: (0, 0, 0)),
        ],
        out_specs=[
            pl.BlockSpec((tm, D_MODEL), lambda i: (i, 0)),
            pl.BlockSpec((tm, D_MODEL), lambda i: (i, 0)),
            pl.BlockSpec((nk, PEER_KEYS, tm), lambda i: (0, 0, jnp.minimum(i, pt - 1))),
            pl.BlockSpec((nk, PEER_KEYS, tm), lambda i: (0, 0, jnp.maximum(i - pt, 0))),
        ],
        out_shape=[
            jax.ShapeDtypeStruct((t, D_MODEL), F32),
            jax.ShapeDtypeStruct((t, D_MODEL), BF16),
            jax.ShapeDtypeStruct((nk, PEER_KEYS, n_p), F32),
            jax.ShapeDtypeStruct((nk, PEER_KEYS, n_s), F32),
        ],
        compiler_params=_cparams(("arbitrary",)),
        name="postmix",
    )(hm_p, hm_s, ys_p, ys_s, x_p, x_s, wglu, bglu, wout, gain, wq, keys)


ID_NONE = 1 << 20
TOPK_SUB = 8


def _sort_network(n):
    pairs = []
    p = 1
    while p < n:
        k = p
        while k >= 1:
            for j in range(k % p, n - k, 2 * k):
                for i in range(min(k, n - j - k)):
                    if (i + j) // (2 * p) == (i + j + k) // (2 * p):
                        pairs.append((i + j, i + j + k))
            k //= 2
        p *= 2
    return pairs


_SORT16 = _sort_network(PEER_TOPK)


def _precedes(b, a):
    (vb, ib), (va, ia) = b, a
    return jnp.logical_or(vb > va, jnp.logical_and(vb == va, ib < ia))


def _first_of(a, b):
    sw = _precedes(b, a)
    return jnp.where(sw, b[0], a[0]), jnp.where(sw, b[1], a[1])


def _exchange(items, i, j):
    a, b = items[i], items[j]
    sw = _precedes(b, a)
    items[i] = (jnp.where(sw, b[0], a[0]), jnp.where(sw, b[1], a[1]))
    items[j] = (jnp.where(sw, a[0], b[0]), jnp.where(sw, a[1], b[1]))


def _sort16(items):
    items = list(items)
    for i, j in _SORT16:
        _exchange(items, i, j)
    return items


def _bitonic_merge16(items):
    items = list(items)
    d = PEER_TOPK // 2
    while d >= 1:
        for i in range(PEER_TOPK):
            if i & d == 0:
                _exchange(items, i, i + d)
        d //= 2
    return items


def _merge_top16(a, b):
    return _bitonic_merge16([_first_of(a[i], b[PEER_TOPK - 1 - i]) for i in range(PEER_TOPK)])


def _top16_of_keys(s_ref, half, shape):
    best = None
    for g in range(PEER_KEYS // PEER_TOPK):
        grp = _sort16([(s_ref[half, g * PEER_TOPK + k], jnp.full(shape, g * PEER_TOPK + k, I32))
                       for k in range(PEER_TOPK)])
        best = grp if best is None else _merge_top16(best, grp)
    return best


def _route_head(s_ref):
    shape = s_ref.shape[2:]
    top1 = _top16_of_keys(s_ref, 0, shape)
    top2 = _top16_of_keys(s_ref, 1, shape)

    def pair(i, j):
        return top1[i][0] + top2[j][0], jnp.full(shape, i * PEER_TOPK + j, I32)

    pad = (jnp.full(shape, NEG_INF, F32), jnp.full(shape, ID_NONE, I32))
    g0 = [pair(0, j) for j in range(16)]
    g1 = _bitonic_merge16([pair(1, j) for j in range(8)] + [pair(i, 0) for i in range(15, 7, -1)])
    g2 = _sort16([pair(i, j) for i in range(2, 7) for j in range(PEER_TOPK // (i + 1))])
    g3 = [pair(7, 0), pair(7, 1)] + [pad] * 14
    best = _merge_top16(_merge_top16(g0, g1), _merge_top16(g2, g3))

    mx = best[0][0]
    e1s, e2s, exps = [], [], []
    for k in range(PEER_TOPK):
        v, pid = best[k]
        a = lax.shift_right_logical(pid, 4)
        b = jnp.bitwise_and(pid, PEER_TOPK - 1)
        e1 = jnp.zeros(shape, I32)
        e2 = jnp.zeros(shape, I32)
        for r in range(PEER_TOPK):
            e1 = jnp.where(a == r, top1[r][1], e1)
            e2 = jnp.where(b == r, top2[r][1], e2)
        e1s.append(e1)
        e2s.append(e2)
        exps.append(jnp.exp(v - mx))
    total = exps[0]
    for k in range(1, PEER_TOPK):
        total = total + exps[k]
    return e1s, e2s, [e / total for e in exps]


def _topk_kernel(s_ref, e1_ref, e2_ref, g_ref):
    e1s, e2s, gs = _route_head(s_ref)
    for k in range(PEER_TOPK):
        e1_ref[k] = e1s[k]
        e2_ref[k] = e2s[k]
        g_ref[k] = gs[k]


def _topk(scores):
    t = scores.shape[-1]
    ng = t // 128
    sub = min(ng, TOPK_SUB)
    r = PEER_HEADS * PEER_TOPK
    o_spec = pl.BlockSpec((PEER_TOPK, sub, 128), lambda i, h: (h, i, 0))
    outs = pl.pallas_call(
        _topk_kernel,
        grid=(ng // sub, PEER_HEADS),
        in_specs=[pl.BlockSpec((2, PEER_KEYS, sub, 128), lambda i, h: (h, 0, i, 0))],
        out_specs=[o_spec, o_spec, o_spec],
        out_shape=[
            jax.ShapeDtypeStruct((r, ng, 128), I32),
            jax.ShapeDtypeStruct((r, ng, 128), I32),
            jax.ShapeDtypeStruct((r, ng, 128), F32),
        ],
        compiler_params=_cparams(("parallel", "parallel")),
        name="topk",
    )(scores.reshape(2 * PEER_HEADS, PEER_KEYS, ng, 128))
    return [o.reshape(r, t) for o in outs]


WB_GROUP = 16
WB_PITCH = 132


def _token_weights(e1_row, e2_row, g_row):
    sub = lax.broadcasted_iota(I32, (PEER_KEYS, PEER_KEYS), 0)
    onehot1 = jnp.where(sub == e1_row, 1.0, 0.0).astype(BF16)
    gated2 = jnp.where(sub == e2_row, g_row, 0.0).astype(BF16)
    return _dot_nt(onehot1, gated2)


def _wbuild_kernel(e1_ref, e2_ref, g_ref, w_ref, stage):
    for grp in range(e1_ref.shape[0] // WB_GROUP):
        base = (grp % 2) * WB_GROUP * WB_PITCH
        for tt in range(WB_GROUP):
            t = grp * WB_GROUP + tt
            stage[base + tt * WB_PITCH:base + tt * WB_PITCH + PEER_KEYS, :] = _token_weights(
                e1_ref[t:t + 1, :], e2_ref[t:t + 1, :], g_ref[t:t + 1, :])
        for e in range(PEER_KEYS):
            blk = stage[pl.ds(base + e, WB_GROUP, stride=WB_PITCH), :]
            w_ref[e, grp * WB_GROUP:(grp + 1) * WB_GROUP, :] = blk.astype(BF16)


def _wbuild(e1, e2, g, tw=128):
    t = e1.shape[0]
    i_spec = pl.BlockSpec((tw, PEER_KEYS), lambda i: (i, 0))
    return pl.pallas_call(
        _wbuild_kernel,
        grid=(t // tw,),
        in_specs=[i_spec, i_spec, i_spec],
        out_specs=pl.BlockSpec((PEER_KEYS, tw, PEER_KEYS), lambda i: (0, i, 0)),
        out_shape=jax.ShapeDtypeStruct((PEER_KEYS, t, PEER_KEYS), BF16),
        scratch_shapes=[pltpu.VMEM((2 * WB_GROUP * WB_PITCH, PEER_KEYS), F32)],
        compiler_params=_cparams(("parallel",)),
        name="wbuild",
    )(e1, e2, g)


PEER_EB = 512


def _peer_kernel(c_ref, u_ref, v_ref, w_ref, o_ref, s_s):
    j = pl.program_id(1)
    last = pl.num_programs(1) - 1

    def gated():
        parts = [_gelu(s_s[:, k * PEER_KEYS:(k + 1) * PEER_KEYS]) * w_ref[k].astype(F32)
                 for k in range(PEER_EB // PEER_KEYS)]
        return jnp.concatenate(parts, axis=1).astype(BF16)

    def scores():
        return _dot_nt(c_ref[...], u_ref[...].astype(BF16))

    @pl.when(j == 0)
    def _():
        o_ref[...] = jnp.zeros_like(o_ref)
        s_s[...] = scores()

    @pl.when(jnp.logical_and(j > 0, j < last))
    def _():
        wact = gated()
        s_s[...] = scores()
        o_ref[...] += _dot(wact, v_ref[...].astype(BF16))

    @pl.when(j == last)
    def _():
        o_ref[...] += _dot(gated(), v_ref[...].astype(BF16))


def _peer(c, u, v, w3, n_tiles=8):
    t = c.shape[0]
    tm = t // n_tiles
    nb = PEER_EXPERTS // PEER_EB
    return pl.pallas_call(
        _peer_kernel,
        grid=(n_tiles, nb + 1),
        in_specs=[
            pl.BlockSpec((tm, D_MODEL), lambda i, j: (i, 0), pipeline_mode=pl.Buffered(1)),
            pl.BlockSpec((PEER_EB, D_MODEL), lambda i, j: (jnp.minimum(j, nb - 1), 0)),
            pl.BlockSpec((PEER_EB, D_MODEL), lambda i, j: (jnp.maximum(j - 1, 0), 0)),
            pl.BlockSpec((PEER_EB // PEER_KEYS, tm, PEER_KEYS),
                         lambda i, j: (jnp.maximum(j - 1, 0), i, 0)),
        ],
        out_specs=pl.BlockSpec((tm, D_MODEL), lambda i, j: (i, 0)),
        out_shape=jax.ShapeDtypeStruct((t, D_MODEL), F32),
        scratch_shapes=[pltpu.VMEM((tm, PEER_EB), F32)],
        compiler_params=_cparams(("parallel", "arbitrary")),
        name="peer",
    )(c, u, v, w3)


def _tail_kernel(h1_ref, peer_ref, pp_ref, ps_ref, gple_ref, wgate_ref, wproj_ref, gfin_ref,
                 yp_ref, ys_ref, *, prompt_tiles):
    i = pl.program_id(0)
    h2 = h1_ref[...] + peer_ref[...]
    gate = _sigmoid(_dot(_rmsnorm(h2, gple_ref[...]).astype(BF16), wgate_ref[...]))
    e = _dot(_group_pick(pp_ref, ps_ref, prompt_tiles).astype(BF16), wproj_ref[...])
    y = _rmsnorm(h2 + e * gate, gfin_ref[...])

    @pl.when(i < prompt_tiles)
    def _():
        yp_ref[...] = y

    @pl.when(i >= prompt_tiles)
    def _():
        ys_ref[...] = y


def _tail(h1, peer, p_p, p_s, gple, wgate, wproj, gfin, tm=512):
    t = h1.shape[0]
    n_prompt = p_p.shape[0]
    pt = n_prompt // tm
    ple = p_p.shape[1]
    return pl.pallas_call(
        functools.partial(_tail_kernel, prompt_tiles=pt),
        grid=(t // tm,),
        in_specs=[
            pl.BlockSpec((tm, D_MODEL), lambda i: (i, 0)),
            pl.BlockSpec((tm, D_MODEL), lambda i: (i, 0)),
        ] + _group_specs(tm, ple, pt) + [
            _resident((1, D_MODEL), lambda i: (0, 0)),
            _resident((D_MODEL, D_MODEL), lambda i: (0, 0)),
            _resident((ple, D_MODEL), lambda i: (0, 0)),
            _resident((1, D_MODEL), lambda i: (0, 0)),
        ],
        out_specs=[
            pl.BlockSpec((tm, D_MODEL), lambda i: (jnp.minimum(i, pt - 1), 0)),
            pl.BlockSpec((tm, D_MODEL), lambda i: (jnp.maximum(i - pt, 0), 0)),
        ],
        out_shape=[
            jax.ShapeDtypeStruct((n_prompt, D_MODEL), F32),
            jax.ShapeDtypeStruct((t - n_prompt, D_MODEL), F32),
        ],
        compiler_params=_cparams(("arbitrary",)),
        name="tail",
    )(h1, peer, p_p, p_s, gple, wgate, wproj, gfin)


def kernel(x_prompt, x_sample, state_mlstm_C, state_mlstm_n, state_mlstm_m, state_ssm_re, state_ssm_im, p_prompt, p_sample, norm_mix, w_in, b_igate, b_fgate, mlstm_norm, ssm_A_re, ssm_A_im, ssm_B_re, ssm_B_im, ssm_C_re, ssm_C_im, ssm_D, ssm_log_dt, w_glu, b_glu, w_out, norm_ffn, peer_w_q, peer_keys, peer_u, peer_v, norm_ple, w_ple_gate, w_ple_proj, norm_final):
    n_pseq, p_len, _ = x_prompt.shape
    n_sseq, s_len, _ = x_sample.shape
    assert s_len == SAMPLE_SEQ and n_pseq == S5_SEQS and w_in.shape[0] == 1
    n_prompt = n_pseq * p_len
    n_sample = n_sseq * s_len
    row = lambda t: t.reshape(1, -1)

    x_p = x_prompt.reshape(n_prompt, D_MODEL)
    x_s = x_sample.reshape(n_sample, D_MODEL)

    w_bf = w_in[0].astype(BF16)
    n_gate = 2 * HEADS
    w_tail = jnp.concatenate(
        [w_bf[:, Z_U + n_gate:], w_bf[:, Z_U:Z_U + n_gate],
         jnp.zeros((D_MODEL, 128 - n_gate), BF16)], axis=1)
    gbias = jnp.concatenate([b_igate[0], b_fgate[0], jnp.zeros((128 - n_gate,), F32)]).reshape(1, 128)
    z, zt = _in_proj(x_p, x_s, row(norm_mix[0]), w_bf, w_tail, gbias)

    gain_m = row(mlstm_norm[0])
    hm_p, c_p, n_p, m_p = _mlstm_prompt(z, zt, gain_m, n_pseq, p_len)
    nbh = n_sseq * HEADS
    hm_s, c_s, n_s, m_s = _mlstm_sample(
        z, zt, gain_m,
        state_mlstm_C[0].reshape(nbh, HEAD_DIM, HEAD_DIM),
        state_mlstm_n[0].reshape(nbh, 1, HEAD_DIM),
        jnp.broadcast_to(state_mlstm_m[0].reshape(nbh, 1, 1), (nbh, 1, 128)),
        n_prompt, n_sample)

    bb_re, bb_im, ab_re, ab_im = _s5_params(ssm_A_re[0], ssm_A_im[0], ssm_log_dt[0], ssm_B_re[0], ssm_B_im[0])
    wb = _blockdiag_in(bb_re, bb_im)
    wcre = _blockdiag_out(ssm_C_re[0])
    wcim = _blockdiag_out(ssm_C_im[0])
    d_row = row(ssm_D[0])
    ys_p, sre_p, sim_p = _s5_prompt(zt, wb, wcre, wcim, ab_re, ab_im, d_row, p_len)
    n_st = SSM_GROUPS * SSM_STATE
    ys_s, sre_s, sim_s = _s5_sample(
        zt, state_ssm_re[0].reshape(n_sseq, n_st), state_ssm_im[0].reshape(n_sseq, n_st),
        wb, wcre, wcim, ab_re, ab_im, d_row, n_prompt, n_sample)

    keys = _blockdiag_keys(peer_keys[0].reshape(2 * PEER_HEADS, PEER_KEYS, PEER_HALF))
    h1, c, scores_p, scores_s = _postmix(
        hm_p, hm_s, ys_p.reshape(n_prompt, SSM_WIDTH), ys_s, x_p, x_s,
        w_glu[0].astype(BF16), row(b_glu[0]), w_out[0].astype(BF16),
        row(norm_ffn[0]), peer_w_q[0].astype(BF16), keys)
    routing = [jnp.concatenate([rp, rs], axis=1).T for rp, rs in zip(_topk(scores_p), _topk(scores_s))]
    w3 = _wbuild(*routing)
    peer = _peer(c, peer_u[0], peer_v[0], w3)

    y_p, y_s = _tail(h1, peer, p_prompt[0].reshape(n_prompt, -1), p_sample[0].reshape(n_sample, -1),
                     row(norm_ple[0]), w_ple_gate[0].astype(BF16), w_ple_proj[0].astype(BF16),
                     row(norm_final))

    st_shape = (1, -1, SSM_GROUPS, SSM_STATE)
    return (
        y_p.reshape(x_prompt.shape), y_s.reshape(x_sample.shape),
        c_p.reshape(1, n_pseq, HEADS, HEAD_DIM, HEAD_DIM), n_p.reshape(1, n_pseq, HEADS, HEAD_DIM),
        m_p[:, 0, 0].reshape(1, n_pseq, HEADS),
        sre_p.reshape(st_shape), sim_p.reshape(st_shape),
        c_s.reshape(1, n_sseq, HEADS, HEAD_DIM, HEAD_DIM), n_s.reshape(1, n_sseq, HEADS, HEAD_DIM),
        m_s[:, 0, 0].reshape(1, n_sseq, HEADS),
        sre_s.reshape(st_shape), sim_s.reshape(st_shape),
    )
```

```python
import functools

import jax
import jax.numpy as jnp
from jax import lax
from jax.experimental import pallas as pl
from jax.experimental.pallas import tpu as pltpu

F32 = jnp.float32
BF16 = jnp.bfloat16
I32 = jnp.int32

EPS = 1e-6
D_MODEL = 2048
HEADS = 4
HEAD_DIM = 256
MLSTM_WIDTH = HEADS * HEAD_DIM
SSM_WIDTH = 1024
SSM_GROUPS = 64
SSM_STATE = 64
SSM_GROUP = 16
GROUPS_PER_BLOCK = 16
SSM_BLOCKS = SSM_GROUPS // GROUPS_PER_BLOCK
BLOCK_CH = GROUPS_PER_BLOCK * SSM_GROUP
BLOCK_ST = GROUPS_PER_BLOCK * SSM_STATE
PEER_HEADS = 8
PEER_KEYS = 128
PEER_TOPK = 16
PEER_HALF = 64
PEER_EXPERTS = PEER_KEYS * PEER_KEYS

Z_Q, Z_K, Z_V, Z_O, Z_U = 0, 1024, 2048, 3072, 4096
ZT_G = 1024
ZT_COLS = ZT_G + 128
IN_PROJ_COLS = 2048

MLSTM_CHUNK = 256
K_SCALE = HEAD_DIM ** -0.5
NEG_INF = float("-inf")

VMEM_LIMIT = 56 * 1024 * 1024


def _cparams(semantics):
    return pltpu.CompilerParams(dimension_semantics=semantics, vmem_limit_bytes=VMEM_LIMIT)


def _resident(shape, index_map):
    return pl.BlockSpec(shape, index_map, pipeline_mode=pl.Buffered(1))


def _sigmoid(x):
    return 1.0 / (1.0 + jnp.exp(-x))


GELU_C1 = 0.7978845608028654
GELU_C2 = GELU_C1 * 0.044715


def _gelu(x):
    return x * (0.5 * (1.0 + jnp.tanh(GELU_C1 * (x + 0.044715 * (x * x * x)))))


def _rmsnorm(x, gain):
    return x * lax.rsqrt(jnp.mean(x * x, axis=-1, keepdims=True) + EPS) * gain


def _dot(a, b):
    return jnp.dot(a, b, preferred_element_type=F32)


def _dot_nt(a, b):
    return lax.dot_general(a, b, (((1,), (1,)), ((), ())), preferred_element_type=F32)


def _dot_tn(a, b):
    return lax.dot_general(a, b, (((0,), (0,)), ((), ())), preferred_element_type=F32)


def _group_specs(tm, width, prompt_tiles):
    return [
        pl.BlockSpec((tm, width), lambda i: (jnp.minimum(i, prompt_tiles - 1), 0)),
        pl.BlockSpec((tm, width), lambda i: (jnp.maximum(i - prompt_tiles, 0), 0)),
    ]


def _group_pick(prompt_ref, sample_ref, prompt_tiles):
    return jnp.where(pl.program_id(0) < prompt_tiles, prompt_ref[...], sample_ref[...])


def _in_proj_kernel(xp_ref, xs_ref, gain_ref, wm_ref, wt_ref, gbias_ref, zm_ref, zt_ref, a_s,
                    *, prompt_tiles, main_steps):
    j = pl.program_id(1)

    @pl.when(j == 0)
    def _():
        a_s[...] = _rmsnorm(_group_pick(xp_ref, xs_ref, prompt_tiles), gain_ref[...]).astype(BF16)

    for step in range(main_steps):
        @pl.when(j == step)
        def _(base=step * IN_PROJ_COLS):
            for c0 in range(0, IN_PROJ_COLS, 512):
                zm_ref[:, c0:c0 + 512] = _dot(a_s[...], wm_ref[:, base + c0:base + c0 + 512])

    @pl.when(j == main_steps)
    def _():
        a = a_s[...]
        for c0 in range(0, ZT_G, 512):
            zt_ref[:, c0:c0 + 512] = _dot(a, wt_ref[:, c0:c0 + 512])
        gz = _dot(a, wt_ref[:, ZT_G:]) + gbias_ref[...]
        lane = lax.broadcasted_iota(I32, gz.shape, 1)
        log_f = jnp.minimum(gz, 0.0) - jnp.log(1.0 + jnp.exp(-jnp.abs(gz)))
        zt_ref[:, ZT_G:] = jnp.where(lane < HEADS, gz, log_f)


def _in_proj(x_p, x_s, gain, w_bf, w_tail, gbias, tm=512):
    t = x_p.shape[0] + x_s.shape[0]
    pt = x_p.shape[0] // tm
    ms = Z_U // IN_PROJ_COLS
    return pl.pallas_call(
        functools.partial(_in_proj_kernel, prompt_tiles=pt, main_steps=ms),
        grid=(t // tm, ms + 1),
        in_specs=[
            pl.BlockSpec((tm, D_MODEL), lambda i, j: (jnp.minimum(i, pt - 1), 0)),
            pl.BlockSpec((tm, D_MODEL), lambda i, j: (jnp.maximum(i - pt, 0), 0),
                         pipeline_mode=pl.Buffered(1)),
            _resident((1, D_MODEL), lambda i, j: (0, 0)),
            _resident((D_MODEL, Z_U), lambda i, j: (0, 0)),
            _resident((D_MODEL, ZT_COLS), lambda i, j: (0, 0)),
            _resident((1, 128), lambda i, j: (0, 0)),
        ],
        out_specs=[
            pl.BlockSpec((tm, IN_PROJ_COLS), lambda i, j: (i, jnp.minimum(j, ms - 1))),
            pl.BlockSpec((tm, ZT_COLS), lambda i, j: (i, 0)),
        ],
        out_shape=[
            jax.ShapeDtypeStruct((t, Z_U), F32),
            jax.ShapeDtypeStruct((t, ZT_COLS), F32),
        ],
        scratch_shapes=[pltpu.VMEM((tm, D_MODEL), BF16)],
        compiler_params=_cparams(("arbitrary", "arbitrary")),
        name="in_proj",
    )(x_p, x_s, gain, w_bf, w_tail, gbias)


def _mlstm_chunk(q, k, v, ig_col, lf_col, c_state, n_state, m_state):
    L = q.shape[0]
    row = lax.broadcasted_iota(I32, (L, L), 0)
    col = lax.broadcasted_iota(I32, (L, L), 1)
    diag = row == col
    causal = col <= row
    lf_row = jnp.sum(jnp.where(diag, lf_col, 0.0), axis=0, keepdims=True)
    ig_row = jnp.sum(jnp.where(diag, ig_col, 0.0), axis=0, keepdims=True)
    b_col = jnp.sum(jnp.where(causal, lf_row, 0.0), axis=1, keepdims=True)
    b_row = jnp.sum(jnp.where(row <= col, lf_col, 0.0), axis=0, keepdims=True)
    r_row = ig_row - b_row
    r_col = ig_col - b_col
    run_max = jnp.max(jnp.where(causal, r_row, NEG_INF), axis=1, keepdims=True)
    m_run = jnp.maximum(m_state, run_max)
    w_intra = jnp.exp(jnp.where(causal, r_row - m_run, NEG_INF))
    w_inter = jnp.exp(m_state - m_run)

    qb = q.astype(BF16)
    vb = v.astype(BF16)
    s = _dot_nt(qb, k.astype(BF16)) * w_intra
    num = w_inter * _dot(qb, c_state.astype(BF16)) + _dot(s.astype(BF16), vb)
    qn = jnp.sum(q * n_state, axis=1, keepdims=True)
    den = w_inter * qn + jnp.sum(s, axis=1, keepdims=True)
    hb = num / jnp.maximum(jnp.abs(den), jnp.exp(-(b_col + m_run)))

    m_last = m_run[L - 1:L, :]
    m_new = b_col[L - 1:L, :] + m_last
    kw = k * jnp.exp(r_col - m_last)
    sc = jnp.exp(m_state - m_last)
    c_new = sc * c_state + _dot_tn(kw.astype(BF16), vb)
    n_new = sc * n_state + jnp.sum(kw, axis=0, keepdims=True)
    return hb, c_new, n_new, m_new


def _head_out(hb, gain, o_pre):
    return hb * lax.rsqrt(jnp.mean(hb * hb, axis=-1, keepdims=True) + EPS) * gain * _sigmoid(o_pre)


def _mlstm_prompt_kernel(q_ref, k_ref, v_ref, o_ref, g_ref, gain_ref,
                         hm_ref, c_out_ref, n_out_ref, m_out_ref, c_s, n_s, m_s):
    h0 = pl.program_id(1) * PROMPT_HEADS
    c_s[...] = jnp.zeros_like(c_s)
    n_s[...] = jnp.zeros_like(n_s)
    m_s[...] = jnp.zeros_like(m_s)

    def body(ci, carry):
        rows = pl.ds(pl.multiple_of(ci * MLSTM_CHUNK, MLSTM_CHUNK), MLSTM_CHUNK)
        gates = g_ref[rows, :]
        lane = lax.broadcasted_iota(I32, gates.shape, 1)
        for hh in range(PROMPT_HEADS):
            cs = slice(hh * HEAD_DIM, (hh + 1) * HEAD_DIM)
            ig_col = jnp.sum(jnp.where(lane == h0 + hh, gates, 0.0), axis=1, keepdims=True)
            lf_col = jnp.sum(jnp.where(lane == h0 + hh + HEADS, gates, 0.0), axis=1, keepdims=True)
            hb, c_new, n_new, m_new = _mlstm_chunk(
                q_ref[rows, cs], k_ref[rows, cs] * K_SCALE, v_ref[rows, cs],
                ig_col, lf_col, c_s[hh], n_s[hh], m_s[hh][:, 0:1])
            c_s[hh] = c_new
            n_s[hh] = n_new
            m_s[hh] = jnp.broadcast_to(m_new, (1, 128))
            hm_ref[rows, cs] = _head_out(hb, gain_ref[:, cs], o_ref[rows, cs])
        return carry

    lax.fori_loop(0, q_ref.shape[0] // MLSTM_CHUNK, body, 0)
    c_out_ref[...] = c_s[...]
    n_out_ref[...] = n_s[...]
    m_out_ref[...] = m_s[...]


PROMPT_HEADS = 2


def _mlstm_prompt(z, zt, gain, n_seq, seq_len):
    width = PROMPT_HEADS * HEAD_DIM
    hsteps = HEADS // PROMPT_HEADS

    def sec(off):
        return pl.BlockSpec((seq_len, width), lambda b, h, o=off // width: (b, o + h))

    def state(*tail):
        return pl.BlockSpec((PROMPT_HEADS,) + tail, lambda b, h: (b * hsteps + h,) + (0,) * len(tail))

    nbh = n_seq * HEADS
    return pl.pallas_call(
        _mlstm_prompt_kernel,
        grid=(n_seq, hsteps),
        in_specs=[
            sec(Z_Q), sec(Z_K), sec(Z_V), sec(Z_O),
            pl.BlockSpec((seq_len, 128), lambda b, h: (b, ZT_G // 128)),
            pl.BlockSpec((1, width), lambda b, h: (0, h)),
        ],
        out_specs=[
            pl.BlockSpec((seq_len, width), lambda b, h: (b, h)),
            state(HEAD_DIM, HEAD_DIM), state(1, HEAD_DIM), state(1, 128),
        ],
        out_shape=[
            jax.ShapeDtypeStruct((n_seq * seq_len, MLSTM_WIDTH), F32),
            jax.ShapeDtypeStruct((nbh, HEAD_DIM, HEAD_DIM), F32),
            jax.ShapeDtypeStruct((nbh, 1, HEAD_DIM), F32),
            jax.ShapeDtypeStruct((nbh, 1, 128), F32),
        ],
        scratch_shapes=[
            pltpu.VMEM((PROMPT_HEADS, HEAD_DIM, HEAD_DIM), F32),
            pltpu.VMEM((PROMPT_HEADS, 1, HEAD_DIM), F32),
            pltpu.VMEM((PROMPT_HEADS, 1, 128), F32),
        ],
        compiler_params=_cparams(("parallel", "parallel")),
        name="mlstm_prompt",
    )(z, z, z, z, zt, gain)


SAMPLE_SEQ = 4
SAMPLE_ROWS = 16


def _mlstm_sample_kernel(z_ref, g_ref, gain_ref, c_in_ref, n_in_ref, m_in_ref,
                         hm_ref, c_out_ref, n_out_ref, m_out_ref):
    for grp in range(SAMPLE_ROWS // 8):
        r0 = grp * 8
        gates = g_ref[r0:r0 + 8, :]
        row = lax.broadcasted_iota(I32, (8, 1), 0)
        for half in range(2):
            lo = half * SAMPLE_SEQ
            live = jnp.logical_and(row >= lo, row < lo + SAMPLE_SEQ)
            for h in range(HEADS):
                j = (grp * 2 + half) * HEADS + h
                cs = slice(h * HEAD_DIM, (h + 1) * HEAD_DIM)
                q = z_ref[r0:r0 + 8, Z_Q + h * HEAD_DIM:Z_Q + (h + 1) * HEAD_DIM]
                k = z_ref[r0:r0 + 8, Z_K + h * HEAD_DIM:Z_K + (h + 1) * HEAD_DIM]
                v = z_ref[r0:r0 + 8, Z_V + h * HEAD_DIM:Z_V + (h + 1) * HEAD_DIM]
                o = z_ref[r0:r0 + 8, Z_O + h * HEAD_DIM:Z_O + (h + 1) * HEAD_DIM]
                k = jnp.where(live, k * K_SCALE, 0.0)
                v = jnp.where(live, v, 0.0)
                ig_col = jnp.where(live, gates[:, h:h + 1], NEG_INF)
                lf_col = jnp.where(live, gates[:, HEADS + h:HEADS + h + 1], 0.0)
                hb, c_new, n_new, m_new = _mlstm_chunk(
                    q, k, v, ig_col, lf_col, c_in_ref[j], n_in_ref[j], m_in_ref[j][:, 0:1])
                out = _head_out(hb, gain_ref[:, cs], o)
                hm_ref[r0 + lo:r0 + lo + SAMPLE_SEQ, cs] = out[lo:lo + SAMPLE_SEQ, :]
                c_out_ref[j] = c_new
                n_out_ref[j] = n_new
                m_out_ref[j] = jnp.broadcast_to(m_new, (1, 128))


def _mlstm_sample(z, zt, gain, c0, n0, m0, row0, n_rows):
    nbh = c0.shape[0]
    per = SAMPLE_ROWS // SAMPLE_SEQ * HEADS
    rb0 = row0 // SAMPLE_ROWS
    return pl.pallas_call(
        _mlstm_sample_kernel,
        grid=(n_rows // SAMPLE_ROWS,),
        in_specs=[
            pl.BlockSpec((SAMPLE_ROWS, Z_U), lambda i: (rb0 + i, 0)),
            pl.BlockSpec((SAMPLE_ROWS, 128), lambda i: (rb0 + i, ZT_G // 128)),
            pl.BlockSpec((1, MLSTM_WIDTH), lambda i: (0, 0)),
            pl.BlockSpec((per, HEAD_DIM, HEAD_DIM), lambda i: (i, 0, 0)),
            pl.BlockSpec((per, 1, HEAD_DIM), lambda i: (i, 0, 0)),
            pl.BlockSpec((per, 1, 128), lambda i: (i, 0, 0)),
        ],
        out_specs=[
            pl.BlockSpec((SAMPLE_ROWS, MLSTM_WIDTH), lambda i: (i, 0)),
            pl.BlockSpec((per, HEAD_DIM, HEAD_DIM), lambda i: (i, 0, 0)),
            pl.BlockSpec((per, 1, HEAD_DIM), lambda i: (i, 0, 0)),
            pl.BlockSpec((per, 1, 128), lambda i: (i, 0, 0)),
        ],
        out_shape=[
            jax.ShapeDtypeStruct((n_rows, MLSTM_WIDTH), F32),
            jax.ShapeDtypeStruct((nbh, HEAD_DIM, HEAD_DIM), F32),
            jax.ShapeDtypeStruct((nbh, 1, HEAD_DIM), F32),
            jax.ShapeDtypeStruct((nbh, 1, 128), F32),
        ],
        compiler_params=_cparams(("parallel",)),
        name="mlstm_sample",
    )(z, zt, gain, c0, n0, m0)


def _s5_discretise(a_re, a_im, log_dt):
    dt = jnp.exp(log_dt)
    mag = jnp.exp(dt * a_re)
    ab_re = mag * jnp.cos(dt * a_im)
    ab_im = mag * jnp.sin(dt * a_im)
    den = a_re * a_re + a_im * a_im
    xr = ab_re - 1.0
    f_re = (xr * a_re + ab_im * a_im) / den
    f_im = (ab_im * a_re - xr * a_im) / den
    return ab_re, ab_im, f_re, f_im


def _s5_param_kernel(are_e, aim_e, ldt_e, b_re, b_im, are_r, aim_r, ldt_r,
                     bb_re_o, bb_im_o, ab_re_o, ab_im_o):
    _, _, f_re, f_im = _s5_discretise(are_e[...], aim_e[...], ldt_e[...])
    bb_re_o[...] = f_re * b_re[...] - f_im * b_im[...]
    bb_im_o[...] = f_re * b_im[...] + f_im * b_re[...]
    ab_re, ab_im, _, _ = _s5_discretise(are_r[...], aim_r[...], ldt_r[...])
    ab_re_o[...] = ab_re
    ab_im_o[...] = ab_im


def _s5_params(a_re, a_im, log_dt, b_re, b_im):
    g, p = a_re.shape
    c = b_re.shape[-1]
    rep = lambda t: jnp.repeat(t, c, axis=-1)
    ldt_e = jnp.broadcast_to(log_dt[:, None], (g, p * c))
    ldt_r = jnp.broadcast_to(log_dt[:, None], (g, p)).reshape(1, g * p)
    flat = jax.ShapeDtypeStruct((g, p * c), F32)
    rowv = jax.ShapeDtypeStruct((1, g * p), F32)
    return pl.pallas_call(
        _s5_param_kernel,
        out_shape=[flat, flat, rowv, rowv],
        name="s5_params",
    )(rep(a_re), rep(a_im), ldt_e, b_re.reshape(g, p * c), b_im.reshape(g, p * c),
      a_re.reshape(1, g * p), a_im.reshape(1, g * p), ldt_r)


def _blockdiag_in(bb_re, bb_im):
    bb = jnp.stack([bb_re, bb_im]).reshape(2, SSM_BLOCKS, GROUPS_PER_BLOCK, SSM_STATE, SSM_GROUP)
    t = jnp.transpose(bb, (1, 2, 4, 0, 3))
    eye = jnp.eye(GROUPS_PER_BLOCK, dtype=bool)
    w = jnp.where(eye[None, :, None, None, :, None], t[:, :, :, :, None, :], 0.0)
    return w.reshape(SSM_BLOCKS, BLOCK_CH, 2 * BLOCK_ST).astype(BF16)


def _blockdiag_out(cm):
    t = jnp.transpose(cm.reshape(SSM_BLOCKS, GROUPS_PER_BLOCK, SSM_GROUP, SSM_STATE), (0, 1, 3, 2))
    eye = jnp.eye(GROUPS_PER_BLOCK, dtype=bool)
    w = jnp.where(eye[None, :, None, :, None], t[:, :, :, None, :], 0.0)
    return w.reshape(SSM_BLOCKS, BLOCK_ST, BLOCK_CH).astype(BF16)


S5_TILE = 512
S5_SEQS = 4


def _s5_prompt_kernel(u0, u1, u2, u3, wb_ref, wcre_ref, wcim_ref, abre_ref, abim_ref, d_ref,
                      y_ref, sre_ref, sim_ref, u_tm, bu, y_tm, st):
    i = pl.program_id(1)
    half = BLOCK_CH // 2

    @pl.when(i == 0)
    def _():
        st[...] = jnp.zeros_like(st)

    for b, u in enumerate((u0, u1, u2, u3)):
        u_tm[0, pl.ds(b, S5_TILE, stride=S5_SEQS), :] = u[:, :half]
        u_tm[1, pl.ds(b, S5_TILE, stride=S5_SEQS), :] = u[:, half:]
    u_all = jnp.concatenate([u_tm[0], u_tm[1]], axis=1)
    bu[...] = _dot(u_all.astype(BF16), wb_ref[...])
    a_re = abre_ref[...]
    a_im = abim_ref[...]

    first = lax.broadcasted_iota(I32, (2 * S5_SEQS, BLOCK_ST), 0) < S5_SEQS

    def body(t2, carry):
        s_re, s_im = carry
        rows = pl.ds(pl.multiple_of(t2 * 2 * S5_SEQS, 2 * S5_SEQS), 2 * S5_SEQS)
        x_re = bu[rows, :BLOCK_ST]
        x_im = bu[rows, BLOCK_ST:]
        p_re = a_re * s_re - a_im * s_im + x_re
        p_im = a_re * s_im + a_im * s_re + x_im
        r_re = pltpu.roll(p_re, S5_SEQS, 0)
        r_im = pltpu.roll(p_im, S5_SEQS, 0)
        q_re = a_re * r_re - a_im * r_im + x_re
        q_im = a_re * r_im + a_im * r_re + x_im
        bu[rows, :BLOCK_ST] = jnp.where(first, p_re, q_re)
        bu[rows, BLOCK_ST:] = jnp.where(first, p_im, q_im)
        return pltpu.roll(q_re, S5_SEQS, 0), pltpu.roll(q_im, S5_SEQS, 0)

    s_re, s_im = lax.fori_loop(0, S5_TILE // 2, body, (st[:, :BLOCK_ST], st[:, BLOCK_ST:]), unroll=2)
    st[:, :BLOCK_ST] = s_re
    st[:, BLOCK_ST:] = s_im
    y = (_dot(bu[:, :BLOCK_ST].astype(BF16), wcre_ref[...])
         - _dot(bu[:, BLOCK_ST:].astype(BF16), wcim_ref[...]) + d_ref[...] * u_all)
    y = _gelu(y)
    y_tm[0] = y[:, :half]
    y_tm[1] = y[:, half:]
    for b in range(S5_SEQS):
        y_ref[b, :, :half] = y_tm[0, pl.ds(b, S5_TILE, stride=S5_SEQS), :]
        y_ref[b, :, half:] = y_tm[1, pl.ds(b, S5_TILE, stride=S5_SEQS), :]

    @pl.when(i == pl.num_programs(1) - 1)
    def _():
        sre_ref[...] = s_re[:S5_SEQS, :]
        sim_ref[...] = s_im[:S5_SEQS, :]


def _s5_prompt(z, wb, wcre, wcim, ab_re, ab_im, d_row, seq_len):
    nt = seq_len // S5_TILE

    def u_spec(b):
        return pl.BlockSpec((S5_TILE, BLOCK_CH), lambda j, i, b=b: (b * nt + i, j))

    rows = S5_TILE * S5_SEQS
    return pl.pallas_call(
        _s5_prompt_kernel,
        grid=(SSM_BLOCKS, nt),
        in_specs=[u_spec(b) for b in range(S5_SEQS)] + [
            pl.BlockSpec((None, BLOCK_CH, 2 * BLOCK_ST), lambda j, i: (j, 0, 0)),
            pl.BlockSpec((None, BLOCK_ST, BLOCK_CH), lambda j, i: (j, 0, 0)),
            pl.BlockSpec((None, BLOCK_ST, BLOCK_CH), lambda j, i: (j, 0, 0)),
            pl.BlockSpec((1, BLOCK_ST), lambda j, i: (0, j)),
            pl.BlockSpec((1, BLOCK_ST), lambda j, i: (0, j)),
            pl.BlockSpec((1, BLOCK_CH), lambda j, i: (0, j)),
        ],
        out_specs=[
            pl.BlockSpec((S5_SEQS, S5_TILE, BLOCK_CH), lambda j, i: (0, i, j)),
            pl.BlockSpec((S5_SEQS, BLOCK_ST), lambda j, i: (0, j)),
            pl.BlockSpec((S5_SEQS, BLOCK_ST), lambda j, i: (0, j)),
        ],
        out_shape=[
            jax.ShapeDtypeStruct((S5_SEQS, seq_len, SSM_WIDTH), F32),
            jax.ShapeDtypeStruct((S5_SEQS, SSM_GROUPS * SSM_STATE), F32),
            jax.ShapeDtypeStruct((S5_SEQS, SSM_GROUPS * SSM_STATE), F32),
        ],
        scratch_shapes=[
            pltpu.VMEM((2, rows, 128), F32),
            pltpu.VMEM((rows, 2 * BLOCK_ST), F32),
            pltpu.VMEM((2, rows, 128), F32),
            pltpu.VMEM((2 * S5_SEQS, 2 * BLOCK_ST), F32),
        ],
        compiler_params=_cparams(("parallel", "arbitrary")),
        name="s5_prompt",
    )(z, z, z, z, wb, wcre, wcim, ab_re, ab_im, d_row)


def _s5_sample_kernel(u_ref, s0re_ref, s0im_ref, wb_ref, wcre_ref, wcim_ref, abre_ref, abim_ref,
                      d_ref, y_ref, sre_ref, sim_ref, u_sl, y_sl):
    half = BLOCK_CH // 2
    n_seq = s0re_ref.shape[0]
    u_sl[0] = u_ref[:, :half]
    u_sl[1] = u_ref[:, half:]
    a_re = abre_ref[...]
    a_im = abim_ref[...]
    s_re = s0re_ref[...]
    s_im = s0im_ref[...]
    for t in range(SAMPLE_SEQ):
        rows = pl.ds(t, n_seq, stride=SAMPLE_SEQ)
        u_t = jnp.concatenate([u_sl[0, rows, :], u_sl[1, rows, :]], axis=1)
        bu = _dot(u_t.astype(BF16), wb_ref[...])
        n_re = a_re * s_re - a_im * s_im + bu[:, :BLOCK_ST]
        n_im = a_re * s_im + a_im * s_re + bu[:, BLOCK_ST:]
        s_re, s_im = n_re, n_im
        y = (_dot(s_re.astype(BF16), wcre_ref[...]) - _dot(s_im.astype(BF16), wcim_ref[...])
             + d_ref[...] * u_t)
        y = _gelu(y)
        y_sl[0, rows, :] = y[:, :half]
        y_sl[1, rows, :] = y[:, half:]
    y_ref[:, :half] = y_sl[0]
    y_ref[:, half:] = y_sl[1]
    sre_ref[...] = s_re
    sim_ref[...] = s_im


def _s5_sample(z, s0_re, s0_im, wb, wcre, wcim, ab_re, ab_im, d_row, row0, n_rows):
    n_seq = s0_re.shape[0]
    st_spec = pl.BlockSpec((n_seq, BLOCK_ST), lambda j: (0, j))
    return pl.pallas_call(
        _s5_sample_kernel,
        grid=(SSM_BLOCKS,),
        in_specs=[
            pl.BlockSpec((n_rows, BLOCK_CH), lambda j: (row0 // n_rows, j)),
            st_spec, st_spec,
            pl.BlockSpec((None, BLOCK_CH, 2 * BLOCK_ST), lambda j: (j, 0, 0)),
            pl.BlockSpec((None, BLOCK_ST, BLOCK_CH), lambda j: (j, 0, 0)),
            pl.BlockSpec((None, BLOCK_ST, BLOCK_CH), lambda j: (j, 0, 0)),
            pl.BlockSpec((1, BLOCK_ST), lambda j: (0, j)),
            pl.BlockSpec((1, BLOCK_ST), lambda j: (0, j)),
            pl.BlockSpec((1, BLOCK_CH), lambda j: (0, j)),
        ],
        out_specs=[pl.BlockSpec((n_rows, BLOCK_CH), lambda j: (0, j)), st_spec, st_spec],
        out_shape=[
            jax.ShapeDtypeStruct((n_rows, SSM_WIDTH), F32),
            jax.ShapeDtypeStruct(s0_re.shape, F32),
            jax.ShapeDtypeStruct(s0_im.shape, F32),
        ],
        scratch_shapes=[pltpu.VMEM((2, n_rows, 128), F32), pltpu.VMEM((2, n_rows, 128), F32)],
        compiler_params=_cparams(("parallel",)),
        name="s5_sample",
    )(z, s0_re, s0_im, wb, wcre, wcim, ab_re, ab_im, d_row)


def _postmix_kernel(hmp_ref, hms_ref, ysp_ref, yss_ref, xp_ref, xs_ref,
                    wglu_ref, bglu_ref, wout_ref, gain_ref, wq_ref, keys_ref,
                    h1_ref, c_ref, sp_ref, ss_ref, *, prompt_tiles):
    i = pl.program_id(0)
    ys = _group_pick(ysp_ref, yss_ref, prompt_tiles)
    hm = _group_pick(hmp_ref, hms_ref, prompt_tiles)
    glu = ys * _sigmoid(_dot(ys.astype(BF16), wglu_ref[...]) + bglu_ref[...])
    mix = (_dot(hm.astype(BF16), wout_ref[:MLSTM_WIDTH, :])
           + _dot(glu.astype(BF16), wout_ref[MLSTM_WIDTH:, :]))
    h1 = _group_pick(xp_ref, xs_ref, prompt_tiles) + mix
    h1_ref[...] = h1
    c = _rmsnorm(h1, gain_ref[...]).astype(BF16)
    c_ref[...] = c
    qp = _dot(c, wq_ref[...])
    scores = []
    width = KEY_GROUP * PEER_HALF
    for g in range(2 * PEER_HEADS // KEY_GROUP):
        sg = _dot_nt(keys_ref[g], qp[:, g * width:(g + 1) * width].astype(BF16))
        scores += [sg[jj * PEER_KEYS:(jj + 1) * PEER_KEYS, :] for jj in range(KEY_GROUP)]

    @pl.when(i < prompt_tiles)
    def _():
        for j, s in enumerate(scores):
            sp_ref[j] = s

    @pl.when(i >= prompt_tiles)
    def _():
        for j, s in enumerate(scores):
            ss_ref[j] = s


KEY_GROUP = 4


def _blockdiag_keys(keys):
    ng = keys.shape[0] // KEY_GROUP
    k = keys.reshape(ng, KEY_GROUP, PEER_KEYS, PEER_HALF)
    eye = jnp.eye(KEY_GROUP, dtype=bool)
    w = jnp.where(eye[None, :, None, :, None], k[:, :, :, None, :], 0.0)
    return w.reshape(ng, KEY_GROUP * PEER_KEYS, KEY_GROUP * PEER_HALF).astype(BF16)


def _postmix(hm_p, hm_s, ys_p, ys_s, x_p, x_s, wglu, bglu, wout, gain, wq, keys, tm=256):
    n_p, n_s = x_p.shape[0], x_s.shape[0]
    t = n_p + n_s
    pt = n_p // tm
    nk = 2 * PEER_HEADS
    return pl.pallas_call(
        functools.partial(_postmix_kernel, prompt_tiles=pt),
        grid=(t // tm,),
        in_specs=_group_specs(tm, MLSTM_WIDTH, pt) + _group_specs(tm, SSM_WIDTH, pt)
        + _group_specs(tm, D_MODEL, pt) + [
            _resident((SSM_WIDTH, SSM_WIDTH), lambda i: (0, 0)),
            _resident((1, SSM_WIDTH), lambda i: (0, 0)),
            _resident((D_MODEL, D_MODEL), lambda i: (0, 0)),
            _resident((1, D_MODEL), lambda i: (0, 0)),
            _resident((D_MODEL, PEER_HEADS * 2 * PEER_HALF), lambda i: (0, 0)),
            _resident(keys.shape, lambda i: (0, 0, 0)),
        ],
        out_specs=[
            pl.BlockSpec((tm, D_MODEL), lambda i: (i, 0)),
            pl.BlockSpec((tm, D_MODEL), lambda i: (i, 0)),
            pl.BlockSpec((nk, PEER_KEYS, tm), lambda i: (0, 0, jnp.minimum(i, pt - 1))),
            pl.BlockSpec((nk, PEER_KEYS, tm), lambda i: (0, 0, jnp.maximum(i - pt, 0))),
        ],
        out_shape=[
            jax.ShapeDtypeStruct((t, D_MODEL), F32),
            jax.ShapeDtypeStruct((t, D_MODEL), BF16),
            jax.ShapeDtypeStruct((nk, PEER_KEYS, n_p), F32),
            jax.ShapeDtypeStruct((nk, PEER_KEYS, n_s), F32),
        ],
        compiler_params=_cparams(("arbitrary",)),
        name="postmix",
    )(hm_p, hm_s, ys_p, ys_s, x_p, x_s, wglu, bglu, wout, gain, wq, keys)


ID_NONE = 1 << 20
TOPK_SUB = 8


def _sort_network(n):
    pairs = []
    p = 1
    while p < n:
        k = p
        while k >= 1:
            for j in range(k % p, n - k, 2 * k):
                for i in range(min(k, n - j - k)):
                    if (i + j) // (2 * p) == (i + j + k) // (2 * p):
                        pairs.append((i + j, i + j + k))
            k //= 2
        p *= 2
    return pairs


_SORT16 = _sort_network(PEER_TOPK)


def _precedes(b, a):
    (vb, ib), (va, ia) = b, a
    return jnp.logical_or(vb > va, jnp.logical_and(vb == va, ib < ia))


def _first_of(a, b):
    sw = _precedes(b, a)
    return jnp.where(sw, b[0], a[0]), jnp.where(sw, b[1], a[1])


def _exchange(items, i, j):
    a, b = items[i], items[j]
    sw = _precedes(b, a)
    items[i] = (jnp.where(sw, b[0], a[0]), jnp.where(sw, b[1], a[1]))
    items[j] = (jnp.where(sw, a[0], b[0]), jnp.where(sw, a[1], b[1]))


def _sort16(items):
    items = list(items)
    for i, j in _SORT16:
        _exchange(items, i, j)
    return items


def _bitonic_merge16(items):
    items = list(items)
    d = PEER_TOPK // 2
    while d >= 1:
        for i in range(PEER_TOPK):
            if i & d == 0:
                _exchange(items, i, i + d)
        d //= 2
    return items


def _merge_top16(a, b):
    return _bitonic_merge16([_first_of(a[i], b[PEER_TOPK - 1 - i]) for i in range(PEER_TOPK)])


def _top16_of_keys(s_ref, half, shape):
    best = None
    for g in range(PEER_KEYS // PEER_TOPK):
        grp = _sort16([(s_ref[half, g * PEER_TOPK + k], jnp.full(shape, g * PEER_TOPK + k, I32))
                       for k in range(PEER_TOPK)])
        best = grp if best is None else _merge_top16(best, grp)
    return best


def _route_head(s_ref):
    shape = s_ref.shape[2:]
    top1 = _top16_of_keys(s_ref, 0, shape)
    top2 = _top16_of_keys(s_ref, 1, shape)

    def pair(i, j):
        return top1[i][0] + top2[j][0], jnp.full(shape, i * PEER_TOPK + j, I32)

    pad = (jnp.full(shape, NEG_INF, F32), jnp.full(shape, ID_NONE, I32))
    g0 = [pair(0, j) for j in range(16)]
    g1 = _bitonic_merge16([pair(1, j) for j in range(8)] + [pair(i, 0) for i in range(15, 7, -1)])
    g2 = _sort16([pair(i, j) for i in range(2, 7) for j in range(PEER_TOPK // (i + 1))])
    g3 = [pair(7, 0), pair(7, 1)] + [pad] * 14
    best = _merge_top16(_merge_top16(g0, g1), _merge_top16(g2, g3))

    mx = best[0][0]
    e1s, e2s, exps = [], [], []
    for k in range(PEER_TOPK):
        v, pid = best[k]
        a = lax.shift_right_logical(pid, 4)
        b = jnp.bitwise_and(pid, PEER_TOPK - 1)
        e1 = jnp.zeros(shape, I32)
        e2 = jnp.zeros(shape, I32)
        for r in range(PEER_TOPK):
            e1 = jnp.where(a == r, top1[r][1], e1)
            e2 = jnp.where(b == r, top2[r][1], e2)
        e1s.append(e1)
        e2s.append(e2)
        exps.append(jnp.exp(v - mx))
    total = exps[0]
    for k in range(1, PEER_TOPK):
        total = total + exps[k]
    return e1s, e2s, [e / total for e in exps]


def _topk_kernel(s_ref, e1_ref, e2_ref, g_ref):
    e1s, e2s, gs = _route_head(s_ref)
    for k in range(PEER_TOPK):
        e1_ref[k] = e1s[k]
        e2_ref[k] = e2s[k]
        g_ref[k] = gs[k]


def _topk(scores):
    t = scores.shape[-1]
    ng = t // 128
    sub = min(ng, TOPK_SUB)
    r = PEER_HEADS * PEER_TOPK
    o_spec = pl.BlockSpec((PEER_TOPK, sub, 128), lambda i, h: (h, i, 0))
    outs = pl.pallas_call(
        _topk_kernel,
        grid=(ng // sub, PEER_HEADS),
        in_specs=[pl.BlockSpec((2, PEER_KEYS, sub, 128), lambda i, h: (h, 0, i, 0))],
        out_specs=[o_spec, o_spec, o_spec],
        out_shape=[
            jax.ShapeDtypeStruct((r, ng, 128), I32),
            jax.ShapeDtypeStruct((r, ng, 128), I32),
            jax.ShapeDtypeStruct((r, ng, 128), F32),
        ],
        compiler_params=_cparams(("parallel", "parallel")),
        name="topk",
    )(scores.reshape(2 * PEER_HEADS, PEER_KEYS, ng, 128))
    return [o.reshape(r, t) for o in outs]


WB_GROUP = 16
WB_PITCH = 132


def _token_weights(e1_row, e2_row, g_row):
    sub = lax.broadcasted_iota(I32, (PEER_KEYS, PEER_KEYS), 0)
    onehot1 = jnp.where(sub == e1_row, 1.0, 0.0).astype(BF16)
    gated2 = jnp.where(sub == e2_row, 0.5 * g_row, 0.0).astype(BF16)
    return _dot_nt(onehot1, gated2)


def _wbuild_kernel(e1_ref, e2_ref, g_ref, w_ref, stage):
    for grp in range(e1_ref.shape[0] // WB_GROUP):
        base = (grp % 2) * WB_GROUP * WB_PITCH
        for tt in range(WB_GROUP):
            t = grp * WB_GROUP + tt
            stage[base + tt * WB_PITCH:base + tt * WB_PITCH + PEER_KEYS, :] = _token_weights(
                e1_ref[t:t + 1, :], e2_ref[t:t + 1, :], g_ref[t:t + 1, :])
        for e in range(PEER_KEYS):
            blk = stage[pl.ds(base + e, WB_GROUP, stride=WB_PITCH), :]
            w_ref[e, grp * WB_GROUP:(grp + 1) * WB_GROUP, :] = blk.astype(BF16)


def _wbuild(e1, e2, g, tw=256):
    t = e1.shape[0]
    i_spec = pl.BlockSpec((tw, PEER_KEYS), lambda i: (i, 0))
    return pl.pallas_call(
        _wbuild_kernel,
        grid=(t // tw,),
        in_specs=[i_spec, i_spec, i_spec],
        out_specs=pl.BlockSpec((PEER_KEYS, tw, PEER_KEYS), lambda i: (0, i, 0)),
        out_shape=jax.ShapeDtypeStruct((PEER_KEYS, t, PEER_KEYS), BF16),
        scratch_shapes=[pltpu.VMEM((2 * WB_GROUP * WB_PITCH, PEER_KEYS), F32)],
        compiler_params=_cparams(("parallel",)),
        name="wbuild",
    )(e1, e2, g)


PEER_EB = 512


def _peer_kernel(c_ref, u_ref, v_ref, w_ref, o_ref, s_s):
    j = pl.program_id(1)
    last = pl.num_programs(1) - 1

    def gated():
        parts = []
        for k in range(PEER_EB // PEER_KEYS):
            x = s_s[:, k * PEER_KEYS:(k + 1) * PEER_KEYS]
            t = jnp.tanh(x * (GELU_C1 + GELU_C2 * (x * x)))
            parts.append((x * w_ref[k].astype(F32)) * (1.0 + t))
        return jnp.concatenate(parts, axis=1).astype(BF16)

    def scores():
        return _dot_nt(c_ref[...], u_ref[...].astype(BF16))

    @pl.when(j == 0)
    def _():
        o_ref[...] = jnp.zeros_like(o_ref)
        s_s[...] = scores()

    @pl.when(jnp.logical_and(j > 0, j < last))
    def _():
        wact = gated()
        s_s[...] = scores()
        o_ref[...] += _dot(wact, v_ref[...].astype(BF16))

    @pl.when(j == last)
    def _():
        o_ref[...] += _dot(gated(), v_ref[...].astype(BF16))


def _peer(c, u, v, w3, n_tiles=8):
    t = c.shape[0]
    tm = t // n_tiles
    nb = PEER_EXPERTS // PEER_EB
    return pl.pallas_call(
        _peer_kernel,
        grid=(n_tiles, nb + 1),
        in_specs=[
            pl.BlockSpec((tm, D_MODEL), lambda i, j: (i, 0), pipeline_mode=pl.Buffered(1)),
            pl.BlockSpec((PEER_EB, D_MODEL), lambda i, j: (jnp.minimum(j, nb - 1), 0)),
            pl.BlockSpec((PEER_EB, D_MODEL), lambda i, j: (jnp.maximum(j - 1, 0), 0)),
            pl.BlockSpec((PEER_EB // PEER_KEYS, tm, PEER_KEYS),
                         lambda i, j: (jnp.maximum(j - 1, 0), i, 0)),
        ],
        out_specs=pl.BlockSpec((tm, D_MODEL), lambda i, j: (i, 0)),
        out_shape=jax.ShapeDtypeStruct((t, D_MODEL), F32),
        scratch_shapes=[pltpu.VMEM((tm, PEER_EB), F32)],
        compiler_params=_cparams(("parallel", "arbitrary")),
        name="peer",
    )(c, u, v, w3)


def _tail_kernel(h1_ref, peer_ref, pp_ref, ps_ref, gple_ref, wgate_ref, wproj_ref, gfin_ref,
                 yp_ref, ys_ref, *, prompt_tiles):
    i = pl.program_id(0)
    h2 = h1_ref[...] + peer_ref[...]
    gate = _sigmoid(_dot(_rmsnorm(h2, gple_ref[...]).astype(BF16), wgate_ref[...]))
    e = _dot(_group_pick(pp_ref, ps_ref, prompt_tiles).astype(BF16), wproj_ref[...])
    y = _rmsnorm(h2 + e * gate, gfin_ref[...])

    @pl.when(i < prompt_tiles)
    def _():
        yp_ref[...] = y

    @pl.when(i >= prompt_tiles)
    def _():
        ys_ref[...] = y


def _tail(h1, peer, p_p, p_s, gple, wgate, wproj, gfin, tm=512):
    t = h1.shape[0]
    n_prompt = p_p.shape[0]
    pt = n_prompt // tm
    ple = p_p.shape[1]
    return pl.pallas_call(
        functools.partial(_tail_kernel, prompt_tiles=pt),
        grid=(t // tm,),
        in_specs=[
            pl.BlockSpec((tm, D_MODEL), lambda i: (i, 0)),
            pl.BlockSpec((tm, D_MODEL), lambda i: (i, 0)),
        ] + _group_specs(tm, ple, pt) + [
            _resident((1, D_MODEL), lambda i: (0, 0)),
            _resident((D_MODEL, D_MODEL), lambda i: (0, 0)),
            _resident((ple, D_MODEL), lambda i: (0, 0)),
            _resident((1, D_MODEL), lambda i: (0, 0)),
        ],
        out_specs=[
            pl.BlockSpec((tm, D_MODEL), lambda i: (jnp.minimum(i, pt - 1), 0)),
            pl.BlockSpec((tm, D_MODEL), lambda i: (jnp.maximum(i - pt, 0), 0)),
        ],
        out_shape=[
            jax.ShapeDtypeStruct((n_prompt, D_MODEL), F32),
            jax.ShapeDtypeStruct((t - n_prompt, D_MODEL), F32),
        ],
        compiler_params=_cparams(("arbitrary",)),
        name="tail",
    )(h1, peer, p_p, p_s, gple, wgate, wproj, gfin)


def kernel(x_prompt, x_sample, state_mlstm_C, state_mlstm_n, state_mlstm_m, state_ssm_re, state_ssm_im, p_prompt, p_sample, norm_mix, w_in, b_igate, b_fgate, mlstm_norm, ssm_A_re, ssm_A_im, ssm_B_re, ssm_B_im, ssm_C_re, ssm_C_im, ssm_D, ssm_log_dt, w_glu, b_glu, w_out, norm_ffn, peer_w_q, peer_keys, peer_u, peer_v, norm_ple, w_ple_gate, w_ple_proj, norm_final):
    n_pseq, p_len, _ = x_prompt.shape
    n_sseq, s_len, _ = x_sample.shape
    assert s_len == SAMPLE_SEQ and n_pseq == S5_SEQS and w_in.shape[0] == 1
    n_prompt = n_pseq * p_len
    n_sample = n_sseq * s_len
    row = lambda t: t.reshape(1, -1)

    x_p = x_prompt.reshape(n_prompt, D_MODEL)
    x_s = x_sample.reshape(n_sample, D_MODEL)

    w_bf = w_in[0].astype(BF16)
    n_gate = 2 * HEADS
    w_tail = jnp.concatenate(
        [w_bf[:, Z_U + n_gate:], w_bf[:, Z_U:Z_U + n_gate],
         jnp.zeros((D_MODEL, 128 - n_gate), BF16)], axis=1)
    gbias = jnp.concatenate([b_igate[0], b_fgate[0], jnp.zeros((128 - n_gate,), F32)]).reshape(1, 128)
    z, zt = _in_proj(x_p, x_s, row(norm_mix[0]), w_bf, w_tail, gbias)

    gain_m = row(mlstm_norm[0])
    hm_p, c_p, n_p, m_p = _mlstm_prompt(z, zt, gain_m, n_pseq, p_len)
    nbh = n_sseq * HEADS
    hm_s, c_s, n_s, m_s = _mlstm_sample(
        z, zt, gain_m,
        state_mlstm_C[0].reshape(nbh, HEAD_DIM, HEAD_DIM),
        state_mlstm_n[0].reshape(nbh, 1, HEAD_DIM),
        jnp.broadcast_to(state_mlstm_m[0].reshape(nbh, 1, 1), (nbh, 1, 128)),
        n_prompt, n_sample)

    bb_re, bb_im, ab_re, ab_im = _s5_params(ssm_A_re[0], ssm_A_im[0], ssm_log_dt[0], ssm_B_re[0], ssm_B_im[0])
    wb = _blockdiag_in(bb_re, bb_im)
    wcre = _blockdiag_out(ssm_C_re[0])
    wcim = _blockdiag_out(ssm_C_im[0])
    d_row = row(ssm_D[0])
    ys_p, sre_p, sim_p = _s5_prompt(zt, wb, wcre, wcim, ab_re, ab_im, d_row, p_len)
    n_st = SSM_GROUPS * SSM_STATE
    ys_s, sre_s, sim_s = _s5_sample(
        zt, state_ssm_re[0].reshape(n_sseq, n_st), state_ssm_im[0].reshape(n_sseq, n_st),
        wb, wcre, wcim, ab_re, ab_im, d_row, n_prompt, n_sample)

    keys = _blockdiag_keys(peer_keys[0].reshape(2 * PEER_HEADS, PEER_KEYS, PEER_HALF))
    h1, c, scores_p, scores_s = _postmix(
        hm_p, hm_s, ys_p.reshape(n_prompt, SSM_WIDTH), ys_s, x_p, x_s,
        w_glu[0].astype(BF16), row(b_glu[0]), w_out[0].astype(BF16),
        row(norm_ffn[0]), peer_w_q[0].astype(BF16), keys)
    routing = [jnp.concatenate([rp, rs], axis=1).T for rp, rs in zip(_topk(scores_p), _topk(scores_s))]
    w3 = _wbuild(*routing)
    peer = _peer(c, peer_u[0], peer_v[0], w3)

    y_p, y_s = _tail(h1, peer, p_prompt[0].reshape(n_prompt, -1), p_sample[0].reshape(n_sample, -1),
                     row(norm_ple[0]), w_ple_gate[0].astype(BF16), w_ple_proj[0].astype(BF16),
                     row(norm_final))

    st_shape = (1, -1, SSM_GROUPS, SSM_STATE)
    return (
        y_p.reshape(x_prompt.shape), y_s.reshape(x_sample.shape),
        c_p.reshape(1, n_pseq, HEADS, HEAD_DIM, HEAD_DIM), n_p.reshape(1, n_pseq, HEADS, HEAD_DIM),
        m_p[:, 0, 0].reshape(1, n_pseq, HEADS),
        sre_p.reshape(st_shape), sim_p.reshape(st_shape),
        c_s.reshape(1, n_sseq, HEADS, HEAD_DIM, HEAD_DIM), n_s.reshape(1, n_sseq, HEADS, HEAD_DIM),
        m_s[:, 0, 0].reshape(1, n_sseq, HEADS),
        sre_s.reshape(st_shape), sim_s.reshape(st_shape),
    )
```

```python
import functools

import jax
import jax.numpy as jnp
from jax import lax
from jax.experimental import pallas as pl
from jax.experimental.pallas import tpu as pltpu

F32 = jnp.float32
BF16 = jnp.bfloat16
I32 = jnp.int32

EPS = 1e-6
D_MODEL = 2048
HEADS = 4
HEAD_DIM = 256
MLSTM_WIDTH = HEADS * HEAD_DIM
SSM_WIDTH = 1024
SSM_GROUPS = 64
SSM_STATE = 64
SSM_GROUP = 16
GROUPS_PER_BLOCK = 16
SSM_BLOCKS = SSM_GROUPS // GROUPS_PER_BLOCK
BLOCK_CH = GROUPS_PER_BLOCK * SSM_GROUP
BLOCK_ST = GROUPS_PER_BLOCK * SSM_STATE
PEER_HEADS = 8
PEER_KEYS = 128
PEER_TOPK = 16
PEER_HALF = 64
PEER_EXPERTS = PEER_KEYS * PEER_KEYS

Z_Q, Z_K, Z_V, Z_O, Z_U = 0, 1024, 2048, 3072, 4096
ZT_G = 1024
ZT_COLS = ZT_G + 128
IN_PROJ_COLS = 2048

MLSTM_CHUNK = 256
K_SCALE = HEAD_DIM ** -0.5
NEG_INF = float("-inf")

VMEM_LIMIT = 56 * 1024 * 1024


def _cparams(semantics):
    return pltpu.CompilerParams(dimension_semantics=semantics, vmem_limit_bytes=VMEM_LIMIT)


def _resident(shape, index_map):
    return pl.BlockSpec(shape, index_map, pipeline_mode=pl.Buffered(1))


def _sigmoid(x):
    return 1.0 / (1.0 + jnp.exp(-x))


GELU_C1 = 0.7978845608028654
GELU_C2 = GELU_C1 * 0.044715


def _gelu(x):
    return x * (0.5 * (1.0 + jnp.tanh(GELU_C1 * (x + 0.044715 * (x * x * x)))))


def _rmsnorm(x, gain):
    return x * lax.rsqrt(jnp.mean(x * x, axis=-1, keepdims=True) + EPS) * gain


def _dot(a, b):
    return jnp.dot(a, b, preferred_element_type=F32)


def _dot_nt(a, b):
    return lax.dot_general(a, b, (((1,), (1,)), ((), ())), preferred_element_type=F32)


def _dot_tn(a, b):
    return lax.dot_general(a, b, (((0,), (0,)), ((), ())), preferred_element_type=F32)


def _group_specs(tm, width, prompt_tiles):
    return [
        pl.BlockSpec((tm, width), lambda i: (jnp.minimum(i, prompt_tiles - 1), 0)),
        pl.BlockSpec((tm, width), lambda i: (jnp.maximum(i - prompt_tiles, 0), 0)),
    ]


def _group_pick(prompt_ref, sample_ref, prompt_tiles):
    return jnp.where(pl.program_id(0) < prompt_tiles, prompt_ref[...], sample_ref[...])


def _in_proj_kernel(xp_ref, xs_ref, gain_ref, wm_ref, wt_ref, gbias_ref, zm_ref, zt_ref, a_s,
                    *, prompt_tiles, main_steps):
    j = pl.program_id(1)

    @pl.when(j == 0)
    def _():
        a_s[...] = _rmsnorm(_group_pick(xp_ref, xs_ref, prompt_tiles), gain_ref[...]).astype(BF16)

    for step in range(main_steps):
        @pl.when(j == step)
        def _(base=step * IN_PROJ_COLS):
            for c0 in range(0, IN_PROJ_COLS, 512):
                zm_ref[:, c0:c0 + 512] = _dot(a_s[...], wm_ref[:, base + c0:base + c0 + 512])

    @pl.when(j == main_steps)
    def _():
        a = a_s[...]
        for c0 in range(0, ZT_G, 512):
            zt_ref[:, c0:c0 + 512] = _dot(a, wt_ref[:, c0:c0 + 512])
        gz = _dot(a, wt_ref[:, ZT_G:]) + gbias_ref[...]
        lane = lax.broadcasted_iota(I32, gz.shape, 1)
        log_f = jnp.minimum(gz, 0.0) - jnp.log(1.0 + jnp.exp(-jnp.abs(gz)))
        zt_ref[:, ZT_G:] = jnp.where(lane < HEADS, gz, log_f)


def _in_proj(x_p, x_s, gain, w_bf, w_tail, gbias, tm=512):
    t = x_p.shape[0] + x_s.shape[0]
    pt = x_p.shape[0] // tm
    ms = Z_U // IN_PROJ_COLS
    return pl.pallas_call(
        functools.partial(_in_proj_kernel, prompt_tiles=pt, main_steps=ms),
        grid=(t // tm, ms + 1),
        in_specs=[
            pl.BlockSpec((tm, D_MODEL), lambda i, j: (jnp.minimum(i, pt - 1), 0)),
            pl.BlockSpec((tm, D_MODEL), lambda i, j: (jnp.maximum(i - pt, 0), 0),
                         pipeline_mode=pl.Buffered(1)),
            _resident((1, D_MODEL), lambda i, j: (0, 0)),
            _resident((D_MODEL, Z_U), lambda i, j: (0, 0)),
            _resident((D_MODEL, ZT_COLS), lambda i, j: (0, 0)),
            _resident((1, 128), lambda i, j: (0, 0)),
        ],
        out_specs=[
            pl.BlockSpec((tm, IN_PROJ_COLS), lambda i, j: (i, jnp.minimum(j, ms - 1))),
            pl.BlockSpec((tm, ZT_COLS), lambda i, j: (i, 0)),
        ],
        out_shape=[
            jax.ShapeDtypeStruct((t, Z_U), F32),
            jax.ShapeDtypeStruct((t, ZT_COLS), F32),
        ],
        scratch_shapes=[pltpu.VMEM((tm, D_MODEL), BF16)],
        compiler_params=_cparams(("arbitrary", "arbitrary")),
        name="in_proj",
    )(x_p, x_s, gain, w_bf, w_tail, gbias)


def _mlstm_chunks(items):
    L = items[0][0].shape[0]
    row = lax.broadcasted_iota(I32, (L, L), 0)
    col = lax.broadcasted_iota(I32, (L, L), 1)
    diag = row == col
    causal = col <= row

    gates = []
    for (_, _, _, ig_col, lf_col, _, _, m_state) in items:
        lf_row = jnp.sum(jnp.where(diag, lf_col, 0.0), axis=0, keepdims=True)
        ig_row = jnp.sum(jnp.where(diag, ig_col, 0.0), axis=0, keepdims=True)
        b_col = jnp.sum(jnp.where(causal, lf_row, 0.0), axis=1, keepdims=True)
        b_row = jnp.sum(jnp.where(row <= col, lf_col, 0.0), axis=0, keepdims=True)
        r_row = ig_row - b_row
        r_col = ig_col - b_col
        run_max = jnp.max(jnp.where(causal, r_row, NEG_INF), axis=1, keepdims=True)
        m_run = jnp.maximum(m_state, run_max)
        w_intra = jnp.exp(jnp.where(causal, r_row - m_run, NEG_INF))
        w_inter = jnp.exp(m_state - m_run)
        gates.append((b_col, r_col, m_run, w_intra, w_inter))

    products = []
    for (q, k, v, _, _, c_get, _, _) in items:
        qb = q.astype(BF16)
        vb = v.astype(BF16)
        products.append((_dot_nt(qb, k.astype(BF16)), _dot(qb, c_get().astype(BF16)), vb))

    heads = []
    for (q, _, _, _, _, _, n_state, _), (b_col, _, m_run, w_intra, w_inter), (qk, qc, vb) in zip(
            items, gates, products):
        s = qk * w_intra
        num = w_inter * qc + _dot(s.astype(BF16), vb)
        qn = jnp.sum(q * n_state, axis=1, keepdims=True)
        den = w_inter * qn + jnp.sum(s, axis=1, keepdims=True)
        heads.append(num / jnp.maximum(jnp.abs(den), jnp.exp(-(b_col + m_run))))

    out = []
    for (_, k, _, _, _, c_get, n_state, m_state), (b_col, r_col, m_run, _, _), (_, _, vb), hb in zip(
            items, gates, products, heads):
        m_last = m_run[L - 1:L, :]
        m_new = b_col[L - 1:L, :] + m_last
        kw = k * jnp.exp(r_col - m_last)
        sc = jnp.exp(m_state - m_last)
        c_new = sc * c_get() + _dot_tn(kw.astype(BF16), vb)
        n_new = sc * n_state + jnp.sum(kw, axis=0, keepdims=True)
        out.append((hb, c_new, n_new, m_new))
    return out


def _head_out(hb, gain, o_pre):
    return hb * lax.rsqrt(jnp.mean(hb * hb, axis=-1, keepdims=True) + EPS) * gain * _sigmoid(o_pre)


def _mlstm_prompt_kernel(q_ref, k_ref, v_ref, o_ref, g_ref, gain_ref,
                         hm_ref, c_out_ref, n_out_ref, m_out_ref, c_s, n_s, m_s):
    h0 = pl.program_id(1) * PROMPT_HEADS
    c_s[...] = jnp.zeros_like(c_s)
    n_s[...] = jnp.zeros_like(n_s)
    m_s[...] = jnp.zeros_like(m_s)

    def body(ci, carry):
        rows = pl.ds(pl.multiple_of(ci * MLSTM_CHUNK, MLSTM_CHUNK), MLSTM_CHUNK)
        gates = g_ref[rows, :]
        lane = lax.broadcasted_iota(I32, gates.shape, 1)
        items = []
        for hh in range(PROMPT_HEADS):
            cs = slice(hh * HEAD_DIM, (hh + 1) * HEAD_DIM)
            ig_col = jnp.sum(jnp.where(lane == h0 + hh, gates, 0.0), axis=1, keepdims=True)
            lf_col = jnp.sum(jnp.where(lane == h0 + hh + HEADS, gates, 0.0), axis=1, keepdims=True)
            items.append((q_ref[rows, cs], k_ref[rows, cs] * K_SCALE, v_ref[rows, cs], ig_col, lf_col,
                          functools.partial(lambda r, i: r[i], c_s, hh), n_s[hh], m_s[hh][:, 0:1]))
        for hh, (hb, c_new, n_new, m_new) in enumerate(_mlstm_chunks(items)):
            cs = slice(hh * HEAD_DIM, (hh + 1) * HEAD_DIM)
            c_s[hh] = c_new
            n_s[hh] = n_new
            m_s[hh] = jnp.broadcast_to(m_new, (1, 128))
            hm_ref[rows, cs] = _head_out(hb, gain_ref[:, cs], o_ref[rows, cs])
        return carry

    lax.fori_loop(0, q_ref.shape[0] // MLSTM_CHUNK, body, 0)
    c_out_ref[...] = c_s[...]
    n_out_ref[...] = n_s[...]
    m_out_ref[...] = m_s[...]


PROMPT_HEADS = 2


def _mlstm_prompt(z, zt, gain, n_seq, seq_len):
    width = PROMPT_HEADS * HEAD_DIM
    hsteps = HEADS // PROMPT_HEADS

    def sec(off):
        return pl.BlockSpec((seq_len, width), lambda b, h, o=off // width: (b, o + h))

    def state(*tail):
        return pl.BlockSpec((PROMPT_HEADS,) + tail, lambda b, h: (b * hsteps + h,) + (0,) * len(tail))

    nbh = n_seq * HEADS
    return pl.pallas_call(
        _mlstm_prompt_kernel,
        grid=(n_seq, hsteps),
        in_specs=[
            sec(Z_Q), sec(Z_K), sec(Z_V), sec(Z_O),
            pl.BlockSpec((seq_len, 128), lambda b, h: (b, ZT_G // 128)),
            pl.BlockSpec((1, width), lambda b, h: (0, h)),
        ],
        out_specs=[
            pl.BlockSpec((seq_len, width), lambda b, h: (b, h)),
            state(HEAD_DIM, HEAD_DIM), state(1, HEAD_DIM), state(1, 128),
        ],
        out_shape=[
            jax.ShapeDtypeStruct((n_seq * seq_len, MLSTM_WIDTH), F32),
            jax.ShapeDtypeStruct((nbh, HEAD_DIM, HEAD_DIM), F32),
            jax.ShapeDtypeStruct((nbh, 1, HEAD_DIM), F32),
            jax.ShapeDtypeStruct((nbh, 1, 128), F32),
        ],
        scratch_shapes=[
            pltpu.VMEM((PROMPT_HEADS, HEAD_DIM, HEAD_DIM), F32),
            pltpu.VMEM((PROMPT_HEADS, 1, HEAD_DIM), F32),
            pltpu.VMEM((PROMPT_HEADS, 1, 128), F32),
        ],
        compiler_params=_cparams(("parallel", "parallel")),
        name="mlstm_prompt",
    )(z, z, z, z, zt, gain)


SAMPLE_SEQ = 4
SAMPLE_ROWS = 16


def _mlstm_sample_kernel(z_ref, g_ref, gain_ref, c_in_ref, n_in_ref, m_in_ref,
                         hm_ref, c_out_ref, n_out_ref, m_out_ref):
    row = lax.broadcasted_iota(I32, (8, 1), 0)
    for grp in range(SAMPLE_ROWS // 8):
        r0 = grp * 8
        gates = g_ref[r0:r0 + 8, :]
        items, where = [], []
        for half in range(2):
            lo = half * SAMPLE_SEQ
            live = jnp.logical_and(row >= lo, row < lo + SAMPLE_SEQ)
            for h in range(HEADS):
                j = (grp * 2 + half) * HEADS + h
                q = z_ref[r0:r0 + 8, Z_Q + h * HEAD_DIM:Z_Q + (h + 1) * HEAD_DIM]
                k = z_ref[r0:r0 + 8, Z_K + h * HEAD_DIM:Z_K + (h + 1) * HEAD_DIM]
                v = z_ref[r0:r0 + 8, Z_V + h * HEAD_DIM:Z_V + (h + 1) * HEAD_DIM]
                k = jnp.where(live, k * K_SCALE, 0.0)
                v = jnp.where(live, v, 0.0)
                ig_col = jnp.where(live, gates[:, h:h + 1], NEG_INF)
                lf_col = jnp.where(live, gates[:, HEADS + h:HEADS + h + 1], 0.0)
                items.append((q, k, v, ig_col, lf_col, functools.partial(lambda r, i: r[i], c_in_ref, j),
                              n_in_ref[j], m_in_ref[j][:, 0:1]))
                where.append((j, lo, h))
        for (j, lo, h), (hb, c_new, n_new, m_new) in zip(where, _mlstm_chunks(items)):
            cs = slice(h * HEAD_DIM, (h + 1) * HEAD_DIM)
            o = z_ref[r0:r0 + 8, Z_O + h * HEAD_DIM:Z_O + (h + 1) * HEAD_DIM]
            out = _head_out(hb, gain_ref[:, cs], o)
            hm_ref[r0 + lo:r0 + lo + SAMPLE_SEQ, cs] = out[lo:lo + SAMPLE_SEQ, :]
            c_out_ref[j] = c_new
            n_out_ref[j] = n_new
            m_out_ref[j] = jnp.broadcast_to(m_new, (1, 128))


def _mlstm_sample(z, zt, gain, c0, n0, m0, row0, n_rows):
    nbh = c0.shape[0]
    per = SAMPLE_ROWS // SAMPLE_SEQ * HEADS
    rb0 = row0 // SAMPLE_ROWS
    return pl.pallas_call(
        _mlstm_sample_kernel,
        grid=(n_rows // SAMPLE_ROWS,),
        in_specs=[
            pl.BlockSpec((SAMPLE_ROWS, Z_U), lambda i: (rb0 + i, 0)),
            pl.BlockSpec((SAMPLE_ROWS, 128), lambda i: (rb0 + i, ZT_G // 128)),
            pl.BlockSpec((1, MLSTM_WIDTH), lambda i: (0, 0)),
            pl.BlockSpec((per, HEAD_DIM, HEAD_DIM), lambda i: (i, 0, 0)),
            pl.BlockSpec((per, 1, HEAD_DIM), lambda i: (i, 0, 0)),
            pl.BlockSpec((per, 1, 128), lambda i: (i, 0, 0)),
        ],
        out_specs=[
            pl.BlockSpec((SAMPLE_ROWS, MLSTM_WIDTH), lambda i: (i, 0)),
            pl.BlockSpec((per, HEAD_DIM, HEAD_DIM), lambda i: (i, 0, 0)),
            pl.BlockSpec((per, 1, HEAD_DIM), lambda i: (i, 0, 0)),
            pl.BlockSpec((per, 1, 128), lambda i: (i, 0, 0)),
        ],
        out_shape=[
            jax.ShapeDtypeStruct((n_rows, MLSTM_WIDTH), F32),
            jax.ShapeDtypeStruct((nbh, HEAD_DIM, HEAD_DIM), F32),
            jax.ShapeDtypeStruct((nbh, 1, HEAD_DIM), F32),
            jax.ShapeDtypeStruct((nbh, 1, 128), F32),
        ],
        compiler_params=_cparams(("parallel",)),
        name="mlstm_sample",
    )(z, zt, gain, c0, n0, m0)


def _s5_discretise(a_re, a_im, log_dt):
    dt = jnp.exp(log_dt)
    mag = jnp.exp(dt * a_re)
    ab_re = mag * jnp.cos(dt * a_im)
    ab_im = mag * jnp.sin(dt * a_im)
    den = a_re * a_re + a_im * a_im
    xr = ab_re - 1.0
    f_re = (xr * a_re + ab_im * a_im) / den
    f_im = (ab_im * a_re - xr * a_im) / den
    return ab_re, ab_im, f_re, f_im


def _s5_param_kernel(are_e, aim_e, ldt_e, b_re, b_im, are_r, aim_r, ldt_r,
                     bb_re_o, bb_im_o, ab_re_o, ab_im_o):
    _, _, f_re, f_im = _s5_discretise(are_e[...], aim_e[...], ldt_e[...])
    bb_re_o[...] = f_re * b_re[...] - f_im * b_im[...]
    bb_im_o[...] = f_re * b_im[...] + f_im * b_re[...]
    ab_re, ab_im, _, _ = _s5_discretise(are_r[...], aim_r[...], ldt_r[...])
    ab_re_o[...] = ab_re
    ab_im_o[...] = ab_im


def _s5_params(a_re, a_im, log_dt, b_re, b_im):
    g, p = a_re.shape
    c = b_re.shape[-1]
    rep = lambda t: jnp.repeat(t, c, axis=-1)
    ldt_e = jnp.broadcast_to(log_dt[:, None], (g, p * c))
    ldt_r = jnp.broadcast_to(log_dt[:, None], (g, p)).reshape(1, g * p)
    flat = jax.ShapeDtypeStruct((g, p * c), F32)
    rowv = jax.ShapeDtypeStruct((1, g * p), F32)
    return pl.pallas_call(
        _s5_param_kernel,
        out_shape=[flat, flat, rowv, rowv],
        name="s5_params",
    )(rep(a_re), rep(a_im), ldt_e, b_re.reshape(g, p * c), b_im.reshape(g, p * c),
      a_re.reshape(1, g * p), a_im.reshape(1, g * p), ldt_r)


def _blockdiag_in(bb_re, bb_im):
    bb = jnp.stack([bb_re, bb_im]).reshape(2, SSM_BLOCKS, GROUPS_PER_BLOCK, SSM_STATE, SSM_GROUP)
    t = jnp.transpose(bb, (1, 2, 4, 0, 3))
    eye = jnp.eye(GROUPS_PER_BLOCK, dtype=bool)
    w = jnp.where(eye[None, :, None, None, :, None], t[:, :, :, :, None, :], 0.0)
    return w.reshape(SSM_BLOCKS, BLOCK_CH, 2 * BLOCK_ST).astype(BF16)


def _blockdiag_out(cm):
    t = jnp.transpose(cm.reshape(SSM_BLOCKS, GROUPS_PER_BLOCK, SSM_GROUP, SSM_STATE), (0, 1, 3, 2))
    eye = jnp.eye(GROUPS_PER_BLOCK, dtype=bool)
    w = jnp.where(eye[None, :, None, :, None], t[:, :, :, None, :], 0.0)
    return w.reshape(SSM_BLOCKS, BLOCK_ST, BLOCK_CH).astype(BF16)


S5_TILE = 512
S5_SEQS = 4


def _s5_prompt_kernel(u0, u1, u2, u3, wb_ref, wcre_ref, wcim_ref, abre_ref, abim_ref, d_ref,
                      y_ref, sre_ref, sim_ref, u_tm, bu, y_tm, st):
    i = pl.program_id(1)
    half = BLOCK_CH // 2

    @pl.when(i == 0)
    def _():
        st[...] = jnp.zeros_like(st)

    for b, u in enumerate((u0, u1, u2, u3)):
        u_tm[0, pl.ds(b, S5_TILE, stride=S5_SEQS), :] = u[:, :half]
        u_tm[1, pl.ds(b, S5_TILE, stride=S5_SEQS), :] = u[:, half:]
    u_all = jnp.concatenate([u_tm[0], u_tm[1]], axis=1)
    bu[...] = _dot(u_all.astype(BF16), wb_ref[...])
    a_re = abre_ref[...]
    a_im = abim_ref[...]

    first = lax.broadcasted_iota(I32, (2 * S5_SEQS, BLOCK_ST), 0) < S5_SEQS

    def body(t2, carry):
        s_re, s_im = carry
        rows = pl.ds(pl.multiple_of(t2 * 2 * S5_SEQS, 2 * S5_SEQS), 2 * S5_SEQS)
        x_re = bu[rows, :BLOCK_ST]
        x_im = bu[rows, BLOCK_ST:]
        p_re = a_re * s_re - a_im * s_im + x_re
        p_im = a_re * s_im + a_im * s_re + x_im
        r_re = pltpu.roll(p_re, S5_SEQS, 0)
        r_im = pltpu.roll(p_im, S5_SEQS, 0)
        q_re = a_re * r_re - a_im * r_im + x_re
        q_im = a_re * r_im + a_im * r_re + x_im
        bu[rows, :BLOCK_ST] = jnp.where(first, p_re, q_re)
        bu[rows, BLOCK_ST:] = jnp.where(first, p_im, q_im)
        return pltpu.roll(q_re, S5_SEQS, 0), pltpu.roll(q_im, S5_SEQS, 0)

    s_re, s_im = lax.fori_loop(0, S5_TILE // 2, body, (st[:, :BLOCK_ST], st[:, BLOCK_ST:]), unroll=2)
    st[:, :BLOCK_ST] = s_re
    st[:, BLOCK_ST:] = s_im
    y = (_dot(bu[:, :BLOCK_ST].astype(BF16), wcre_ref[...])
         - _dot(bu[:, BLOCK_ST:].astype(BF16), wcim_ref[...]) + d_ref[...] * u_all)
    y = _gelu(y)
    y_tm[0] = y[:, :half]
    y_tm[1] = y[:, half:]
    for b in range(S5_SEQS):
        y_ref[b, :, :half] = y_tm[0, pl.ds(b, S5_TILE, stride=S5_SEQS), :]
        y_ref[b, :, half:] = y_tm[1, pl.ds(b, S5_TILE, stride=S5_SEQS), :]

    @pl.when(i == pl.num_programs(1) - 1)
    def _():
        sre_ref[...] = s_re[:S5_SEQS, :]
        sim_ref[...] = s_im[:S5_SEQS, :]


def _s5_prompt(z, wb, wcre, wcim, ab_re, ab_im, d_row, seq_len):
    nt = seq_len // S5_TILE

    def u_spec(b):
        return pl.BlockSpec((S5_TILE, BLOCK_CH), lambda j, i, b=b: (b * nt + i, j))

    rows = S5_TILE * S5_SEQS
    return pl.pallas_call(
        _s5_prompt_kernel,
        grid=(SSM_BLOCKS, nt),
        in_specs=[u_spec(b) for b in range(S5_SEQS)] + [
            pl.BlockSpec((None, BLOCK_CH, 2 * BLOCK_ST), lambda j, i: (j, 0, 0)),
            pl.BlockSpec((None, BLOCK_ST, BLOCK_CH), lambda j, i: (j, 0, 0)),
            pl.BlockSpec((None, BLOCK_ST, BLOCK_CH), lambda j, i: (j, 0, 0)),
            pl.BlockSpec((1, BLOCK_ST), lambda j, i: (0, j)),
            pl.BlockSpec((1, BLOCK_ST), lambda j, i: (0, j)),
            pl.BlockSpec((1, BLOCK_CH), lambda j, i: (0, j)),
        ],
        out_specs=[
            pl.BlockSpec((S5_SEQS, S5_TILE, BLOCK_CH), lambda j, i: (0, i, j)),
            pl.BlockSpec((S5_SEQS, BLOCK_ST), lambda j, i: (0, j)),
            pl.BlockSpec((S5_SEQS, BLOCK_ST), lambda j, i: (0, j)),
        ],
        out_shape=[
            jax.ShapeDtypeStruct((S5_SEQS, seq_len, SSM_WIDTH), F32),
            jax.ShapeDtypeStruct((S5_SEQS, SSM_GROUPS * SSM_STATE), F32),
            jax.ShapeDtypeStruct((S5_SEQS, SSM_GROUPS * SSM_STATE), F32),
        ],
        scratch_shapes=[
            pltpu.VMEM((2, rows, 128), F32),
            pltpu.VMEM((rows, 2 * BLOCK_ST), F32),
            pltpu.VMEM((2, rows, 128), F32),
            pltpu.VMEM((2 * S5_SEQS, 2 * BLOCK_ST), F32),
        ],
        compiler_params=_cparams(("parallel", "arbitrary")),
        name="s5_prompt",
    )(z, z, z, z, wb, wcre, wcim, ab_re, ab_im, d_row)


def _s5_sample_kernel(u_ref, s0re_ref, s0im_ref, wb_ref, wcre_ref, wcim_ref, abre_ref, abim_ref,
                      d_ref, y_ref, sre_ref, sim_ref, u_sl, y_sl):
    half = BLOCK_CH // 2
    n_seq = s0re_ref.shape[0]
    u_sl[0] = u_ref[:, :half]
    u_sl[1] = u_ref[:, half:]
    a_re = abre_ref[...]
    a_im = abim_ref[...]
    s_re = s0re_ref[...]
    s_im = s0im_ref[...]
    for t in range(SAMPLE_SEQ):
        rows = pl.ds(t, n_seq, stride=SAMPLE_SEQ)
        u_t = jnp.concatenate([u_sl[0, rows, :], u_sl[1, rows, :]], axis=1)
        bu = _dot(u_t.astype(BF16), wb_ref[...])
        n_re = a_re * s_re - a_im * s_im + bu[:, :BLOCK_ST]
        n_im = a_re * s_im + a_im * s_re + bu[:, BLOCK_ST:]
        s_re, s_im = n_re, n_im
        y = (_dot(s_re.astype(BF16), wcre_ref[...]) - _dot(s_im.astype(BF16), wcim_ref[...])
             + d_ref[...] * u_t)
        y = _gelu(y)
        y_sl[0, rows, :] = y[:, :half]
        y_sl[1, rows, :] = y[:, half:]
    y_ref[:, :half] = y_sl[0]
    y_ref[:, half:] = y_sl[1]
    sre_ref[...] = s_re
    sim_ref[...] = s_im


def _s5_sample(z, s0_re, s0_im, wb, wcre, wcim, ab_re, ab_im, d_row, row0, n_rows):
    n_seq = s0_re.shape[0]
    st_spec = pl.BlockSpec((n_seq, BLOCK_ST), lambda j: (0, j))
    return pl.pallas_call(
        _s5_sample_kernel,
        grid=(SSM_BLOCKS,),
        in_specs=[
            pl.BlockSpec((n_rows, BLOCK_CH), lambda j: (row0 // n_rows, j)),
            st_spec, st_spec,
            pl.BlockSpec((None, BLOCK_CH, 2 * BLOCK_ST), lambda j: (j, 0, 0)),
            pl.BlockSpec((None, BLOCK_ST, BLOCK_CH), lambda j: (j, 0, 0)),
            pl.BlockSpec((None, BLOCK_ST, BLOCK_CH), lambda j: (j, 0, 0)),
            pl.BlockSpec((1, BLOCK_ST), lambda j: (0, j)),
            pl.BlockSpec((1, BLOCK_ST), lambda j: (0, j)),
            pl.BlockSpec((1, BLOCK_CH), lambda j: (0, j)),
        ],
        out_specs=[pl.BlockSpec((n_rows, BLOCK_CH), lambda j: (0, j)), st_spec, st_spec],
        out_shape=[
            jax.ShapeDtypeStruct((n_rows, SSM_WIDTH), F32),
            jax.ShapeDtypeStruct(s0_re.shape, F32),
            jax.ShapeDtypeStruct(s0_im.shape, F32),
        ],
        scratch_shapes=[pltpu.VMEM((2, n_rows, 128), F32), pltpu.VMEM((2, n_rows, 128), F32)],
        compiler_params=_cparams(("parallel",)),
        name="s5_sample",
    )(z, s0_re, s0_im, wb, wcre, wcim, ab_re, ab_im, d_row)


def _postmix_kernel(hmp_ref, hms_ref, ysp_ref, yss_ref, xp_ref, xs_ref,
                    wglu_ref, bglu_ref, wout_ref, gain_ref, wq_ref, keys_ref,
                    h1_ref, c_ref, sp_ref, ss_ref, *, prompt_tiles):
    i = pl.program_id(0)
    ys = _group_pick(ysp_ref, yss_ref, prompt_tiles)
    hm = _group_pick(hmp_ref, hms_ref, prompt_tiles)
    glu = ys * _sigmoid(_dot(ys.astype(BF16), wglu_ref[...]) + bglu_ref[...])
    mix = (_dot(hm.astype(BF16), wout_ref[:MLSTM_WIDTH, :])
           + _dot(glu.astype(BF16), wout_ref[MLSTM_WIDTH:, :]))
    h1 = _group_pick(xp_ref, xs_ref, prompt_tiles) + mix
    h1_ref[...] = h1
    c = _rmsnorm(h1, gain_ref[...]).astype(BF16)
    c_ref[...] = c
    qp = _dot(c, wq_ref[...])
    scores = []
    width = KEY_GROUP * PEER_HALF
    for g in range(2 * PEER_HEADS // KEY_GROUP):
        sg = _dot_nt(keys_ref[g], qp[:, g * width:(g + 1) * width].astype(BF16))
        scores += [sg[jj * PEER_KEYS:(jj + 1) * PEER_KEYS, :] for jj in range(KEY_GROUP)]

    @pl.when(i < prompt_tiles)
    def _():
        for j, s in enumerate(scores):
            sp_ref[j] = s

    @pl.when(i >= prompt_tiles)
    def _():
        for j, s in enumerate(scores):
            ss_ref[j] = s


KEY_GROUP = 4


def _blockdiag_keys(keys):
    ng = keys.shape[0] // KEY_GROUP
    k = keys.reshape(ng, KEY_GROUP, PEER_KEYS, PEER_HALF)
    eye = jnp.eye(KEY_GROUP, dtype=bool)
    w = jnp.where(eye[None, :, None, :, None], k[:, :, :, None, :], 0.0)
    return w.reshape(ng, KEY_GROUP * PEER_KEYS, KEY_GROUP * PEER_HALF).astype(BF16)


def _postmix(hm_p, hm_s, ys_p, ys_s, x_p, x_s, wglu, bglu, wout, gain, wq, keys, tm=256):
    n_p, n_s = x_p.shape[0], x_s.shape[0]
    t = n_p + n_s
    pt = n_p // tm
    nk = 2 * PEER_HEADS
    return pl.pallas_call(
        functools.partial(_postmix_kernel, prompt_tiles=pt),
        grid=(t // tm,),
        in_specs=_group_specs(tm, MLSTM_WIDTH, pt) + _group_specs(tm, SSM_WIDTH, pt)
        + _group_specs(tm, D_MODEL, pt) + [
            _resident((SSM_WIDTH, SSM_WIDTH), lambda i: (0, 0)),
            _resident((1, SSM_WIDTH), lambda i: (0, 0)),
            _resident((D_MODEL, D_MODEL), lambda i: (0, 0)),
            _resident((1, D_MODEL), lambda i: (0, 0)),
            _resident((D_MODEL, PEER_HEADS * 2 * PEER_HALF), lambda i: (0, 0)),
            _resident(keys.shape, lambda i: (0, 0, 0)),
        ],
        out_specs=[
            pl.BlockSpec((tm, D_MODEL), lambda i: (i, 0)),
            pl.BlockSpec((tm, D_MODEL), lambda i: (i, 0)),
            pl.BlockSpec((nk, PEER_KEYS, tm), lambda i: (0, 0, jnp.minimum(i, pt - 1))),
            pl.BlockSpec((nk, PEER_KEYS, tm), lambda i: (0, 0, jnp.maximum(i - pt, 0))),
        ],
        out_shape=[
            jax.ShapeDtypeStruct((t, D_MODEL), F32),
            jax.ShapeDtypeStruct((t, D_MODEL), BF16),
            jax.ShapeDtypeStruct((nk, PEER_KEYS, n_p), F32),
            jax.ShapeDtypeStruct((nk, PEER_KEYS, n_s), F32),
        ],
        compiler_params=_cparams(("arbitrary",)),
        name="postmix",
    )(hm_p, hm_s, ys_p, ys_s, x_p, x_s, wglu, bglu, wout, gain, wq, keys)


ID_NONE = 1 << 20
TOPK_SUB = 8


def _sort_network(n):
    pairs = []
    p = 1
    while p < n:
        k = p
        while k >= 1:
            for j in range(k % p, n - k, 2 * k):
                for i in range(min(k, n - j - k)):
                    if (i + j) // (2 * p) == (i + j + k) // (2 * p):
                        pairs.append((i + j, i + j + k))
            k //= 2
        p *= 2
    return pairs


_SORT16 = _sort_network(PEER_TOPK)


def _precedes(b, a):
    (vb, ib), (va, ia) = b, a
    return jnp.logical_or(vb > va, jnp.logical_and(vb == va, ib < ia))


def _first_of(a, b):
    sw = _precedes(b, a)
    return jnp.where(sw, b[0], a[0]), jnp.where(sw, b[1], a[1])


def _exchange(items, i, j):
    a, b = items[i], items[j]
    sw = _precedes(b, a)
    items[i] = (jnp.where(sw, b[0], a[0]), jnp.where(sw, b[1], a[1]))
    items[j] = (jnp.where(sw, a[0], b[0]), jnp.where(sw, a[1], b[1]))


def _sort16(items):
    items = list(items)
    for i, j in _SORT16:
        _exchange(items, i, j)
    return items


def _bitonic_merge16(items):
    items = list(items)
    d = PEER_TOPK // 2
    while d >= 1:
        for i in range(PEER_TOPK):
            if i & d == 0:
                _exchange(items, i, i + d)
        d //= 2
    return items


def _merge_top16(a, b):
    return _bitonic_merge16([_first_of(a[i], b[PEER_TOPK - 1 - i]) for i in range(PEER_TOPK)])


def _top16_of_keys(s_ref, half, shape):
    best = None
    for g in range(PEER_KEYS // PEER_TOPK):
        grp = _sort16([(s_ref[half, g * PEER_TOPK + k], jnp.full(shape, g * PEER_TOPK + k, I32))
                       for k in range(PEER_TOPK)])
        best = grp if best is None else _merge_top16(best, grp)
    return best


def _route_head(s_ref):
    shape = s_ref.shape[2:]
    top1 = _top16_of_keys(s_ref, 0, shape)
    top2 = _top16_of_keys(s_ref, 1, shape)

    def pair(i, j):
        return top1[i][0] + top2[j][0], jnp.full(shape, i * PEER_TOPK + j, I32)

    pad = (jnp.full(shape, NEG_INF, F32), jnp.full(shape, ID_NONE, I32))
    g0 = [pair(0, j) for j in range(16)]
    g1 = _bitonic_merge16([pair(1, j) for j in range(8)] + [pair(i, 0) for i in range(15, 7, -1)])
    g2 = _sort16([pair(i, j) for i in range(2, 7) for j in range(PEER_TOPK // (i + 1))])
    g3 = [pair(7, 0), pair(7, 1)] + [pad] * 14
    best = _merge_top16(_merge_top16(g0, g1), _merge_top16(g2, g3))

    mx = best[0][0]
    e1s, e2s, exps = [], [], []
    for k in range(PEER_TOPK):
        v, pid = best[k]
        a = lax.shift_right_logical(pid, 4)
        b = jnp.bitwise_and(pid, PEER_TOPK - 1)
        e1 = jnp.zeros(shape, I32)
        e2 = jnp.zeros(shape, I32)
        for r in range(PEER_TOPK):
            e1 = jnp.where(a == r, top1[r][1], e1)
            e2 = jnp.where(b == r, top2[r][1], e2)
        e1s.append(e1)
        e2s.append(e2)
        exps.append(jnp.exp(v - mx))
    total = exps[0]
    for k in range(1, PEER_TOPK):
        total = total + exps[k]
    return e1s, e2s, [e / total for e in exps]


def _topk_kernel(s_ref, e1_ref, e2_ref, g_ref):
    e1s, e2s, gs = _route_head(s_ref)
    for k in range(PEER_TOPK):
        e1_ref[k] = e1s[k]
        e2_ref[k] = e2s[k]
        g_ref[k] = gs[k]


def _topk(scores):
    t = scores.shape[-1]
    ng = t // 128
    sub = min(ng, TOPK_SUB)
    r = PEER_HEADS * PEER_TOPK
    o_spec = pl.BlockSpec((PEER_TOPK, sub, 128), lambda i, h: (h, i, 0))
    outs = pl.pallas_call(
        _topk_kernel,
        grid=(ng // sub, PEER_HEADS),
        in_specs=[pl.BlockSpec((2, PEER_KEYS, sub, 128), lambda i, h: (h, 0, i, 0))],
        out_specs=[o_spec, o_spec, o_spec],
        out_shape=[
            jax.ShapeDtypeStruct((r, ng, 128), I32),
            jax.ShapeDtypeStruct((r, ng, 128), I32),
            jax.ShapeDtypeStruct((r, ng, 128), F32),
        ],
        compiler_params=_cparams(("parallel", "parallel")),
        name="topk",
    )(scores.reshape(2 * PEER_HEADS, PEER_KEYS, ng, 128))
    return [o.reshape(r, t) for o in outs]


WB_GROUP = 16
WB_PITCH = 132


def _token_weights(e1_row, e2_row, g_row):
    sub = lax.broadcasted_iota(I32, (PEER_KEYS, PEER_KEYS), 0)
    onehot1 = jnp.where(sub == e1_row, 1.0, 0.0).astype(BF16)
    gated2 = jnp.where(sub == e2_row, 0.5 * g_row, 0.0).astype(BF16)
    return _dot_nt(onehot1, gated2)


def _wbuild_kernel(e1_ref, e2_ref, g_ref, w_ref, stage):
    for grp in range(e1_ref.shape[0] // WB_GROUP):
        base = (grp % 2) * WB_GROUP * WB_PITCH
        for tt in range(WB_GROUP):
            t = grp * WB_GROUP + tt
            stage[base + tt * WB_PITCH:base + tt * WB_PITCH + PEER_KEYS, :] = _token_weights(
                e1_ref[t:t + 1, :], e2_ref[t:t + 1, :], g_ref[t:t + 1, :])
        for e in range(PEER_KEYS):
            blk = stage[pl.ds(base + e, WB_GROUP, stride=WB_PITCH), :]
            w_ref[e, grp * WB_GROUP:(grp + 1) * WB_GROUP, :] = blk.astype(BF16)


def _wbuild(e1, e2, g, tw=256):
    t = e1.shape[0]
    i_spec = pl.BlockSpec((tw, PEER_KEYS), lambda i: (i, 0))
    return pl.pallas_call(
        _wbuild_kernel,
        grid=(t // tw,),
        in_specs=[i_spec, i_spec, i_spec],
        out_specs=pl.BlockSpec((PEER_KEYS, tw, PEER_KEYS), lambda i: (0, i, 0)),
        out_shape=jax.ShapeDtypeStruct((PEER_KEYS, t, PEER_KEYS), BF16),
        scratch_shapes=[pltpu.VMEM((2 * WB_GROUP * WB_PITCH, PEER_KEYS), F32)],
        compiler_params=_cparams(("parallel",)),
        name="wbuild",
    )(e1, e2, g)


PEER_EB = 256


def _peer_kernel(c_ref, u_ref, v_ref, w_ref, o_ref, s_s):
    j = pl.program_id(1)
    last = pl.num_programs(1) - 1

    def gated():
        parts = []
        for k in range(PEER_EB // PEER_KEYS):
            x = s_s[:, k * PEER_KEYS:(k + 1) * PEER_KEYS]
            t = jnp.tanh(x * (GELU_C1 + GELU_C2 * (x * x)))
            parts.append((x * w_ref[k].astype(F32)) * (1.0 + t))
        return jnp.concatenate(parts, axis=1).astype(BF16)

    def scores():
        return _dot_nt(c_ref[...], u_ref[...].astype(BF16))

    @pl.when(j == 0)
    def _():
        o_ref[...] = jnp.zeros_like(o_ref)
        s_s[...] = scores()

    @pl.when(jnp.logical_and(j > 0, j < last))
    def _():
        wact = gated()
        s_s[...] = scores()
        o_ref[...] += _dot(wact, v_ref[...].astype(BF16))

    @pl.when(j == last)
    def _():
        o_ref[...] += _dot(gated(), v_ref[...].astype(BF16))


def _peer(c, u, v, w3, n_tiles=4):
    t = c.shape[0]
    tm = t // n_tiles
    nb = PEER_EXPERTS // PEER_EB
    return pl.pallas_call(
        _peer_kernel,
        grid=(n_tiles, nb + 1),
        in_specs=[
            pl.BlockSpec((tm, D_MODEL), lambda i, j: (i, 0), pipeline_mode=pl.Buffered(1)),
            pl.BlockSpec((PEER_EB, D_MODEL), lambda i, j: (jnp.minimum(j, nb - 1), 0)),
            pl.BlockSpec((PEER_EB, D_MODEL), lambda i, j: (jnp.maximum(j - 1, 0), 0)),
            pl.BlockSpec((PEER_EB // PEER_KEYS, tm, PEER_KEYS),
                         lambda i, j: (jnp.maximum(j - 1, 0), i, 0)),
        ],
        out_specs=pl.BlockSpec((tm, D_MODEL), lambda i, j: (i, 0), pipeline_mode=pl.Buffered(1)),
        out_shape=jax.ShapeDtypeStruct((t, D_MODEL), F32),
        scratch_shapes=[pltpu.VMEM((tm, PEER_EB), F32)],
        compiler_params=_cparams(("parallel", "arbitrary")),
        name="peer",
    )(c, u, v, w3)


def _tail_kernel(h1_ref, peer_ref, pp_ref, ps_ref, gple_ref, wgate_ref, wproj_ref, gfin_ref,
                 yp_ref, ys_ref, *, prompt_tiles):
    i = pl.program_id(0)
    h2 = h1_ref[...] + peer_ref[...]
    gate = _sigmoid(_dot(_rmsnorm(h2, gple_ref[...]).astype(BF16), wgate_ref[...]))
    e = _dot(_group_pick(pp_ref, ps_ref, prompt_tiles).astype(BF16), wproj_ref[...])
    y = _rmsnorm(h2 + e * gate, gfin_ref[...])

    @pl.when(i < prompt_tiles)
    def _():
        yp_ref[...] = y

    @pl.when(i >= prompt_tiles)
    def _():
        ys_ref[...] = y


def _tail(h1, peer, p_p, p_s, gple, wgate, wproj, gfin, tm=512):
    t = h1.shape[0]
    n_prompt = p_p.shape[0]
    pt = n_prompt // tm
    ple = p_p.shape[1]
    return pl.pallas_call(
        functools.partial(_tail_kernel, prompt_tiles=pt),
        grid=(t // tm,),
        in_specs=[
            pl.BlockSpec((tm, D_MODEL), lambda i: (i, 0)),
            pl.BlockSpec((tm, D_MODEL), lambda i: (i, 0)),
        ] + _group_specs(tm, ple, pt) + [
            _resident((1, D_MODEL), lambda i: (0, 0)),
            _resident((D_MODEL, D_MODEL), lambda i: (0, 0)),
            _resident((ple, D_MODEL), lambda i: (0, 0)),
            _resident((1, D_MODEL), lambda i: (0, 0)),
        ],
        out_specs=[
            pl.BlockSpec((tm, D_MODEL), lambda i: (jnp.minimum(i, pt - 1), 0)),
            pl.BlockSpec((tm, D_MODEL), lambda i: (jnp.maximum(i - pt, 0), 0)),
        ],
        out_shape=[
            jax.ShapeDtypeStruct((n_prompt, D_MODEL), F32),
            jax.ShapeDtypeStruct((t - n_prompt, D_MODEL), F32),
        ],
        compiler_params=_cparams(("arbitrary",)),
        name="tail",
    )(h1, peer, p_p, p_s, gple, wgate, wproj, gfin)


def kernel(x_prompt, x_sample, state_mlstm_C, state_mlstm_n, state_mlstm_m, state_ssm_re, state_ssm_im, p_prompt, p_sample, norm_mix, w_in, b_igate, b_fgate, mlstm_norm, ssm_A_re, ssm_A_im, ssm_B_re, ssm_B_im, ssm_C_re, ssm_C_im, ssm_D, ssm_log_dt, w_glu, b_glu, w_out, norm_ffn, peer_w_q, peer_keys, peer_u, peer_v, norm_ple, w_ple_gate, w_ple_proj, norm_final):
    n_pseq, p_len, _ = x_prompt.shape
    n_sseq, s_len, _ = x_sample.shape
    assert s_len == SAMPLE_SEQ and n_pseq == S5_SEQS and w_in.shape[0] == 1
    n_prompt = n_pseq * p_len
    n_sample = n_sseq * s_len
    row = lambda t: t.reshape(1, -1)

    x_p = x_prompt.reshape(n_prompt, D_MODEL)
    x_s = x_sample.reshape(n_sample, D_MODEL)

    w_bf = w_in[0].astype(BF16)
    n_gate = 2 * HEADS
    w_tail = jnp.concatenate(
        [w_bf[:, Z_U + n_gate:], w_bf[:, Z_U:Z_U + n_gate],
         jnp.zeros((D_MODEL, 128 - n_gate), BF16)], axis=1)
    gbias = jnp.concatenate([b_igate[0], b_fgate[0], jnp.zeros((128 - n_gate,), F32)]).reshape(1, 128)
    z, zt = _in_proj(x_p, x_s, row(norm_mix[0]), w_bf, w_tail, gbias)

    gain_m = row(mlstm_norm[0])
    hm_p, c_p, n_p, m_p = _mlstm_prompt(z, zt, gain_m, n_pseq, p_len)
    nbh = n_sseq * HEADS
    hm_s, c_s, n_s, m_s = _mlstm_sample(
        z, zt, gain_m,
        state_mlstm_C[0].reshape(nbh, HEAD_DIM, HEAD_DIM),
        state_mlstm_n[0].reshape(nbh, 1, HEAD_DIM),
        jnp.broadcast_to(state_mlstm_m[0].reshape(nbh, 1, 1), (nbh, 1, 128)),
        n_prompt, n_sample)

    bb_re, bb_im, ab_re, ab_im = _s5_params(ssm_A_re[0], ssm_A_im[0], ssm_log_dt[0], ssm_B_re[0], ssm_B_im[0])
    wb = _blockdiag_in(bb_re, bb_im)
    wcre = _blockdiag_out(ssm_C_re[0])
    wcim = _blockdiag_out(ssm_C_im[0])
    d_row = row(ssm_D[0])
    ys_p, sre_p, sim_p = _s5_prompt(zt, wb, wcre, wcim, ab_re, ab_im, d_row, p_len)
    n_st = SSM_GROUPS * SSM_STATE
    ys_s, sre_s, sim_s = _s5_sample(
        zt, state_ssm_re[0].reshape(n_sseq, n_st), state_ssm_im[0].reshape(n_sseq, n_st),
        wb, wcre, wcim, ab_re, ab_im, d_row, n_prompt, n_sample)

    keys = _blockdiag_keys(peer_keys[0].reshape(2 * PEER_HEADS, PEER_KEYS, PEER_HALF))
    h1, c, scores_p, scores_s = _postmix(
        hm_p, hm_s, ys_p.reshape(n_prompt, SSM_WIDTH), ys_s, x_p, x_s,
        w_glu[0].astype(BF16), row(b_glu[0]), w_out[0].astype(BF16),
        row(norm_ffn[0]), peer_w_q[0].astype(BF16), keys)
    routing = [jnp.concatenate([rp, rs], axis=1).T for rp, rs in zip(_topk(scores_p), _topk(scores_s))]
    w3 = _wbuild(*routing)
    peer = _peer(c, peer_u[0], peer_v[0], w3)

    y_p, y_s = _tail(h1, peer, p_prompt[0].reshape(n_prompt, -1), p_sample[0].reshape(n_sample, -1),
                     row(norm_ple[0]), w_ple_gate[0].astype(BF16), w_ple_proj[0].astype(BF16),
                     row(norm_final))

    st_shape = (1, -1, SSM_GROUPS, SSM_STATE)
    return (
        y_p.reshape(x_prompt.shape), y_s.reshape(x_sample.shape),
        c_p.reshape(1, n_pseq, HEADS, HEAD_DIM, HEAD_DIM), n_p.reshape(1, n_pseq, HEADS, HEAD_DIM),
        m_p[:, 0, 0].reshape(1, n_pseq, HEADS),
        sre_p.reshape(st_shape), sim_p.reshape(st_shape),
        c_s.reshape(1, n_sseq, HEADS, HEAD_DIM, HEAD_DIM), n_s.reshape(1, n_sseq, HEADS, HEAD_DIM),
        m_s[:, 0, 0].reshape(1, n_sseq, HEADS),
        sre_s.reshape(st_shape), sim_s.reshape(st_shape),
    )
```

```python
import functools

import jax
import jax.numpy as jnp
from jax import lax
from jax.experimental import pallas as pl
from jax.experimental.pallas import tpu as pltpu

F32 = jnp.float32
BF16 = jnp.bfloat16
I32 = jnp.int32

EPS = 1e-6
D_MODEL = 2048
HEADS = 4
HEAD_DIM = 256
MLSTM_WIDTH = HEADS * HEAD_DIM
SSM_WIDTH = 1024
SSM_GROUPS = 64
SSM_STATE = 64
SSM_GROUP = 16
GROUPS_PER_BLOCK = 16
SSM_BLOCKS = SSM_GROUPS // GROUPS_PER_BLOCK
BLOCK_CH = GROUPS_PER_BLOCK * SSM_GROUP
BLOCK_ST = GROUPS_PER_BLOCK * SSM_STATE
PEER_HEADS = 8
PEER_KEYS = 128
PEER_TOPK = 16
PEER_HALF = 64
PEER_EXPERTS = PEER_KEYS * PEER_KEYS

Z_Q, Z_K, Z_V, Z_O, Z_U = 0, 1024, 2048, 3072, 4096
ZT_G = 1024
ZT_COLS = ZT_G + 128
IN_PROJ_COLS = 2048

MLSTM_CHUNK = 256
K_SCALE = HEAD_DIM ** -0.5
NEG_INF = float("-inf")

VMEM_LIMIT = 56 * 1024 * 1024


def _cparams(semantics):
    return pltpu.CompilerParams(dimension_semantics=semantics, vmem_limit_bytes=VMEM_LIMIT)


def _resident(shape, index_map):
    return pl.BlockSpec(shape, index_map, pipeline_mode=pl.Buffered(1))


def _sigmoid(x):
    return 1.0 / (1.0 + jnp.exp(-x))


GELU_C1 = 0.7978845608028654
GELU_C2 = GELU_C1 * 0.044715


def _gelu(x):
    return x * (0.5 * (1.0 + jnp.tanh(GELU_C1 * (x + 0.044715 * (x * x * x)))))


def _rmsnorm(x, gain):
    return x * lax.rsqrt(jnp.mean(x * x, axis=-1, keepdims=True) + EPS) * gain


def _dot(a, b):
    return jnp.dot(a, b, preferred_element_type=F32)


def _dot_nt(a, b):
    return lax.dot_general(a, b, (((1,), (1,)), ((), ())), preferred_element_type=F32)


def _dot_tn(a, b):
    return lax.dot_general(a, b, (((0,), (0,)), ((), ())), preferred_element_type=F32)


def _group_specs(tm, width, prompt_tiles):
    return [
        pl.BlockSpec((tm, width), lambda i: (jnp.minimum(i, prompt_tiles - 1), 0)),
        pl.BlockSpec((tm, width), lambda i: (jnp.maximum(i - prompt_tiles, 0), 0)),
    ]


def _group_pick(prompt_ref, sample_ref, prompt_tiles):
    return jnp.where(pl.program_id(0) < prompt_tiles, prompt_ref[...], sample_ref[...])


def _in_proj_kernel(xp_ref, xs_ref, gain_ref, wm_ref, wt_ref, gbias_ref, zm_ref, zt_ref, a_s,
                    *, prompt_tiles, main_steps):
    j = pl.program_id(1)

    @pl.when(j == 0)
    def _():
        a_s[...] = _rmsnorm(_group_pick(xp_ref, xs_ref, prompt_tiles), gain_ref[...]).astype(BF16)

    for step in range(main_steps):
        @pl.when(j == step)
        def _(base=step * IN_PROJ_COLS):
            for c0 in range(0, IN_PROJ_COLS, 512):
                zm_ref[:, c0:c0 + 512] = _dot(a_s[...], wm_ref[:, base + c0:base + c0 + 512])

    @pl.when(j == main_steps)
    def _():
        a = a_s[...]
        for c0 in range(0, ZT_G, 512):
            zt_ref[:, c0:c0 + 512] = _dot(a, wt_ref[:, c0:c0 + 512])
        gz = _dot(a, wt_ref[:, ZT_G:]) + gbias_ref[...]
        lane = lax.broadcasted_iota(I32, gz.shape, 1)
        log_f = jnp.minimum(gz, 0.0) - jnp.log(1.0 + jnp.exp(-jnp.abs(gz)))
        zt_ref[:, ZT_G:] = jnp.where(lane < HEADS, gz, log_f)


def _in_proj(x_p, x_s, gain, w_bf, w_tail, gbias, tm=512):
    t = x_p.shape[0] + x_s.shape[0]
    pt = x_p.shape[0] // tm
    ms = Z_U // IN_PROJ_COLS
    return pl.pallas_call(
        functools.partial(_in_proj_kernel, prompt_tiles=pt, main_steps=ms),
        grid=(t // tm, ms + 1),
        in_specs=[
            pl.BlockSpec((tm, D_MODEL), lambda i, j: (jnp.minimum(i, pt - 1), 0)),
            pl.BlockSpec((tm, D_MODEL), lambda i, j: (jnp.maximum(i - pt, 0), 0),
                         pipeline_mode=pl.Buffered(1)),
            _resident((1, D_MODEL), lambda i, j: (0, 0)),
            _resident((D_MODEL, Z_U), lambda i, j: (0, 0)),
            _resident((D_MODEL, ZT_COLS), lambda i, j: (0, 0)),
            _resident((1, 128), lambda i, j: (0, 0)),
        ],
        out_specs=[
            pl.BlockSpec((tm, IN_PROJ_COLS), lambda i, j: (i, jnp.minimum(j, ms - 1))),
            pl.BlockSpec((tm, ZT_COLS), lambda i, j: (i, 0)),
        ],
        out_shape=[
            jax.ShapeDtypeStruct((t, Z_U), F32),
            jax.ShapeDtypeStruct((t, ZT_COLS), F32),
        ],
        scratch_shapes=[pltpu.VMEM((tm, D_MODEL), BF16)],
        compiler_params=_cparams(("arbitrary", "arbitrary")),
        name="in_proj",
    )(x_p, x_s, gain, w_bf, w_tail, gbias)


def _mlstm_chunks(items):
    L = items[0][0].shape[0]
    row = lax.broadcasted_iota(I32, (L, L), 0)
    col = lax.broadcasted_iota(I32, (L, L), 1)
    diag = row == col
    causal = col <= row

    gates = []
    for (_, _, _, ig_col, lf_col, _, _, m_state) in items:
        lf_row = jnp.sum(jnp.where(diag, lf_col, 0.0), axis=0, keepdims=True)
        ig_row = jnp.sum(jnp.where(diag, ig_col, 0.0), axis=0, keepdims=True)
        b_col = jnp.sum(jnp.where(causal, lf_row, 0.0), axis=1, keepdims=True)
        b_row = jnp.sum(jnp.where(row <= col, lf_col, 0.0), axis=0, keepdims=True)
        r_row = ig_row - b_row
        r_col = ig_col - b_col
        run_max = jnp.max(jnp.where(causal, r_row, NEG_INF), axis=1, keepdims=True)
        m_run = jnp.maximum(m_state, run_max)
        w_intra = jnp.exp(jnp.where(causal, r_row - m_run, NEG_INF))
        w_inter = jnp.exp(m_state - m_run)
        gates.append((b_col, r_col, m_run, w_intra, w_inter))

    products = []
    for (q, k, v, _, _, c_get, _, _) in items:
        qb = q.astype(BF16)
        vb = v.astype(BF16)
        products.append((_dot_nt(qb, k.astype(BF16)), _dot(qb, c_get().astype(BF16)), vb))

    heads = []
    for (q, _, _, _, _, _, n_state, _), (b_col, _, m_run, w_intra, w_inter), (qk, qc, vb) in zip(
            items, gates, products):
        s = qk * w_intra
        num = w_inter * qc + _dot(s.astype(BF16), vb)
        qn = jnp.sum(q * n_state, axis=1, keepdims=True)
        den = w_inter * qn + jnp.sum(s, axis=1, keepdims=True)
        heads.append(num / jnp.maximum(jnp.abs(den), jnp.exp(-(b_col + m_run))))

    out = []
    for (_, k, _, _, _, c_get, n_state, m_state), (b_col, r_col, m_run, _, _), (_, _, vb), hb in zip(
            items, gates, products, heads):
        m_last = m_run[L - 1:L, :]
        m_new = b_col[L - 1:L, :] + m_last
        kw = k * jnp.exp(r_col - m_last)
        sc = jnp.exp(m_state - m_last)
        c_new = sc * c_get() + _dot_tn(kw.astype(BF16), vb)
        n_new = sc * n_state + jnp.sum(kw, axis=0, keepdims=True)
        out.append((hb, c_new, n_new, m_new))
    return out


def _head_out(hb, gain, o_pre):
    return hb * lax.rsqrt(jnp.mean(hb * hb, axis=-1, keepdims=True) + EPS) * gain * _sigmoid(o_pre)


def _mlstm_prompt_kernel(q_ref, k_ref, v_ref, o_ref, g_ref, gain_ref,
                         hm_ref, c_out_ref, n_out_ref, m_out_ref, c_s, n_s, m_s):
    h0 = pl.program_id(1) * PROMPT_HEADS
    c_s[...] = jnp.zeros_like(c_s)
    n_s[...] = jnp.zeros_like(n_s)
    m_s[...] = jnp.zeros_like(m_s)

    def body(ci, carry):
        rows = pl.ds(pl.multiple_of(ci * MLSTM_CHUNK, MLSTM_CHUNK), MLSTM_CHUNK)
        gates = g_ref[rows, :]
        lane = lax.broadcasted_iota(I32, gates.shape, 1)
        items = []
        for hh in range(PROMPT_HEADS):
            cs = slice(hh * HEAD_DIM, (hh + 1) * HEAD_DIM)
            ig_col = jnp.sum(jnp.where(lane == h0 + hh, gates, 0.0), axis=1, keepdims=True)
            lf_col = jnp.sum(jnp.where(lane == h0 + hh + HEADS, gates, 0.0), axis=1, keepdims=True)
            items.append((q_ref[rows, cs], k_ref[rows, cs] * K_SCALE, v_ref[rows, cs], ig_col, lf_col,
                          functools.partial(lambda r, i: r[i], c_s, hh), n_s[hh], m_s[hh][:, 0:1]))
        for hh, (hb, c_new, n_new, m_new) in enumerate(_mlstm_chunks(items)):
            cs = slice(hh * HEAD_DIM, (hh + 1) * HEAD_DIM)
            c_s[hh] = c_new
            n_s[hh] = n_new
            m_s[hh] = jnp.broadcast_to(m_new, (1, 128))
            hm_ref[rows, cs] = _head_out(hb, gain_ref[:, cs], o_ref[rows, cs])
        return carry

    lax.fori_loop(0, q_ref.shape[0] // MLSTM_CHUNK, body, 0)
    c_out_ref[...] = c_s[...]
    n_out_ref[...] = n_s[...]
    m_out_ref[...] = m_s[...]


PROMPT_HEADS = 2


def _mlstm_prompt(z, zt, gain, n_seq, seq_len):
    width = PROMPT_HEADS * HEAD_DIM
    hsteps = HEADS // PROMPT_HEADS

    def sec(off):
        return pl.BlockSpec((seq_len, width), lambda b, h, o=off // width: (b, o + h))

    def state(*tail):
        return pl.BlockSpec((PROMPT_HEADS,) + tail, lambda b, h: (b * hsteps + h,) + (0,) * len(tail))

    nbh = n_seq * HEADS
    return pl.pallas_call(
        _mlstm_prompt_kernel,
        grid=(n_seq, hsteps),
        in_specs=[
            sec(Z_Q), sec(Z_K), sec(Z_V), sec(Z_O),
            pl.BlockSpec((seq_len, 128), lambda b, h: (b, ZT_G // 128)),
            pl.BlockSpec((1, width), lambda b, h: (0, h)),
        ],
        out_specs=[
            pl.BlockSpec((seq_len, width), lambda b, h: (b, h)),
            state(HEAD_DIM, HEAD_DIM), state(1, HEAD_DIM), state(1, 128),
        ],
        out_shape=[
            jax.ShapeDtypeStruct((n_seq * seq_len, MLSTM_WIDTH), F32),
            jax.ShapeDtypeStruct((nbh, HEAD_DIM, HEAD_DIM), F32),
            jax.ShapeDtypeStruct((nbh, 1, HEAD_DIM), F32),
            jax.ShapeDtypeStruct((nbh, 1, 128), F32),
        ],
        scratch_shapes=[
            pltpu.VMEM((PROMPT_HEADS, HEAD_DIM, HEAD_DIM), F32),
            pltpu.VMEM((PROMPT_HEADS, 1, HEAD_DIM), F32),
            pltpu.VMEM((PROMPT_HEADS, 1, 128), F32),
        ],
        compiler_params=_cparams(("parallel", "parallel")),
        name="mlstm_prompt",
    )(z, z, z, z, zt, gain)


SAMPLE_SEQ = 4
SAMPLE_ROWS = 16


def _mlstm_sample_kernel(z_ref, g_ref, gain_ref, c_in_ref, n_in_ref, m_in_ref,
                         hm_ref, c_out_ref, n_out_ref, m_out_ref):
    row = lax.broadcasted_iota(I32, (8, 1), 0)
    for grp in range(SAMPLE_ROWS // 8):
        r0 = grp * 8
        gates = g_ref[r0:r0 + 8, :]
        items, where = [], []
        for half in range(2):
            lo = half * SAMPLE_SEQ
            live = jnp.logical_and(row >= lo, row < lo + SAMPLE_SEQ)
            for h in range(HEADS):
                j = (grp * 2 + half) * HEADS + h
                q = z_ref[r0:r0 + 8, Z_Q + h * HEAD_DIM:Z_Q + (h + 1) * HEAD_DIM]
                k = z_ref[r0:r0 + 8, Z_K + h * HEAD_DIM:Z_K + (h + 1) * HEAD_DIM]
                v = z_ref[r0:r0 + 8, Z_V + h * HEAD_DIM:Z_V + (h + 1) * HEAD_DIM]
                k = jnp.where(live, k * K_SCALE, 0.0)
                v = jnp.where(live, v, 0.0)
                ig_col = jnp.where(live, gates[:, h:h + 1], NEG_INF)
                lf_col = jnp.where(live, gates[:, HEADS + h:HEADS + h + 1], 0.0)
                items.append((q, k, v, ig_col, lf_col, functools.partial(lambda r, i: r[i], c_in_ref, j),
                              n_in_ref[j], m_in_ref[j][:, 0:1]))
                where.append((j, lo, h))
        for (j, lo, h), (hb, c_new, n_new, m_new) in zip(where, _mlstm_chunks(items)):
            cs = slice(h * HEAD_DIM, (h + 1) * HEAD_DIM)
            o = z_ref[r0:r0 + 8, Z_O + h * HEAD_DIM:Z_O + (h + 1) * HEAD_DIM]
            out = _head_out(hb, gain_ref[:, cs], o)
            hm_ref[r0 + lo:r0 + lo + SAMPLE_SEQ, cs] = out[lo:lo + SAMPLE_SEQ, :]
            c_out_ref[j] = c_new
            n_out_ref[j] = n_new
            m_out_ref[j] = jnp.broadcast_to(m_new, (1, 128))


def _mlstm_sample(z, zt, gain, c0, n0, m0, row0, n_rows):
    nbh = c0.shape[0]
    per = SAMPLE_ROWS // SAMPLE_SEQ * HEADS
    rb0 = row0 // SAMPLE_ROWS
    return pl.pallas_call(
        _mlstm_sample_kernel,
        grid=(n_rows // SAMPLE_ROWS,),
        in_specs=[
            pl.BlockSpec((SAMPLE_ROWS, Z_U), lambda i: (rb0 + i, 0)),
            pl.BlockSpec((SAMPLE_ROWS, 128), lambda i: (rb0 + i, ZT_G // 128)),
            pl.BlockSpec((1, MLSTM_WIDTH), lambda i: (0, 0)),
            pl.BlockSpec((per, HEAD_DIM, HEAD_DIM), lambda i: (i, 0, 0)),
            pl.BlockSpec((per, 1, HEAD_DIM), lambda i: (i, 0, 0)),
            pl.BlockSpec((per, 1, 128), lambda i: (i, 0, 0)),
        ],
        out_specs=[
            pl.BlockSpec((SAMPLE_ROWS, MLSTM_WIDTH), lambda i: (i, 0)),
            pl.BlockSpec((per, HEAD_DIM, HEAD_DIM), lambda i: (i, 0, 0)),
            pl.BlockSpec((per, 1, HEAD_DIM), lambda i: (i, 0, 0)),
            pl.BlockSpec((per, 1, 128), lambda i: (i, 0, 0)),
        ],
        out_shape=[
            jax.ShapeDtypeStruct((n_rows, MLSTM_WIDTH), F32),
            jax.ShapeDtypeStruct((nbh, HEAD_DIM, HEAD_DIM), F32),
            jax.ShapeDtypeStruct((nbh, 1, HEAD_DIM), F32),
            jax.ShapeDtypeStruct((nbh, 1, 128), F32),
        ],
        compiler_params=_cparams(("parallel",)),
        name="mlstm_sample",
    )(z, zt, gain, c0, n0, m0)


def _s5_discretise(a_re, a_im, log_dt):
    dt = jnp.exp(log_dt)
    mag = jnp.exp(dt * a_re)
    ab_re = mag * jnp.cos(dt * a_im)
    ab_im = mag * jnp.sin(dt * a_im)
    den = a_re * a_re + a_im * a_im
    xr = ab_re - 1.0
    f_re = (xr * a_re + ab_im * a_im) / den
    f_im = (ab_im * a_re - xr * a_im) / den
    return ab_re, ab_im, f_re, f_im


def _s5_param_kernel(are, aim, ldt, bre, bim, cre, cim, are_r, aim_r, ldt_r,
                     wb_o, wcre_o, wcim_o, ab_re_o, ab_im_o):
    _, _, f_re, f_im = _s5_discretise(are[...], aim[...], ldt[...])
    bb = (f_re * bre[...] - f_im * bim[...], f_re * bim[...] + f_im * bre[...])
    row_g = lax.shift_right_logical(lax.broadcasted_iota(I32, (BLOCK_CH, BLOCK_ST), 0), 4)
    lane_g = lax.shift_right_logical(lax.broadcasted_iota(I32, (BLOCK_CH, BLOCK_ST), 1), 6)
    keep = row_g == lane_g
    for j in range(SSM_BLOCKS):
        for ri in range(2):
            src = bb[ri][j * BLOCK_CH:(j + 1) * BLOCK_CH, :]
            tiled = jnp.concatenate([src] * (BLOCK_ST // 128), axis=1)
            wb_o[j, :, ri * BLOCK_ST:(ri + 1) * BLOCK_ST] = jnp.where(keep, tiled, 0.0).astype(BF16)
    row_g = lax.shift_right_logical(lax.broadcasted_iota(I32, (BLOCK_ST, BLOCK_CH), 0), 6)
    lane_g = lax.shift_right_logical(lax.broadcasted_iota(I32, (BLOCK_ST, BLOCK_CH), 1), 4)
    keep = row_g == lane_g
    for j in range(SSM_BLOCKS):
        for src_ref, dst in ((cre, wcre_o), (cim, wcim_o)):
            src = src_ref[j * BLOCK_ST:(j + 1) * BLOCK_ST, :]
            tiled = jnp.concatenate([src] * (BLOCK_CH // 128), axis=1)
            dst[j] = jnp.where(keep, tiled, 0.0).astype(BF16)
    ab_re, ab_im, _, _ = _s5_discretise(are_r[...], aim_r[...], ldt_r[...])
    ab_re_o[...] = ab_re
    ab_im_o[...] = ab_im


def _s5_params(a_re, a_im, log_dt, b_re, b_im, c_re, c_im):
    g, p = a_re.shape
    c = b_re.shape[-1]
    dup = lambda t: jnp.concatenate([t] * (128 // t.shape[-1]), axis=-1)
    by_channel = lambda t: dup(jnp.repeat(t, c, axis=0))
    b_rows = lambda t: dup(jnp.transpose(t, (0, 2, 1)).reshape(g * c, p))
    c_rows = lambda t: dup(jnp.transpose(t, (0, 2, 1)).reshape(g * p, c))
    ldt = jnp.broadcast_to(jnp.repeat(log_dt, c)[:, None], (g * c, 128))
    ldt_r = jnp.broadcast_to(log_dt[:, None], (g, p)).reshape(1, g * p)
    rowv = jax.ShapeDtypeStruct((1, g * p), F32)
    return pl.pallas_call(
        _s5_param_kernel,
        out_shape=[
            jax.ShapeDtypeStruct((SSM_BLOCKS, BLOCK_CH, 2 * BLOCK_ST), BF16),
            jax.ShapeDtypeStruct((SSM_BLOCKS, BLOCK_ST, BLOCK_CH), BF16),
            jax.ShapeDtypeStruct((SSM_BLOCKS, BLOCK_ST, BLOCK_CH), BF16),
            rowv, rowv,
        ],
        compiler_params=pltpu.CompilerParams(vmem_limit_bytes=VMEM_LIMIT),
        name="s5_params",
    )(by_channel(a_re), by_channel(a_im), ldt, b_rows(b_re), b_rows(b_im), c_rows(c_re), c_rows(c_im),
      a_re.reshape(1, g * p), a_im.reshape(1, g * p), ldt_r)


S5_TILE = 512
S5_SEQS = 4


def _s5_prompt_kernel(u0, u1, u2, u3, wb_ref, wcre_ref, wcim_ref, abre_ref, abim_ref, d_ref,
                      y_ref, sre_ref, sim_ref, u_tm, bu, y_tm, st):
    i = pl.program_id(1)
    half = BLOCK_CH // 2

    @pl.when(i == 0)
    def _():
        st[...] = jnp.zeros_like(st)

    for b, u in enumerate((u0, u1, u2, u3)):
        u_tm[0, pl.ds(b, S5_TILE, stride=S5_SEQS), :] = u[:, :half]
        u_tm[1, pl.ds(b, S5_TILE, stride=S5_SEQS), :] = u[:, half:]
    u_all = jnp.concatenate([u_tm[0], u_tm[1]], axis=1)
    bu[...] = _dot(u_all.astype(BF16), wb_ref[...])
    a_re = abre_ref[...]
    a_im = abim_ref[...]

    first = lax.broadcasted_iota(I32, (2 * S5_SEQS, BLOCK_ST), 0) < S5_SEQS

    def body(t2, carry):
        s_re, s_im = carry
        rows = pl.ds(pl.multiple_of(t2 * 2 * S5_SEQS, 2 * S5_SEQS), 2 * S5_SEQS)
        x_re = bu[rows, :BLOCK_ST]
        x_im = bu[rows, BLOCK_ST:]
        p_re = a_re * s_re - a_im * s_im + x_re
        p_im = a_re * s_im + a_im * s_re + x_im
        r_re = pltpu.roll(p_re, S5_SEQS, 0)
        r_im = pltpu.roll(p_im, S5_SEQS, 0)
        q_re = a_re * r_re - a_im * r_im + x_re
        q_im = a_re * r_im + a_im * r_re + x_im
        bu[rows, :BLOCK_ST] = jnp.where(first, p_re, q_re)
        bu[rows, BLOCK_ST:] = jnp.where(first, p_im, q_im)
        return pltpu.roll(q_re, S5_SEQS, 0), pltpu.roll(q_im, S5_SEQS, 0)

    s_re, s_im = lax.fori_loop(0, S5_TILE // 2, body, (st[:, :BLOCK_ST], st[:, BLOCK_ST:]), unroll=2)
    st[:, :BLOCK_ST] = s_re
    st[:, BLOCK_ST:] = s_im
    y = (_dot(bu[:, :BLOCK_ST].astype(BF16), wcre_ref[...])
         - _dot(bu[:, BLOCK_ST:].astype(BF16), wcim_ref[...]) + d_ref[...] * u_all)
    y = _gelu(y)
    y_tm[0] = y[:, :half]
    y_tm[1] = y[:, half:]
    for b in range(S5_SEQS):
        y_ref[b, :, :half] = y_tm[0, pl.ds(b, S5_TILE, stride=S5_SEQS), :]
        y_ref[b, :, half:] = y_tm[1, pl.ds(b, S5_TILE, stride=S5_SEQS), :]

    @pl.when(i == pl.num_programs(1) - 1)
    def _():
        sre_ref[...] = s_re[:S5_SEQS, :]
        sim_ref[...] = s_im[:S5_SEQS, :]


def _s5_prompt(z, wb, wcre, wcim, ab_re, ab_im, d_row, seq_len):
    nt = seq_len // S5_TILE

    def u_spec(b):
        return pl.BlockSpec((S5_TILE, BLOCK_CH), lambda j, i, b=b: (b * nt + i, j))

    rows = S5_TILE * S5_SEQS
    return pl.pallas_call(
        _s5_prompt_kernel,
        grid=(SSM_BLOCKS, nt),
        in_specs=[u_spec(b) for b in range(S5_SEQS)] + [
            pl.BlockSpec((None, BLOCK_CH, 2 * BLOCK_ST), lambda j, i: (j, 0, 0)),
            pl.BlockSpec((None, BLOCK_ST, BLOCK_CH), lambda j, i: (j, 0, 0)),
            pl.BlockSpec((None, BLOCK_ST, BLOCK_CH), lambda j, i: (j, 0, 0)),
            pl.BlockSpec((1, BLOCK_ST), lambda j, i: (0, j)),
            pl.BlockSpec((1, BLOCK_ST), lambda j, i: (0, j)),
            pl.BlockSpec((1, BLOCK_CH), lambda j, i: (0, j)),
        ],
        out_specs=[
            pl.BlockSpec((S5_SEQS, S5_TILE, BLOCK_CH), lambda j, i: (0, i, j)),
            pl.BlockSpec((S5_SEQS, BLOCK_ST), lambda j, i: (0, j)),
            pl.BlockSpec((S5_SEQS, BLOCK_ST), lambda j, i: (0, j)),
        ],
        out_shape=[
            jax.ShapeDtypeStruct((S5_SEQS, seq_len, SSM_WIDTH), F32),
            jax.ShapeDtypeStruct((S5_SEQS, SSM_GROUPS * SSM_STATE), F32),
            jax.ShapeDtypeStruct((S5_SEQS, SSM_GROUPS * SSM_STATE), F32),
        ],
        scratch_shapes=[
            pltpu.VMEM((2, rows, 128), F32),
            pltpu.VMEM((rows, 2 * BLOCK_ST), F32),
            pltpu.VMEM((2, rows, 128), F32),
            pltpu.VMEM((2 * S5_SEQS, 2 * BLOCK_ST), F32),
        ],
        compiler_params=_cparams(("parallel", "arbitrary")),
        name="s5_prompt",
    )(z, z, z, z, wb, wcre, wcim, ab_re, ab_im, d_row)


def _s5_sample_kernel(u_ref, s0re_ref, s0im_ref, wb_ref, wcre_ref, wcim_ref, abre_ref, abim_ref,
                      d_ref, y_ref, sre_ref, sim_ref, u_sl, y_sl):
    half = BLOCK_CH // 2
    n_seq = s0re_ref.shape[0]
    u_sl[0] = u_ref[:, :half]
    u_sl[1] = u_ref[:, half:]
    a_re = abre_ref[...]
    a_im = abim_ref[...]
    s_re = s0re_ref[...]
    s_im = s0im_ref[...]
    for t in range(SAMPLE_SEQ):
        rows = pl.ds(t, n_seq, stride=SAMPLE_SEQ)
        u_t = jnp.concatenate([u_sl[0, rows, :], u_sl[1, rows, :]], axis=1)
        bu = _dot(u_t.astype(BF16), wb_ref[...])
        n_re = a_re * s_re - a_im * s_im + bu[:, :BLOCK_ST]
        n_im = a_re * s_im + a_im * s_re + bu[:, BLOCK_ST:]
        s_re, s_im = n_re, n_im
        y = (_dot(s_re.astype(BF16), wcre_ref[...]) - _dot(s_im.astype(BF16), wcim_ref[...])
             + d_ref[...] * u_t)
        y = _gelu(y)
        y_sl[0, rows, :] = y[:, :half]
        y_sl[1, rows, :] = y[:, half:]
    y_ref[:, :half] = y_sl[0]
    y_ref[:, half:] = y_sl[1]
    sre_ref[...] = s_re
    sim_ref[...] = s_im


def _s5_sample(z, s0_re, s0_im, wb, wcre, wcim, ab_re, ab_im, d_row, row0, n_rows):
    n_seq = s0_re.shape[0]
    st_spec = pl.BlockSpec((n_seq, BLOCK_ST), lambda j: (0, j))
    return pl.pallas_call(
        _s5_sample_kernel,
        grid=(SSM_BLOCKS,),
        in_specs=[
            pl.BlockSpec((n_rows, BLOCK_CH), lambda j: (row0 // n_rows, j)),
            st_spec, st_spec,
            pl.BlockSpec((None, BLOCK_CH, 2 * BLOCK_ST), lambda j: (j, 0, 0)),
            pl.BlockSpec((None, BLOCK_ST, BLOCK_CH), lambda j: (j, 0, 0)),
            pl.BlockSpec((None, BLOCK_ST, BLOCK_CH), lambda j: (j, 0, 0)),
            pl.BlockSpec((1, BLOCK_ST), lambda j: (0, j)),
            pl.BlockSpec((1, BLOCK_ST), lambda j: (0, j)),
            pl.BlockSpec((1, BLOCK_CH), lambda j: (0, j)),
        ],
        out_specs=[pl.BlockSpec((n_rows, BLOCK_CH), lambda j: (0, j)), st_spec, st_spec],
        out_shape=[
            jax.ShapeDtypeStruct((n_rows, SSM_WIDTH), F32),
            jax.ShapeDtypeStruct(s0_re.shape, F32),
            jax.ShapeDtypeStruct(s0_im.shape, F32),
        ],
        scratch_shapes=[pltpu.VMEM((2, n_rows, 128), F32), pltpu.VMEM((2, n_rows, 128), F32)],
        compiler_params=_cparams(("parallel",)),
        name="s5_sample",
    )(z, s0_re, s0_im, wb, wcre, wcim, ab_re, ab_im, d_row)


def _postmix_kernel(hmp_ref, hms_ref, ysp_ref, yss_ref, xp_ref, xs_ref,
                    wglu_ref, bglu_ref, wout_ref, gain_ref, wq_ref, keys_ref,
                    h1_ref, c_ref, sp_ref, ss_ref, *, prompt_tiles):
    i = pl.program_id(0)
    ys = _group_pick(ysp_ref, yss_ref, prompt_tiles)
    hm = _group_pick(hmp_ref, hms_ref, prompt_tiles)
    glu = ys * _sigmoid(_dot(ys.astype(BF16), wglu_ref[...]) + bglu_ref[...])
    mix = (_dot(hm.astype(BF16), wout_ref[:MLSTM_WIDTH, :])
           + _dot(glu.astype(BF16), wout_ref[MLSTM_WIDTH:, :]))
    h1 = _group_pick(xp_ref, xs_ref, prompt_tiles) + mix
    h1_ref[...] = h1
    c = _rmsnorm(h1, gain_ref[...]).astype(BF16)
    c_ref[...] = c
    qp = _dot(c, wq_ref[...])
    scores = []
    width = KEY_GROUP * PEER_HALF
    for g in range(2 * PEER_HEADS // KEY_GROUP):
        sg = _dot_nt(keys_ref[g], qp[:, g * width:(g + 1) * width].astype(BF16))
        scores += [sg[jj * PEER_KEYS:(jj + 1) * PEER_KEYS, :] for jj in range(KEY_GROUP)]

    @pl.when(i < prompt_tiles)
    def _():
        for j, s in enumerate(scores):
            sp_ref[j] = s

    @pl.when(i >= prompt_tiles)
    def _():
        for j, s in enumerate(scores):
            ss_ref[j] = s


KEY_GROUP = 4


def _blockdiag_keys(keys):
    ng = keys.shape[0] // KEY_GROUP
    k = keys.reshape(ng, KEY_GROUP, PEER_KEYS, PEER_HALF)
    eye = jnp.eye(KEY_GROUP, dtype=bool)
    w = jnp.where(eye[None, :, None, :, None], k[:, :, :, None, :], 0.0)
    return w.reshape(ng, KEY_GROUP * PEER_KEYS, KEY_GROUP * PEER_HALF).astype(BF16)


def _postmix(hm_p, hm_s, ys_p, ys_s, x_p, x_s, wglu, bglu, wout, gain, wq, keys, tm=256):
    n_p, n_s = x_p.shape[0], x_s.shape[0]
    t = n_p + n_s
    pt = n_p // tm
    nk = 2 * PEER_HEADS
    return pl.pallas_call(
        functools.partial(_postmix_kernel, prompt_tiles=pt),
        grid=(t // tm,),
        in_specs=_group_specs(tm, MLSTM_WIDTH, pt) + _group_specs(tm, SSM_WIDTH, pt)
        + _group_specs(tm, D_MODEL, pt) + [
            _resident((SSM_WIDTH, SSM_WIDTH), lambda i: (0, 0)),
            _resident((1, SSM_WIDTH), lambda i: (0, 0)),
            _resident((D_MODEL, D_MODEL), lambda i: (0, 0)),
            _resident((1, D_MODEL), lambda i: (0, 0)),
            _resident((D_MODEL, PEER_HEADS * 2 * PEER_HALF), lambda i: (0, 0)),
            _resident(keys.shape, lambda i: (0, 0, 0)),
        ],
        out_specs=[
            pl.BlockSpec((tm, D_MODEL), lambda i: (i, 0)),
            pl.BlockSpec((tm, D_MODEL), lambda i: (i, 0)),
            pl.BlockSpec((nk, PEER_KEYS, tm), lambda i: (0, 0, jnp.minimum(i, pt - 1))),
            pl.BlockSpec((nk, PEER_KEYS, tm), lambda i: (0, 0, jnp.maximum(i - pt, 0))),
        ],
        out_shape=[
            jax.ShapeDtypeStruct((t, D_MODEL), F32),
            jax.ShapeDtypeStruct((t, D_MODEL), BF16),
            jax.ShapeDtypeStruct((nk, PEER_KEYS, n_p), F32),
            jax.ShapeDtypeStruct((nk, PEER_KEYS, n_s), F32),
        ],
        compiler_params=_cparams(("arbitrary",)),
        name="postmix",
    )(hm_p, hm_s, ys_p, ys_s, x_p, x_s, wglu, bglu, wout, gain, wq, keys)


ID_NONE = 1 << 20
TOPK_SUB = 8


def _sort_network(n):
    pairs = []
    p = 1
    while p < n:
        k = p
        while k >= 1:
            for j in range(k % p, n - k, 2 * k):
                for i in range(min(k, n - j - k)):
                    if (i + j) // (2 * p) == (i + j + k) // (2 * p):
                        pairs.append((i + j, i + j + k))
            k //= 2
        p *= 2
    return pairs


_SORT16 = _sort_network(PEER_TOPK)


def _precedes(b, a):
    (vb, ib), (va, ia) = b, a
    return jnp.logical_or(vb > va, jnp.logical_and(vb == va, ib < ia))


def _first_of(a, b):
    sw = _precedes(b, a)
    return jnp.where(sw, b[0], a[0]), jnp.where(sw, b[1], a[1])


def _exchange(items, i, j):
    a, b = items[i], items[j]
    sw = _precedes(b, a)
    items[i] = (jnp.where(sw, b[0], a[0]), jnp.where(sw, b[1], a[1]))
    items[j] = (jnp.where(sw, a[0], b[0]), jnp.where(sw, a[1], b[1]))


def _sort16(items):
    items = list(items)
    for i, j in _SORT16:
        _exchange(items, i, j)
    return items


def _bitonic_merge16(items):
    items = list(items)
    d = PEER_TOPK // 2
    while d >= 1:
        for i in range(PEER_TOPK):
            if i & d == 0:
                _exchange(items, i, i + d)
        d //= 2
    return items


def _merge_top16(a, b):
    return _bitonic_merge16([_first_of(a[i], b[PEER_TOPK - 1 - i]) for i in range(PEER_TOPK)])


def _top16_of_keys(s_ref, half, shape):
    best = None
    for g in range(PEER_KEYS // PEER_TOPK):
        grp = _sort16([(s_ref[half, g * PEER_TOPK + k], jnp.full(shape, g * PEER_TOPK + k, I32))
                       for k in range(PEER_TOPK)])
        best = grp if best is None else _merge_top16(best, grp)
    return best


def _route_head(s_ref):
    shape = s_ref.shape[2:]
    top1 = _top16_of_keys(s_ref, 0, shape)
    top2 = _top16_of_keys(s_ref, 1, shape)

    def pair(i, j):
        return top1[i][0] + top2[j][0], jnp.full(shape, i * PEER_TOPK + j, I32)

    pad = (jnp.full(shape, NEG_INF, F32), jnp.full(shape, ID_NONE, I32))
    g0 = [pair(0, j) for j in range(16)]
    g1 = _bitonic_merge16([pair(1, j) for j in range(8)] + [pair(i, 0) for i in range(15, 7, -1)])
    g2 = _sort16([pair(i, j) for i in range(2, 7) for j in range(PEER_TOPK // (i + 1))])
    g3 = [pair(7, 0), pair(7, 1)] + [pad] * 14
    best = _merge_top16(_merge_top16(g0, g1), _merge_top16(g2, g3))

    mx = best[0][0]
    e1s, e2s, exps = [], [], []
    for k in range(PEER_TOPK):
        v, pid = best[k]
        a = lax.shift_right_logical(pid, 4)
        b = jnp.bitwise_and(pid, PEER_TOPK - 1)
        e1 = jnp.zeros(shape, I32)
        e2 = jnp.zeros(shape, I32)
        for r in range(PEER_TOPK):
            e1 = jnp.where(a == r, top1[r][1], e1)
            e2 = jnp.where(b == r, top2[r][1], e2)
        e1s.append(e1)
        e2s.append(e2)
        exps.append(jnp.exp(v - mx))
    total = exps[0]
    for k in range(1, PEER_TOPK):
        total = total + exps[k]
    return e1s, e2s, [e / total for e in exps]


def _topk_kernel(s_ref, e1_ref, e2_ref, g_ref):
    sub = s_ref.shape[2]
    for k, vals in enumerate(zip(*_route_head(s_ref))):
        for ref, val in zip((e1_ref, e2_ref, g_ref), vals):
            for s in range(sub):
                ref[k:k + 1, s * 128:(s + 1) * 128] = val[s:s + 1, :]


def _topk(scores):
    t = scores.shape[-1]
    ng = t // 128
    sub = min(ng, TOPK_SUB)
    r = PEER_HEADS * PEER_TOPK
    o_spec = pl.BlockSpec((PEER_TOPK, sub * 128), lambda i, h: (h, i))
    return pl.pallas_call(
        _topk_kernel,
        grid=(ng // sub, PEER_HEADS),
        in_specs=[pl.BlockSpec((2, PEER_KEYS, sub, 128), lambda i, h: (h, 0, i, 0))],
        out_specs=[o_spec, o_spec, o_spec],
        out_shape=[
            jax.ShapeDtypeStruct((r, t), I32),
            jax.ShapeDtypeStruct((r, t), I32),
            jax.ShapeDtypeStruct((r, t), F32),
        ],
        compiler_params=_cparams(("parallel", "parallel")),
        name="topk",
    )(scores.reshape(2 * PEER_HEADS, PEER_KEYS, ng, 128))


WB_GROUP = 16
WB_PITCH = 132


def _token_weights(e1_row, e2_row, g_row):
    sub = lax.broadcasted_iota(I32, (PEER_KEYS, PEER_KEYS), 0)
    onehot1 = jnp.where(sub == e1_row, 1.0, 0.0).astype(BF16)
    gated2 = jnp.where(sub == e2_row, 0.5 * g_row, 0.0).astype(BF16)
    return _dot_nt(onehot1, gated2)


def _wbuild_kernel(e1p_ref, e1s_ref, e2p_ref, e2s_ref, gp_ref, gs_ref, w_ref, stage, *, prompt_tiles):
    e1 = jnp.transpose(_group_pick(e1p_ref, e1s_ref, prompt_tiles))
    e2 = jnp.transpose(_group_pick(e2p_ref, e2s_ref, prompt_tiles))
    g = jnp.transpose(_group_pick(gp_ref, gs_ref, prompt_tiles))
    for grp in range(e1.shape[0] // WB_GROUP):
        base = (grp % 2) * WB_GROUP * WB_PITCH
        for tt in range(WB_GROUP):
            t = grp * WB_GROUP + tt
            stage[base + tt * WB_PITCH:base + tt * WB_PITCH + PEER_KEYS, :] = _token_weights(
                e1[t:t + 1, :], e2[t:t + 1, :], g[t:t + 1, :])
        for e in range(PEER_KEYS):
            blk = stage[pl.ds(base + e, WB_GROUP, stride=WB_PITCH), :]
            w_ref[e, grp * WB_GROUP:(grp + 1) * WB_GROUP, :] = blk.astype(BF16)


def _wbuild(routing_p, routing_s, tw=256):
    n_p, n_s = routing_p[0].shape[1], routing_s[0].shape[1]
    t = n_p + n_s
    pt = n_p // tw
    specs = [
        pl.BlockSpec((PEER_KEYS, tw), lambda i: (0, jnp.minimum(i, pt - 1))),
        pl.BlockSpec((PEER_KEYS, tw), lambda i: (0, jnp.maximum(i - pt, 0))),
    ]
    operands = [x for pair in zip(routing_p, routing_s) for x in pair]
    return pl.pallas_call(
        functools.partial(_wbuild_kernel, prompt_tiles=pt),
        grid=(t // tw,),
        in_specs=specs * 3,
        out_specs=pl.BlockSpec((PEER_KEYS, tw, PEER_KEYS), lambda i: (0, i, 0)),
        out_shape=jax.ShapeDtypeStruct((PEER_KEYS, t, PEER_KEYS), BF16),
        scratch_shapes=[pltpu.VMEM((2 * WB_GROUP * WB_PITCH, PEER_KEYS), F32)],
        compiler_params=_cparams(("arbitrary",)),
        name="wbuild",
    )(*operands)


PEER_EB = 256


def _peer_kernel(c_ref, u_ref, v_ref, w_ref, o_ref, s_s):
    j = pl.program_id(1)
    last = pl.num_programs(1) - 1

    def gated():
        parts = []
        for k in range(PEER_EB // PEER_KEYS):
            x = s_s[:, k * PEER_KEYS:(k + 1) * PEER_KEYS]
            t = jnp.tanh(x * (GELU_C1 + GELU_C2 * (x * x)))
            parts.append((x * w_ref[k].astype(F32)) * (1.0 + t))
        return jnp.concatenate(parts, axis=1).astype(BF16)

    def scores():
        return _dot_nt(c_ref[...], u_ref[...].astype(BF16))

    @pl.when(j == 0)
    def _():
        o_ref[...] = jnp.zeros_like(o_ref)
        s_s[...] = scores()

    @pl.when(jnp.logical_and(j > 0, j < last))
    def _():
        wact = gated()
        s_s[...] = scores()
        o_ref[...] += _dot(wact, v_ref[...].astype(BF16))

    @pl.when(j == last)
    def _():
        o_ref[...] += _dot(gated(), v_ref[...].astype(BF16))


def _peer(c, u, v, w3, n_tiles=4):
    t = c.shape[0]
    tm = t // n_tiles
    nb = PEER_EXPERTS // PEER_EB
    return pl.pallas_call(
        _peer_kernel,
        grid=(n_tiles, nb + 1),
        in_specs=[
            pl.BlockSpec((tm, D_MODEL), lambda i, j: (i, 0), pipeline_mode=pl.Buffered(1)),
            pl.BlockSpec((PEER_EB, D_MODEL), lambda i, j: (jnp.minimum(j, nb - 1), 0)),
            pl.BlockSpec((PEER_EB, D_MODEL), lambda i, j: (jnp.maximum(j - 1, 0), 0)),
            pl.BlockSpec((PEER_EB // PEER_KEYS, tm, PEER_KEYS),
                         lambda i, j: (jnp.maximum(j - 1, 0), i, 0)),
        ],
        out_specs=pl.BlockSpec((tm, D_MODEL), lambda i, j: (i, 0), pipeline_mode=pl.Buffered(1)),
        out_shape=jax.ShapeDtypeStruct((t, D_MODEL), F32),
        scratch_shapes=[pltpu.VMEM((tm, PEER_EB), F32)],
        compiler_params=_cparams(("parallel", "arbitrary")),
        name="peer",
    )(c, u, v, w3)


def _tail_kernel(h1_ref, peer_ref, pp_ref, ps_ref, gple_ref, wgate_ref, wproj_ref, gfin_ref,
                 yp_ref, ys_ref, *, prompt_tiles):
    i = pl.program_id(0)
    h2 = h1_ref[...] + peer_ref[...]
    gate = _sigmoid(_dot(_rmsnorm(h2, gple_ref[...]).astype(BF16), wgate_ref[...]))
    e = _dot(_group_pick(pp_ref, ps_ref, prompt_tiles).astype(BF16), wproj_ref[...])
    y = _rmsnorm(h2 + e * gate, gfin_ref[...])

    @pl.when(i < prompt_tiles)
    def _():
        yp_ref[...] = y

    @pl.when(i >= prompt_tiles)
    def _():
        ys_ref[...] = y


def _tail(h1, peer, p_p, p_s, gple, wgate, wproj, gfin, tm=512):
    t = h1.shape[0]
    n_prompt = p_p.shape[0]
    pt = n_prompt // tm
    ple = p_p.shape[1]
    return pl.pallas_call(
        functools.partial(_tail_kernel, prompt_tiles=pt),
        grid=(t // tm,),
        in_specs=[
            pl.BlockSpec((tm, D_MODEL), lambda i: (i, 0)),
            pl.BlockSpec((tm, D_MODEL), lambda i: (i, 0)),
        ] + _group_specs(tm, ple, pt) + [
            _resident((1, D_MODEL), lambda i: (0, 0)),
            _resident((D_MODEL, D_MODEL), lambda i: (0, 0)),
            _resident((ple, D_MODEL), lambda i: (0, 0)),
            _resident((1, D_MODEL), lambda i: (0, 0)),
        ],
        out_specs=[
            pl.BlockSpec((tm, D_MODEL), lambda i: (jnp.minimum(i, pt - 1), 0)),
            pl.BlockSpec((tm, D_MODEL), lambda i: (jnp.maximum(i - pt, 0), 0)),
        ],
        out_shape=[
            jax.ShapeDtypeStruct((n_prompt, D_MODEL), F32),
            jax.ShapeDtypeStruct((t - n_prompt, D_MODEL), F32),
        ],
        compiler_params=_cparams(("arbitrary",)),
        name="tail",
    )(h1, peer, p_p, p_s, gple, wgate, wproj, gfin)


def kernel(x_prompt, x_sample, state_mlstm_C, state_mlstm_n, state_mlstm_m, state_ssm_re, state_ssm_im, p_prompt, p_sample, norm_mix, w_in, b_igate, b_fgate, mlstm_norm, ssm_A_re, ssm_A_im, ssm_B_re, ssm_B_im, ssm_C_re, ssm_C_im, ssm_D, ssm_log_dt, w_glu, b_glu, w_out, norm_ffn, peer_w_q, peer_keys, peer_u, peer_v, norm_ple, w_ple_gate, w_ple_proj, norm_final):
    n_pseq, p_len, _ = x_prompt.shape
    n_sseq, s_len, _ = x_sample.shape
    assert s_len == SAMPLE_SEQ and n_pseq == S5_SEQS and w_in.shape[0] == 1
    n_prompt = n_pseq * p_len
    n_sample = n_sseq * s_len
    row = lambda t: t.reshape(1, -1)

    x_p = x_prompt.reshape(n_prompt, D_MODEL)
    x_s = x_sample.reshape(n_sample, D_MODEL)

    w_bf = w_in[0].astype(BF16)
    n_gate = 2 * HEADS
    w_tail = jnp.concatenate(
        [w_bf[:, Z_U + n_gate:], w_bf[:, Z_U:Z_U + n_gate],
         jnp.zeros((D_MODEL, 128 - n_gate), BF16)], axis=1)
    gbias = jnp.concatenate([b_igate[0], b_fgate[0], jnp.zeros((128 - n_gate,), F32)]).reshape(1, 128)
    z, zt = _in_proj(x_p, x_s, row(norm_mix[0]), w_bf, w_tail, gbias)

    gain_m = row(mlstm_norm[0])
    hm_p, c_p, n_p, m_p = _mlstm_prompt(z, zt, gain_m, n_pseq, p_len)
    nbh = n_sseq * HEADS
    hm_s, c_s, n_s, m_s = _mlstm_sample(
        z, zt, gain_m,
        state_mlstm_C[0].reshape(nbh, HEAD_DIM, HEAD_DIM),
        state_mlstm_n[0].reshape(nbh, 1, HEAD_DIM),
        jnp.broadcast_to(state_mlstm_m[0].reshape(nbh, 1, 1), (nbh, 1, 128)),
        n_prompt, n_sample)

    wb, wcre, wcim, ab_re, ab_im = _s5_params(ssm_A_re[0], ssm_A_im[0], ssm_log_dt[0], ssm_B_re[0],
                                              ssm_B_im[0], ssm_C_re[0], ssm_C_im[0])
    d_row = row(ssm_D[0])
    ys_p, sre_p, sim_p = _s5_prompt(zt, wb, wcre, wcim, ab_re, ab_im, d_row, p_len)
    n_st = SSM_GROUPS * SSM_STATE
    ys_s, sre_s, sim_s = _s5_sample(
        zt, state_ssm_re[0].reshape(n_sseq, n_st), state_ssm_im[0].reshape(n_sseq, n_st),
        wb, wcre, wcim, ab_re, ab_im, d_row, n_prompt, n_sample)

    keys = _blockdiag_keys(peer_keys[0].reshape(2 * PEER_HEADS, PEER_KEYS, PEER_HALF))
    h1, c, scores_p, scores_s = _postmix(
        hm_p, hm_s, ys_p.reshape(n_prompt, SSM_WIDTH), ys_s, x_p, x_s,
        w_glu[0].astype(BF16), row(b_glu[0]), w_out[0].astype(BF16),
        row(norm_ffn[0]), peer_w_q[0].astype(BF16), keys)
    w3 = _wbuild(_topk(scores_p), _topk(scores_s))
    peer = _peer(c, peer_u[0], peer_v[0], w3)

    y_p, y_s = _tail(h1, peer, p_prompt[0].reshape(n_prompt, -1), p_sample[0].reshape(n_sample, -1),
                     row(norm_ple[0]), w_ple_gate[0].astype(BF16), w_ple_proj[0].astype(BF16),
                     row(norm_final))

    st_shape = (1, -1, SSM_GROUPS, SSM_STATE)
    return (
        y_p.reshape(x_prompt.shape), y_s.reshape(x_sample.shape),
        c_p.reshape(1, n_pseq, HEADS, HEAD_DIM, HEAD_DIM), n_p.reshape(1, n_pseq, HEADS, HEAD_DIM),
        m_p[:, 0, 0].reshape(1, n_pseq, HEADS),
        sre_p.reshape(st_shape), sim_p.reshape(st_shape),
        c_s.reshape(1, n_sseq, HEADS, HEAD_DIM, HEAD_DIM), n_s.reshape(1, n_sseq, HEADS, HEAD_DIM),
        m_s[:, 0, 0].reshape(1, n_sseq, HEADS),
        sre_s.reshape(st_shape), sim_s.reshape(st_shape),
    )
```

```python
import functools

import jax
import jax.numpy as jnp
from jax import lax
from jax.experimental import pallas as pl
from jax.experimental.pallas import tpu as pltpu

F32 = jnp.float32
BF16 = jnp.bfloat16
I32 = jnp.int32

EPS = 1e-6
D_MODEL = 2048
HEADS = 4
HEAD_DIM = 256
MLSTM_WIDTH = HEADS * HEAD_DIM
SSM_WIDTH = 1024
SSM_GROUPS = 64
SSM_STATE = 64
SSM_GROUP = 16
GROUPS_PER_BLOCK = 16
SSM_BLOCKS = SSM_GROUPS // GROUPS_PER_BLOCK
BLOCK_CH = GROUPS_PER_BLOCK * SSM_GROUP
BLOCK_ST = GROUPS_PER_BLOCK * SSM_STATE
PEER_HEADS = 8
PEER_KEYS = 128
PEER_TOPK = 16
PEER_HALF = 64
PEER_EXPERTS = PEER_KEYS * PEER_KEYS

Z_Q, Z_K, Z_V, Z_O, Z_U = 0, 1024, 2048, 3072, 4096
ZT_G = 1024
ZT_COLS = ZT_G + 128
IN_PROJ_COLS = 2048

MLSTM_CHUNK = 256
K_SCALE = HEAD_DIM ** -0.5
NEG_INF = float("-inf")

VMEM_LIMIT = 56 * 1024 * 1024


def _cparams(semantics):
    return pltpu.CompilerParams(dimension_semantics=semantics, vmem_limit_bytes=VMEM_LIMIT)


def _resident(shape, index_map):
    return pl.BlockSpec(shape, index_map, pipeline_mode=pl.Buffered(1))


def _sigmoid(x):
    return 1.0 / (1.0 + jnp.exp(-x))


GELU_C1 = 0.7978845608028654
GELU_C2 = GELU_C1 * 0.044715


def _gelu(x):
    return x * (0.5 * (1.0 + jnp.tanh(GELU_C1 * (x + 0.044715 * (x * x * x)))))


def _rmsnorm(x, gain):
    return x * lax.rsqrt(jnp.mean(x * x, axis=-1, keepdims=True) + EPS) * gain


def _dot(a, b):
    return jnp.dot(a, b, preferred_element_type=F32)


def _dot_nt(a, b):
    return lax.dot_general(a, b, (((1,), (1,)), ((), ())), preferred_element_type=F32)


def _dot_tn(a, b):
    return lax.dot_general(a, b, (((0,), (0,)), ((), ())), preferred_element_type=F32)


def _group_specs(tm, width, prompt_tiles):
    return [
        pl.BlockSpec((tm, width), lambda i: (jnp.minimum(i, prompt_tiles - 1), 0)),
        pl.BlockSpec((tm, width), lambda i: (jnp.maximum(i - prompt_tiles, 0), 0)),
    ]


def _group_pick(prompt_ref, sample_ref, prompt_tiles):
    return jnp.where(pl.program_id(0) < prompt_tiles, prompt_ref[...], sample_ref[...])


def _in_proj_kernel(xp_ref, xs_ref, gain_ref, wm_ref, wt_ref, gbias_ref, zm_ref, zt_ref, a_s,
                    *, prompt_tiles, main_steps):
    j = pl.program_id(1)

    @pl.when(j == 0)
    def _():
        a_s[...] = _rmsnorm(_group_pick(xp_ref, xs_ref, prompt_tiles), gain_ref[...]).astype(BF16)

    for step in range(main_steps):
        @pl.when(j == step)
        def _(base=step * IN_PROJ_COLS):
            for c0 in range(0, IN_PROJ_COLS, 512):
                zm_ref[:, c0:c0 + 512] = _dot(a_s[...], wm_ref[:, base + c0:base + c0 + 512])

    @pl.when(j == main_steps)
    def _():
        a = a_s[...]
        for c0 in range(0, ZT_G, 512):
            zt_ref[:, c0:c0 + 512] = _dot(a, wt_ref[:, c0:c0 + 512])
        gz = _dot(a, wt_ref[:, ZT_G:]) + gbias_ref[...]
        lane = lax.broadcasted_iota(I32, gz.shape, 1)
        log_f = jnp.minimum(gz, 0.0) - jnp.log(1.0 + jnp.exp(-jnp.abs(gz)))
        zt_ref[:, ZT_G:] = jnp.where(lane < HEADS, gz, log_f)


def _in_proj(x_p, x_s, gain, w_bf, w_tail, gbias, tm=512):
    t = x_p.shape[0] + x_s.shape[0]
    pt = x_p.shape[0] // tm
    ms = Z_U // IN_PROJ_COLS
    return pl.pallas_call(
        functools.partial(_in_proj_kernel, prompt_tiles=pt, main_steps=ms),
        grid=(t // tm, ms + 1),
        in_specs=[
            pl.BlockSpec((tm, D_MODEL), lambda i, j: (jnp.minimum(i, pt - 1), 0)),
            pl.BlockSpec((tm, D_MODEL), lambda i, j: (jnp.maximum(i - pt, 0), 0),
                         pipeline_mode=pl.Buffered(1)),
            _resident((1, D_MODEL), lambda i, j: (0, 0)),
            _resident((D_MODEL, Z_U), lambda i, j: (0, 0)),
            _resident((D_MODEL, ZT_COLS), lambda i, j: (0, 0)),
            _resident((1, 128), lambda i, j: (0, 0)),
        ],
        out_specs=[
            pl.BlockSpec((tm, IN_PROJ_COLS), lambda i, j: (i, jnp.minimum(j, ms - 1))),
            pl.BlockSpec((tm, ZT_COLS), lambda i, j: (i, 0)),
        ],
        out_shape=[
            jax.ShapeDtypeStruct((t, Z_U), F32),
            jax.ShapeDtypeStruct((t, ZT_COLS), F32),
        ],
        scratch_shapes=[pltpu.VMEM((tm, D_MODEL), BF16)],
        compiler_params=_cparams(("arbitrary", "arbitrary")),
        name="in_proj",
    )(x_p, x_s, gain, w_bf, w_tail, gbias)


def _mlstm_chunks(items):
    L = items[0][0].shape[0]
    row = lax.broadcasted_iota(I32, (L, L), 0)
    col = lax.broadcasted_iota(I32, (L, L), 1)
    diag = row == col
    causal = col <= row

    gates = []
    for (_, _, _, ig_col, lf_col, _, _, m_state) in items:
        lf_row = jnp.sum(jnp.where(diag, lf_col, 0.0), axis=0, keepdims=True)
        ig_row = jnp.sum(jnp.where(diag, ig_col, 0.0), axis=0, keepdims=True)
        b_col = jnp.sum(jnp.where(causal, lf_row, 0.0), axis=1, keepdims=True)
        b_row = jnp.sum(jnp.where(row <= col, lf_col, 0.0), axis=0, keepdims=True)
        r_row = ig_row - b_row
        r_col = ig_col - b_col
        run_max = jnp.max(jnp.where(causal, r_row, NEG_INF), axis=1, keepdims=True)
        m_run = jnp.maximum(m_state, run_max)
        w_intra = jnp.exp(jnp.where(causal, r_row - m_run, NEG_INF))
        w_inter = jnp.exp(m_state - m_run)
        gates.append((b_col, r_col, m_run, w_intra, w_inter))

    products = []
    for (q, k, v, _, _, c_get, _, _) in items:
        qb = q.astype(BF16)
        vb = v.astype(BF16)
        products.append((_dot_nt(qb, k.astype(BF16)), _dot(qb, c_get().astype(BF16)), vb))

    heads = []
    for (q, _, _, _, _, _, n_state, _), (b_col, _, m_run, w_intra, w_inter), (qk, qc, vb) in zip(
            items, gates, products):
        s = qk * w_intra
        num = w_inter * qc + _dot(s.astype(BF16), vb)
        qn = jnp.sum(q * n_state, axis=1, keepdims=True)
        den = w_inter * qn + jnp.sum(s, axis=1, keepdims=True)
        heads.append(num / jnp.maximum(jnp.abs(den), jnp.exp(-(b_col + m_run))))

    out = []
    for (_, k, _, _, _, c_get, n_state, m_state), (b_col, r_col, m_run, _, _), (_, _, vb), hb in zip(
            items, gates, products, heads):
        m_last = m_run[L - 1:L, :]
        m_new = b_col[L - 1:L, :] + m_last
        kw = k * jnp.exp(r_col - m_last)
        sc = jnp.exp(m_state - m_last)
        c_new = sc * c_get() + _dot_tn(kw.astype(BF16), vb)
        n_new = sc * n_state + jnp.sum(kw, axis=0, keepdims=True)
        out.append((hb, c_new, n_new, m_new))
    return out


def _head_out(hb, gain, o_pre):
    return hb * lax.rsqrt(jnp.mean(hb * hb, axis=-1, keepdims=True) + EPS) * gain * _sigmoid(o_pre)


def _mlstm_prompt_kernel(q_ref, k_ref, v_ref, o_ref, g_ref, gain_ref,
                         hm_ref, c_out_ref, n_out_ref, m_out_ref, c_s, n_s, m_s):
    h0 = pl.program_id(1) * PROMPT_HEADS
    c_s[...] = jnp.zeros_like(c_s)
    n_s[...] = jnp.zeros_like(n_s)
    m_s[...] = jnp.zeros_like(m_s)

    def body(ci, carry):
        rows = pl.ds(pl.multiple_of(ci * MLSTM_CHUNK, MLSTM_CHUNK), MLSTM_CHUNK)
        gates = g_ref[rows, :]
        lane = lax.broadcasted_iota(I32, gates.shape, 1)
        items = []
        for hh in range(PROMPT_HEADS):
            cs = slice(hh * HEAD_DIM, (hh + 1) * HEAD_DIM)
            ig_col = jnp.sum(jnp.where(lane == h0 + hh, gates, 0.0), axis=1, keepdims=True)
            lf_col = jnp.sum(jnp.where(lane == h0 + hh + HEADS, gates, 0.0), axis=1, keepdims=True)
            items.append((q_ref[rows, cs], k_ref[rows, cs] * K_SCALE, v_ref[rows, cs], ig_col, lf_col,
                          functools.partial(lambda r, i: r[i], c_s, hh), n_s[hh], m_s[hh][:, 0:1]))
        for hh, (hb, c_new, n_new, m_new) in enumerate(_mlstm_chunks(items)):
            cs = slice(hh * HEAD_DIM, (hh + 1) * HEAD_DIM)
            c_s[hh] = c_new
            n_s[hh] = n_new
            m_s[hh] = jnp.broadcast_to(m_new, (1, 128))
            hm_ref[rows, cs] = _head_out(hb, gain_ref[:, cs], o_ref[rows, cs])
        return carry

    lax.fori_loop(0, q_ref.shape[0] // MLSTM_CHUNK, body, 0)
    c_out_ref[...] = c_s[...]
    n_out_ref[...] = n_s[...]
    m_out_ref[...] = m_s[...]


PROMPT_HEADS = 2


def _mlstm_prompt(z, zt, gain, n_seq, seq_len):
    width = PROMPT_HEADS * HEAD_DIM
    hsteps = HEADS // PROMPT_HEADS

    def sec(off):
        return pl.BlockSpec((seq_len, width), lambda b, h, o=off // width: (b, o + h))

    def state(*tail):
        return pl.BlockSpec((PROMPT_HEADS,) + tail, lambda b, h: (b * hsteps + h,) + (0,) * len(tail))

    nbh = n_seq * HEADS
    return pl.pallas_call(
        _mlstm_prompt_kernel,
        grid=(n_seq, hsteps),
        in_specs=[
            sec(Z_Q), sec(Z_K), sec(Z_V), sec(Z_O),
            pl.BlockSpec((seq_len, 128), lambda b, h: (b, ZT_G // 128)),
            pl.BlockSpec((1, width), lambda b, h: (0, h)),
        ],
        out_specs=[
            pl.BlockSpec((seq_len, width), lambda b, h: (b, h)),
            state(HEAD_DIM, HEAD_DIM), state(1, HEAD_DIM), state(1, 128),
        ],
        out_shape=[
            jax.ShapeDtypeStruct((n_seq * seq_len, MLSTM_WIDTH), F32),
            jax.ShapeDtypeStruct((nbh, HEAD_DIM, HEAD_DIM), F32),
            jax.ShapeDtypeStruct((nbh, 1, HEAD_DIM), F32),
            jax.ShapeDtypeStruct((nbh, 1, 128), F32),
        ],
        scratch_shapes=[
            pltpu.VMEM((PROMPT_HEADS, HEAD_DIM, HEAD_DIM), F32),
            pltpu.VMEM((PROMPT_HEADS, 1, HEAD_DIM), F32),
            pltpu.VMEM((PROMPT_HEADS, 1, 128), F32),
        ],
        compiler_params=_cparams(("parallel", "parallel")),
        name="mlstm_prompt",
    )(z, z, z, z, zt, gain)


SAMPLE_SEQ = 4
SAMPLE_ROWS = 16


def _mlstm_sample_kernel(z_ref, g_ref, gain_ref, c_in_ref, n_in_ref, m_in_ref,
                         hm_ref, c_out_ref, n_out_ref, m_out_ref):
    row = lax.broadcasted_iota(I32, (8, 1), 0)
    for grp in range(SAMPLE_ROWS // 8):
        r0 = grp * 8
        gates = g_ref[r0:r0 + 8, :]
        items, where = [], []
        for half in range(2):
            lo = half * SAMPLE_SEQ
            live = jnp.logical_and(row >= lo, row < lo + SAMPLE_SEQ)
            for h in range(HEADS):
                j = (grp * 2 + half) * HEADS + h
                q = z_ref[r0:r0 + 8, Z_Q + h * HEAD_DIM:Z_Q + (h + 1) * HEAD_DIM]
                k = z_ref[r0:r0 + 8, Z_K + h * HEAD_DIM:Z_K + (h + 1) * HEAD_DIM]
                v = z_ref[r0:r0 + 8, Z_V + h * HEAD_DIM:Z_V + (h + 1) * HEAD_DIM]
                k = jnp.where(live, k * K_SCALE, 0.0)
                v = jnp.where(live, v, 0.0)
                ig_col = jnp.where(live, gates[:, h:h + 1], NEG_INF)
                lf_col = jnp.where(live, gates[:, HEADS + h:HEADS + h + 1], 0.0)
                items.append((q, k, v, ig_col, lf_col, functools.partial(lambda r, i: r[i], c_in_ref, j),
                              n_in_ref[j], m_in_ref[j][:, 0:1]))
                where.append((j, lo, h))
        for (j, lo, h), (hb, c_new, n_new, m_new) in zip(where, _mlstm_chunks(items)):
            cs = slice(h * HEAD_DIM, (h + 1) * HEAD_DIM)
            o = z_ref[r0:r0 + 8, Z_O + h * HEAD_DIM:Z_O + (h + 1) * HEAD_DIM]
            out = _head_out(hb, gain_ref[:, cs], o)
            hm_ref[r0 + lo:r0 + lo + SAMPLE_SEQ, cs] = out[lo:lo + SAMPLE_SEQ, :]
            c_out_ref[j] = c_new
            n_out_ref[j] = n_new
            m_out_ref[j] = jnp.broadcast_to(m_new, (1, 128))


def _mlstm_sample(z, zt, gain, c0, n0, m0, row0, n_rows):
    nbh = c0.shape[0]
    per = SAMPLE_ROWS // SAMPLE_SEQ * HEADS
    rb0 = row0 // SAMPLE_ROWS
    return pl.pallas_call(
        _mlstm_sample_kernel,
        grid=(n_rows // SAMPLE_ROWS,),
        in_specs=[
            pl.BlockSpec((SAMPLE_ROWS, Z_U), lambda i: (rb0 + i, 0)),
            pl.BlockSpec((SAMPLE_ROWS, 128), lambda i: (rb0 + i, ZT_G // 128)),
            pl.BlockSpec((1, MLSTM_WIDTH), lambda i: (0, 0)),
            pl.BlockSpec((per, HEAD_DIM, HEAD_DIM), lambda i: (i, 0, 0)),
            pl.BlockSpec((per, 1, HEAD_DIM), lambda i: (i, 0, 0)),
            pl.BlockSpec((per, 1, 128), lambda i: (i, 0, 0)),
        ],
        out_specs=[
            pl.BlockSpec((SAMPLE_ROWS, MLSTM_WIDTH), lambda i: (i, 0)),
            pl.BlockSpec((per, HEAD_DIM, HEAD_DIM), lambda i: (i, 0, 0)),
            pl.BlockSpec((per, 1, HEAD_DIM), lambda i: (i, 0, 0)),
            pl.BlockSpec((per, 1, 128), lambda i: (i, 0, 0)),
        ],
        out_shape=[
            jax.ShapeDtypeStruct((n_rows, MLSTM_WIDTH), F32),
            jax.ShapeDtypeStruct((nbh, HEAD_DIM, HEAD_DIM), F32),
            jax.ShapeDtypeStruct((nbh, 1, HEAD_DIM), F32),
            jax.ShapeDtypeStruct((nbh, 1, 128), F32),
        ],
        compiler_params=_cparams(("parallel",)),
        name="mlstm_sample",
    )(z, zt, gain, c0, n0, m0)


def _s5_discretise(a_re, a_im, log_dt):
    dt = jnp.exp(log_dt)
    mag = jnp.exp(dt * a_re)
    ab_re = mag * jnp.cos(dt * a_im)
    ab_im = mag * jnp.sin(dt * a_im)
    den = a_re * a_re + a_im * a_im
    xr = ab_re - 1.0
    f_re = (xr * a_re + ab_im * a_im) / den
    f_im = (ab_im * a_re - xr * a_im) / den
    return ab_re, ab_im, f_re, f_im


def _s5_param_kernel(are, aim, ldt, bre, bim, cre, cim, are_r, aim_r, ldt_r,
                     wb_o, wcre_o, wcim_o, ab_re_o, ab_im_o):
    _, _, f_re, f_im = _s5_discretise(are[...], aim[...], ldt[...])
    bb = (f_re * bre[...] - f_im * bim[...], f_re * bim[...] + f_im * bre[...])
    row_g = lax.shift_right_logical(lax.broadcasted_iota(I32, (BLOCK_CH, BLOCK_ST), 0), 4)
    lane_g = lax.shift_right_logical(lax.broadcasted_iota(I32, (BLOCK_CH, BLOCK_ST), 1), 6)
    keep = row_g == lane_g

    def spread(src):
        return jnp.where(keep, jnp.concatenate([src] * (BLOCK_ST // 128), axis=1), 0.0).astype(BF16)

    for j in range(SSM_BLOCKS):
        rows = slice(j * BLOCK_CH, (j + 1) * BLOCK_CH)
        for ri in range(2):
            wb_o[j, :, ri * BLOCK_ST:(ri + 1) * BLOCK_ST] = spread(bb[ri][rows, :])
        wcre_o[j] = spread(cre[rows, :])
        wcim_o[j] = spread(cim[rows, :])
    ab_re, ab_im, _, _ = _s5_discretise(are_r[...], aim_r[...], ldt_r[...])
    ab_re_o[...] = ab_re
    ab_im_o[...] = ab_im


def _s5_params(a_re, a_im, log_dt, b_re, b_im, c_re, c_im):
    g, p = a_re.shape
    c = b_re.shape[-1]
    dup = lambda t: jnp.concatenate([t] * (128 // t.shape[-1]), axis=-1)
    by_channel = lambda t: dup(jnp.repeat(t, c, axis=0))
    b_rows = lambda t: dup(jnp.transpose(t, (0, 2, 1)).reshape(g * c, p))
    c_rows = lambda t: dup(t.reshape(g * c, p))
    ldt =jnp.broadcast_to(jnp.repeat(log_dt, c)[:, None], (g * c, 128))
    ldt_r = jnp.broadcast_to(log_dt[:, None], (g, p)).reshape(1, g * p)
    rowv = jax.ShapeDtypeStruct((1, g * p), F32)
    return pl.pallas_call(
        _s5_param_kernel,
        out_shape=[
            jax.ShapeDtypeStruct((SSM_BLOCKS, BLOCK_CH, 2 * BLOCK_ST), BF16),
            jax.ShapeDtypeStruct((SSM_BLOCKS, BLOCK_CH, BLOCK_ST), BF16),
            jax.ShapeDtypeStruct((SSM_BLOCKS, BLOCK_CH, BLOCK_ST), BF16),
            rowv, rowv,
        ],
        compiler_params=pltpu.CompilerParams(vmem_limit_bytes=VMEM_LIMIT),
        name="s5_params",
    )(by_channel(a_re), by_channel(a_im), ldt, b_rows(b_re), b_rows(b_im), c_rows(c_re), c_rows(c_im),
      a_re.reshape(1, g * p), a_im.reshape(1, g * p), ldt_r)


S5_TILE = 512
S5_SEQS = 4


def _s5_prompt_kernel(u0, u1, u2, u3, wb_ref, wcre_ref, wcim_ref, abre_ref, abim_ref, d_ref,
                      y_ref, sre_ref, sim_ref, u_tm, bu, y_tm, st):
    i = pl.program_id(1)
    half = BLOCK_CH // 2

    @pl.when(i == 0)
    def _():
        st[...] = jnp.zeros_like(st)

    for b, u in enumerate((u0, u1, u2, u3)):
        u_tm[0, pl.ds(b, S5_TILE, stride=S5_SEQS), :] = u[:, :half]
        u_tm[1, pl.ds(b, S5_TILE, stride=S5_SEQS), :] = u[:, half:]
    u_all = jnp.concatenate([u_tm[0], u_tm[1]], axis=1)
    bu[...] = _dot(u_all.astype(BF16), wb_ref[...])
    a_re = abre_ref[...]
    a_im = abim_ref[...]

    first = lax.broadcasted_iota(I32, (2 * S5_SEQS, BLOCK_ST), 0) < S5_SEQS

    def body(t2, carry):
        s_re, s_im = carry
        rows = pl.ds(pl.multiple_of(t2 * 2 * S5_SEQS, 2 * S5_SEQS), 2 * S5_SEQS)
        x_re = bu[rows, :BLOCK_ST]
        x_im = bu[rows, BLOCK_ST:]
        p_re = a_re * s_re - a_im * s_im + x_re
        p_im = a_re * s_im + a_im * s_re + x_im
        r_re = pltpu.roll(p_re, S5_SEQS, 0)
        r_im = pltpu.roll(p_im, S5_SEQS, 0)
        q_re = a_re * r_re - a_im * r_im + x_re
        q_im = a_re * r_im + a_im * r_re + x_im
        bu[rows, :BLOCK_ST] = jnp.where(first, p_re, q_re)
        bu[rows, BLOCK_ST:] = jnp.where(first, p_im, q_im)
        return pltpu.roll(q_re, S5_SEQS, 0), pltpu.roll(q_im, S5_SEQS, 0)

    s_re, s_im = lax.fori_loop(0, S5_TILE // 2, body, (st[:, :BLOCK_ST], st[:, BLOCK_ST:]), unroll=2)
    st[:, :BLOCK_ST] = s_re
    st[:, BLOCK_ST:] = s_im
    y = (_dot_nt(bu[:, :BLOCK_ST].astype(BF16), wcre_ref[...])
         - _dot_nt(bu[:, BLOCK_ST:].astype(BF16), wcim_ref[...]) + d_ref[...] * u_all)
    y = _gelu(y)
    y_tm[0] = y[:, :half]
    y_tm[1] = y[:, half:]
    for b in range(S5_SEQS):
        y_ref[b, :, :half] = y_tm[0, pl.ds(b, S5_TILE, stride=S5_SEQS), :]
        y_ref[b, :, half:] = y_tm[1, pl.ds(b, S5_TILE, stride=S5_SEQS), :]

    @pl.when(i == pl.num_programs(1) - 1)
    def _():
        sre_ref[...] = s_re[:S5_SEQS, :]
        sim_ref[...] = s_im[:S5_SEQS, :]


def _s5_prompt(z, wb, wcre, wcim, ab_re, ab_im, d_row, seq_len):
    nt = seq_len // S5_TILE

    def u_spec(b):
        return pl.BlockSpec((S5_TILE, BLOCK_CH), lambda j, i, b=b: (b * nt + i, j))

    rows = S5_TILE * S5_SEQS
    return pl.pallas_call(
        _s5_prompt_kernel,
        grid=(SSM_BLOCKS, nt),
        in_specs=[u_spec(b) for b in range(S5_SEQS)] + [
            pl.BlockSpec((None, BLOCK_CH, 2 * BLOCK_ST), lambda j, i: (j, 0, 0)),
            pl.BlockSpec((None, BLOCK_CH, BLOCK_ST), lambda j, i: (j, 0, 0)),
            pl.BlockSpec((None, BLOCK_CH, BLOCK_ST), lambda j, i: (j, 0, 0)),
            pl.BlockSpec((1, BLOCK_ST), lambda j, i: (0, j)),
            pl.BlockSpec((1, BLOCK_ST), lambda j, i: (0, j)),
            pl.BlockSpec((1, BLOCK_CH), lambda j, i: (0, j)),
        ],
        out_specs=[
            pl.BlockSpec((S5_SEQS, S5_TILE, BLOCK_CH), lambda j, i: (0, i, j)),
            pl.BlockSpec((S5_SEQS, BLOCK_ST), lambda j, i: (0, j)),
            pl.BlockSpec((S5_SEQS, BLOCK_ST), lambda j, i: (0, j)),
        ],
        out_shape=[
            jax.ShapeDtypeStruct((S5_SEQS, seq_len, SSM_WIDTH), F32),
            jax.ShapeDtypeStruct((S5_SEQS, SSM_GROUPS * SSM_STATE), F32),
            jax.ShapeDtypeStruct((S5_SEQS, SSM_GROUPS * SSM_STATE), F32),
        ],
        scratch_shapes=[
            pltpu.VMEM((2, rows, 128), F32),
            pltpu.VMEM((rows, 2 * BLOCK_ST), F32),
            pltpu.VMEM((2, rows, 128), F32),
            pltpu.VMEM((2 * S5_SEQS, 2 * BLOCK_ST), F32),
        ],
        compiler_params=_cparams(("parallel", "arbitrary")),
        name="s5_prompt",
    )(z, z, z, z, wb, wcre, wcim, ab_re, ab_im, d_row)


def _s5_sample_kernel(u_ref, s0re_ref, s0im_ref, wb_ref, wcre_ref, wcim_ref, abre_ref, abim_ref,
                      d_ref, y_ref, sre_ref, sim_ref, u_sl, y_sl):
    half = BLOCK_CH // 2
    n_seq = s0re_ref.shape[0]
    u_sl[0] = u_ref[:, :half]
    u_sl[1] = u_ref[:, half:]
    a_re = abre_ref[...]
    a_im = abim_ref[...]
    s_re = s0re_ref[...]
    s_im = s0im_ref[...]
    for t in range(SAMPLE_SEQ):
        rows = pl.ds(t, n_seq, stride=SAMPLE_SEQ)
        u_t = jnp.concatenate([u_sl[0, rows, :], u_sl[1, rows, :]], axis=1)
        bu = _dot(u_t.astype(BF16), wb_ref[...])
        n_re = a_re * s_re - a_im * s_im + bu[:, :BLOCK_ST]
        n_im = a_re * s_im + a_im * s_re + bu[:, BLOCK_ST:]
        s_re, s_im = n_re, n_im
        y = (_dot_nt(s_re.astype(BF16), wcre_ref[...]) - _dot_nt(s_im.astype(BF16), wcim_ref[...])
             + d_ref[...] * u_t)
        y = _gelu(y)
        y_sl[0, rows, :] = y[:, :half]
        y_sl[1, rows, :] = y[:, half:]
    y_ref[:, :half] = y_sl[0]
    y_ref[:, half:] = y_sl[1]
    sre_ref[...] = s_re
    sim_ref[...] = s_im


def _s5_sample(z, s0_re, s0_im, wb, wcre, wcim, ab_re, ab_im, d_row, row0, n_rows):
    n_seq = s0_re.shape[0]
    st_spec = pl.BlockSpec((n_seq, BLOCK_ST), lambda j: (0, j))
    return pl.pallas_call(
        _s5_sample_kernel,
        grid=(SSM_BLOCKS,),
        in_specs=[
            pl.BlockSpec((n_rows, BLOCK_CH), lambda j: (row0 // n_rows, j)),
            st_spec, st_spec,
            pl.BlockSpec((None, BLOCK_CH, 2 * BLOCK_ST), lambda j: (j, 0, 0)),
            pl.BlockSpec((None, BLOCK_CH, BLOCK_ST), lambda j: (j, 0, 0)),
            pl.BlockSpec((None, BLOCK_CH, BLOCK_ST), lambda j: (j, 0, 0)),
            pl.BlockSpec((1, BLOCK_ST), lambda j: (0, j)),
            pl.BlockSpec((1, BLOCK_ST), lambda j: (0, j)),
            pl.BlockSpec((1, BLOCK_CH), lambda j: (0, j)),
        ],
        out_specs=[pl.BlockSpec((n_rows, BLOCK_CH), lambda j: (0, j)), st_spec, st_spec],
        out_shape=[
            jax.ShapeDtypeStruct((n_rows, SSM_WIDTH), F32),
            jax.ShapeDtypeStruct(s0_re.shape, F32),
            jax.ShapeDtypeStruct(s0_im.shape, F32),
        ],
        scratch_shapes=[pltpu.VMEM((2, n_rows, 128), F32), pltpu.VMEM((2, n_rows, 128), F32)],
        compiler_params=_cparams(("parallel",)),
        name="s5_sample",
    )(z, s0_re, s0_im, wb, wcre, wcim, ab_re, ab_im, d_row)


def _postmix_kernel(hmp_ref, hms_ref, ysp_ref, yss_ref, xp_ref, xs_ref,
                    wglu_ref, bglu_ref, wout_ref, gain_ref, wq_ref, keys_ref,
                    h1_ref, c_ref, sp_ref, ss_ref, *, prompt_tiles):
    i = pl.program_id(0)
    ys = _group_pick(ysp_ref, yss_ref, prompt_tiles)
    hm = _group_pick(hmp_ref, hms_ref, prompt_tiles)
    glu = ys * _sigmoid(_dot(ys.astype(BF16), wglu_ref[...]) + bglu_ref[...])
    mix = (_dot(hm.astype(BF16), wout_ref[:MLSTM_WIDTH, :])
           + _dot(glu.astype(BF16), wout_ref[MLSTM_WIDTH:, :]))
    h1 = _group_pick(xp_ref, xs_ref, prompt_tiles) + mix
    h1_ref[...] = h1
    c = _rmsnorm(h1, gain_ref[...]).astype(BF16)
    c_ref[...] = c
    qp = _dot(c, wq_ref[...])
    scores = []
    width = KEY_GROUP * PEER_HALF
    for g in range(2 * PEER_HEADS // KEY_GROUP):
        sg = _dot_nt(keys_ref[g], qp[:, g * width:(g + 1) * width].astype(BF16))
        scores += [sg[jj * PEER_KEYS:(jj + 1) * PEER_KEYS, :] for jj in range(KEY_GROUP)]

    @pl.when(i < prompt_tiles)
    def _():
        for j, s in enumerate(scores):
            sp_ref[j] = s

    @pl.when(i >= prompt_tiles)
    def _():
        for j, s in enumerate(scores):
            ss_ref[j] = s


KEY_GROUP = 4


def _blockdiag_keys(keys):
    ng = keys.shape[0] // KEY_GROUP
    k = keys.reshape(ng, KEY_GROUP, PEER_KEYS, PEER_HALF)
    eye = jnp.eye(KEY_GROUP, dtype=bool)
    w = jnp.where(eye[None, :, None, :, None], k[:, :, :, None, :], 0.0)
    return w.reshape(ng, KEY_GROUP * PEER_KEYS, KEY_GROUP * PEER_HALF).astype(BF16)


def _postmix(hm_p, hm_s, ys_p, ys_s, x_p, x_s, wglu, bglu, wout, gain, wq, keys, tm=256):
    n_p, n_s = x_p.shape[0], x_s.shape[0]
    t = n_p + n_s
    pt = n_p // tm
    nk = 2 * PEER_HEADS
    return pl.pallas_call(
        functools.partial(_postmix_kernel, prompt_tiles=pt),
        grid=(t // tm,),
        in_specs=_group_specs(tm, MLSTM_WIDTH, pt) + _group_specs(tm, SSM_WIDTH, pt)
        + _group_specs(tm, D_MODEL, pt) + [
            _resident((SSM_WIDTH, SSM_WIDTH), lambda i: (0, 0)),
            _resident((1, SSM_WIDTH), lambda i: (0, 0)),
            _resident((D_MODEL, D_MODEL), lambda i: (0, 0)),
            _resident((1, D_MODEL), lambda i: (0, 0)),
            _resident((D_MODEL, PEER_HEADS * 2 * PEER_HALF), lambda i: (0, 0)),
            _resident(keys.shape, lambda i: (0, 0, 0)),
        ],
        out_specs=[
            pl.BlockSpec((tm, D_MODEL), lambda i: (i, 0)),
            pl.BlockSpec((tm, D_MODEL), lambda i: (i, 0)),
            pl.BlockSpec((nk, PEER_KEYS, tm), lambda i: (0, 0, jnp.minimum(i, pt - 1))),
            pl.BlockSpec((nk, PEER_KEYS, tm), lambda i: (0, 0, jnp.maximum(i - pt, 0))),
        ],
        out_shape=[
            jax.ShapeDtypeStruct((t, D_MODEL), F32),
            jax.ShapeDtypeStruct((t, D_MODEL), BF16),
            jax.ShapeDtypeStruct((nk, PEER_KEYS, n_p), F32),
            jax.ShapeDtypeStruct((nk, PEER_KEYS, n_s), F32),
        ],
        compiler_params=_cparams(("arbitrary",)),
        name="postmix",
    )(hm_p, hm_s, ys_p, ys_s, x_p, x_s, wglu, bglu, wout, gain, wq, keys)


ID_NONE = 1 << 20
TOPK_SUB = 8


def _sort_network(n):
    pairs = []
    p = 1
    while p < n:
        k = p
        while k >= 1:
            for j in range(k % p, n - k, 2 * k):
                for i in range(min(k, n - j - k)):
                    if (i + j) // (2 * p) == (i + j + k) // (2 * p):
                        pairs.append((i + j, i + j + k))
            k //= 2
        p *= 2
    return pairs


_SORT16 = _sort_network(PEER_TOPK)


def _precedes(b, a):
    (vb, ib), (va, ia) = b, a
    return jnp.logical_or(vb > va, jnp.logical_and(vb == va, ib < ia))


def _first_of(a, b):
    sw = _precedes(b, a)
    return jnp.where(sw, b[0], a[0]), jnp.where(sw, b[1], a[1])


def _exchange(items, i, j):
    a, b = items[i], items[j]
    sw = _precedes(b, a)
    items[i] = (jnp.where(sw, b[0], a[0]), jnp.where(sw, b[1], a[1]))
    items[j] = (jnp.where(sw, a[0], b[0]), jnp.where(sw, a[1], b[1]))


def _sort16(items):
    items = list(items)
    for i, j in _SORT16:
        _exchange(items, i, j)
    return items


def _bitonic_merge16(items):
    items = list(items)
    d = PEER_TOPK // 2
    while d >= 1:
        for i in range(PEER_TOPK):
            if i & d == 0:
                _exchange(items, i, i + d)
        d //= 2
    return items


def _merge_top16(a, b):
    return _bitonic_merge16([_first_of(a[i], b[PEER_TOPK - 1 - i]) for i in range(PEER_TOPK)])


def _top16_of_keys(s_ref, half, shape):
    best = None
    for g in range(PEER_KEYS // PEER_TOPK):
        grp = _sort16([(s_ref[half, g * PEER_TOPK + k], jnp.full(shape, g * PEER_TOPK + k, I32))
                       for k in range(PEER_TOPK)])
        best = grp if best is None else _merge_top16(best, grp)
    return best


def _route_head(s_ref):
    shape = s_ref.shape[2:]
    top1 = _top16_of_keys(s_ref, 0, shape)
    top2 = _top16_of_keys(s_ref, 1, shape)

    def pair(i, j):
        return top1[i][0] + top2[j][0], jnp.full(shape, i * PEER_TOPK + j, I32)

    pad = (jnp.full(shape, NEG_INF, F32), jnp.full(shape, ID_NONE, I32))
    g0 = [pair(0, j) for j in range(16)]
    g1 = _bitonic_merge16([pair(1, j) for j in range(8)] + [pair(i, 0) for i in range(15, 7, -1)])
    g2 = _sort16([pair(i, j) for i in range(2, 7) for j in range(PEER_TOPK // (i + 1))])
    g3 = [pair(7, 0), pair(7, 1)] + [pad] * 14
    best = _merge_top16(_merge_top16(g0, g1), _merge_top16(g2, g3))

    mx = best[0][0]
    e1s, e2s, exps = [], [], []
    for k in range(PEER_TOPK):
        v, pid = best[k]
        a = lax.shift_right_logical(pid, 4)
        b = jnp.bitwise_and(pid, PEER_TOPK - 1)
        e1 = jnp.zeros(shape, I32)
        e2 = jnp.zeros(shape, I32)
        for r in range(PEER_TOPK):
            e1 = jnp.where(a == r, top1[r][1], e1)
            e2 = jnp.where(b == r, top2[r][1], e2)
        e1s.append(e1)
        e2s.append(e2)
        exps.append(jnp.exp(v - mx))
    total = exps[0]
    for k in range(1, PEER_TOPK):
        total = total + exps[k]
    return e1s, e2s, [e / total for e in exps]


def _topk_kernel(s_ref, e1_ref, e2_ref, g_ref):
    sub = s_ref.shape[2]
    for k, vals in enumerate(zip(*_route_head(s_ref))):
        for ref, val in zip((e1_ref, e2_ref, g_ref), vals):
            for s in range(sub):
                ref[k:k + 1, s * 128:(s + 1) * 128] = val[s:s + 1, :]


def _topk(scores):
    t = scores.shape[-1]
    ng = t // 128
    sub = min(ng, TOPK_SUB)
    r = PEER_HEADS * PEER_TOPK
    o_spec = pl.BlockSpec((PEER_TOPK, sub * 128), lambda i, h: (h, i))
    return pl.pallas_call(
        _topk_kernel,
        grid=(ng // sub, PEER_HEADS),
        in_specs=[pl.BlockSpec((2, PEER_KEYS, sub, 128), lambda i, h: (h, 0, i, 0))],
        out_specs=[o_spec, o_spec, o_spec],
        out_shape=[
            jax.ShapeDtypeStruct((r, t), I32),
            jax.ShapeDtypeStruct((r, t), I32),
            jax.ShapeDtypeStruct((r, t), F32),
        ],
        compiler_params=_cparams(("parallel", "parallel")),
        name="topk",
    )(scores.reshape(2 * PEER_HEADS, PEER_KEYS, ng, 128))


WB_GROUP = 16
WB_PITCH = 132


def _token_weights(e1_row, e2_row, g_row):
    sub = lax.broadcasted_iota(I32, (PEER_KEYS, PEER_KEYS), 0)
    onehot1 = jnp.where(sub == e1_row, 1.0, 0.0).astype(BF16)
    gated2 = jnp.where(sub == e2_row, 0.5 * g_row, 0.0).astype(BF16)
    return _dot_nt(onehot1, gated2)


def _wbuild_kernel(e1p_ref, e1s_ref, e2p_ref, e2s_ref, gp_ref, gs_ref, w_ref, stage, *, prompt_tiles):
    e1 = jnp.transpose(_group_pick(e1p_ref, e1s_ref, prompt_tiles))
    e2 = jnp.transpose(_group_pick(e2p_ref, e2s_ref, prompt_tiles))
    g = jnp.transpose(_group_pick(gp_ref, gs_ref, prompt_tiles))
    for grp in range(e1.shape[0] // WB_GROUP):
        base = (grp % 2) * WB_GROUP * WB_PITCH
        for tt in range(WB_GROUP):
            t = grp * WB_GROUP + tt
            stage[base + tt * WB_PITCH:base + tt * WB_PITCH + PEER_KEYS, :] = _token_weights(
                e1[t:t + 1, :], e2[t:t + 1, :], g[t:t + 1, :])
        for e in range(PEER_KEYS):
            blk = stage[pl.ds(base + e, WB_GROUP, stride=WB_PITCH), :]
            w_ref[e, grp * WB_GROUP:(grp + 1) * WB_GROUP, :] = blk.astype(BF16)


def _wbuild(routing_p, routing_s, tw=256):
    n_p, n_s = routing_p[0].shape[1], routing_s[0].shape[1]
    t = n_p + n_s
    pt = n_p // tw
    specs = [
        pl.BlockSpec((PEER_KEYS, tw), lambda i: (0, jnp.minimum(i, pt - 1))),
        pl.BlockSpec((PEER_KEYS, tw), lambda i: (0, jnp.maximum(i - pt, 0))),
    ]
    operands = [x for pair in zip(routing_p, routing_s) for x in pair]
    return pl.pallas_call(
        functools.partial(_wbuild_kernel, prompt_tiles=pt),
        grid=(t // tw,),
        in_specs=specs * 3,
        out_specs=pl.BlockSpec((PEER_KEYS, tw, PEER_KEYS), lambda i: (0, i, 0)),
        out_shape=jax.ShapeDtypeStruct((PEER_KEYS, t, PEER_KEYS), BF16),
        scratch_shapes=[pltpu.VMEM((2 * WB_GROUP * WB_PITCH, PEER_KEYS), F32)],
        compiler_params=_cparams(("arbitrary",)),
        name="wbuild",
    )(*operands)


PEER_EB = 256


def _peer_kernel(c_ref, u_ref, v_ref, w_ref, o_ref, s_s):
    j = pl.program_id(1)
    last = pl.num_programs(1) - 1

    def gated():
        parts = []
        for k in range(PEER_EB // PEER_KEYS):
            x = s_s[:, k * PEER_KEYS:(k + 1) * PEER_KEYS]
            t = jnp.tanh(x * (GELU_C1 + GELU_C2 * (x * x)))
            parts.append((x * w_ref[k].astype(F32)) * (1.0 + t))
        return jnp.concatenate(parts, axis=1).astype(BF16)

    def scores():
        return _dot_nt(c_ref[...], u_ref[...].astype(BF16))

    @pl.when(j == 0)
    def _():
        o_ref[...] = jnp.zeros_like(o_ref)
        s_s[...] = scores()

    @pl.when(jnp.logical_and(j > 0, j < last))
    def _():
        wact = gated()
        s_s[...] = scores()
        o_ref[...] += _dot(wact, v_ref[...].astype(BF16))

    @pl.when(j == last)
    def _():
        o_ref[...] += _dot(gated(), v_ref[...].astype(BF16))


def _peer(c, u, v, w3, n_tiles=4):
    t = c.shape[0]
    tm = t // n_tiles
    nb = PEER_EXPERTS // PEER_EB
    return pl.pallas_call(
        _peer_kernel,
        grid=(n_tiles, nb + 1),
        in_specs=[
            pl.BlockSpec((tm, D_MODEL), lambda i, j: (i, 0), pipeline_mode=pl.Buffered(1)),
            pl.BlockSpec((PEER_EB, D_MODEL), lambda i, j: (jnp.minimum(j, nb - 1), 0)),
            pl.BlockSpec((PEER_EB, D_MODEL), lambda i, j: (jnp.maximum(j - 1, 0), 0)),
            pl.BlockSpec((PEER_EB // PEER_KEYS, tm, PEER_KEYS),
                         lambda i, j: (jnp.maximum(j - 1, 0), i, 0)),
        ],
        out_specs=pl.BlockSpec((tm, D_MODEL), lambda i, j: (i, 0), pipeline_mode=pl.Buffered(1)),
        out_shape=jax.ShapeDtypeStruct((t, D_MODEL), F32),
        scratch_shapes=[pltpu.VMEM((tm, PEER_EB), F32)],
        compiler_params=_cparams(("parallel", "arbitrary")),
        name="peer",
    )(c, u, v, w3)


def _tail_kernel(h1_ref, peer_ref, pp_ref, ps_ref, gple_ref, wgate_ref, wproj_ref, gfin_ref,
                 yp_ref, ys_ref, *, prompt_tiles):
    i = pl.program_id(0)
    h2 = h1_ref[...] + peer_ref[...]
    gate = _sigmoid(_dot(_rmsnorm(h2, gple_ref[...]).astype(BF16), wgate_ref[...]))
    e = _dot(_group_pick(pp_ref, ps_ref, prompt_tiles).astype(BF16), wproj_ref[...])
    y = _rmsnorm(h2 + e * gate, gfin_ref[...])

    @pl.when(i < prompt_tiles)
    def _():
        yp_ref[...] = y

    @pl.when(i >= prompt_tiles)
    def _():
        ys_ref[...] = y


def _tail(h1, peer, p_p, p_s, gple, wgate, wproj, gfin, tm=512):
    t = h1.shape[0]
    n_prompt = p_p.shape[0]
    pt = n_prompt // tm
    ple = p_p.shape[1]
    return pl.pallas_call(
        functools.partial(_tail_kernel, prompt_tiles=pt),
        grid=(t // tm,),
        in_specs=[
            pl.BlockSpec((tm, D_MODEL), lambda i: (i, 0)),
            pl.BlockSpec((tm, D_MODEL), lambda i: (i, 0)),
        ] + _group_specs(tm, ple, pt) + [
            _resident((1, D_MODEL), lambda i: (0, 0)),
            _resident((D_MODEL, D_MODEL), lambda i: (0, 0)),
            _resident((ple, D_MODEL), lambda i: (0, 0)),
            _resident((1, D_MODEL), lambda i: (0, 0)),
        ],
        out_specs=[
            pl.BlockSpec((tm, D_MODEL), lambda i: (jnp.minimum(i, pt - 1), 0)),
            pl.BlockSpec((tm, D_MODEL), lambda i: (jnp.maximum(i - pt, 0), 0)),
        ],
        out_shape=[
            jax.ShapeDtypeStruct((n_prompt, D_MODEL), F32),
            jax.ShapeDtypeStruct((t - n_prompt, D_MODEL), F32),
        ],
        compiler_params=_cparams(("arbitrary",)),
        name="tail",
    )(h1, peer, p_p, p_s, gple, wgate, wproj, gfin)


def kernel(x_prompt, x_sample, state_mlstm_C, state_mlstm_n, state_mlstm_m, state_ssm_re, state_ssm_im, p_prompt, p_sample, norm_mix, w_in, b_igate, b_fgate, mlstm_norm, ssm_A_re, ssm_A_im, ssm_B_re, ssm_B_im, ssm_C_re, ssm_C_im, ssm_D, ssm_log_dt, w_glu, b_glu, w_out, norm_ffn, peer_w_q, peer_keys, peer_u, peer_v, norm_ple, w_ple_gate, w_ple_proj, norm_final):
    n_pseq, p_len, _ = x_prompt.shape
    n_sseq, s_len, _ = x_sample.shape
    assert s_len == SAMPLE_SEQ and n_pseq == S5_SEQS and w_in.shape[0] == 1
    n_prompt = n_pseq * p_len
    n_sample = n_sseq * s_len
    row = lambda t: t.reshape(1, -1)

    x_p = x_prompt.reshape(n_prompt, D_MODEL)
    x_s = x_sample.reshape(n_sample, D_MODEL)

    w_bf = w_in[0].astype(BF16)
    n_gate = 2 * HEADS
    w_tail = jnp.concatenate(
        [w_bf[:, Z_U + n_gate:], w_bf[:, Z_U:Z_U + n_gate],
         jnp.zeros((D_MODEL, 128 - n_gate), BF16)], axis=1)
    gbias = jnp.concatenate([b_igate[0], b_fgate[0], jnp.zeros((128 - n_gate,), F32)]).reshape(1, 128)
    z, zt = _in_proj(x_p, x_s, row(norm_mix[0]), w_bf, w_tail, gbias)

    gain_m = row(mlstm_norm[0])
    hm_p, c_p, n_p, m_p = _mlstm_prompt(z, zt, gain_m, n_pseq, p_len)
    nbh = n_sseq * HEADS
    hm_s, c_s, n_s, m_s = _mlstm_sample(
        z, zt, gain_m,
        state_mlstm_C[0].reshape(nbh, HEAD_DIM, HEAD_DIM),
        state_mlstm_n[0].reshape(nbh, 1, HEAD_DIM),
        jnp.broadcast_to(state_mlstm_m[0].reshape(nbh, 1, 1), (nbh, 1, 128)),
        n_prompt, n_sample)

    wb, wcre, wcim, ab_re, ab_im = _s5_params(ssm_A_re[0], ssm_A_im[0], ssm_log_dt[0], ssm_B_re[0],
                                              ssm_B_im[0], ssm_C_re[0], ssm_C_im[0])
    d_row = row(ssm_D[0])
    ys_p, sre_p, sim_p = _s5_prompt(zt, wb, wcre, wcim, ab_re, ab_im, d_row, p_len)
    n_st = SSM_GROUPS * SSM_STATE
    ys_s, sre_s, sim_s = _s5_sample(
        zt, state_ssm_re[0].reshape(n_sseq, n_st), state_ssm_im[0].reshape(n_sseq, n_st),
        wb, wcre, wcim, ab_re, ab_im, d_row, n_prompt, n_sample)

    keys = _blockdiag_keys(peer_keys[0].reshape(2 * PEER_HEADS, PEER_KEYS, PEER_HALF))
    h1, c, scores_p, scores_s = _postmix(
        hm_p, hm_s, ys_p.reshape(n_prompt, SSM_WIDTH), ys_s, x_p, x_s,
        w_glu[0].astype(BF16), row(b_glu[0]), w_out[0].astype(BF16),
        row(norm_ffn[0]), peer_w_q[0].astype(BF16), keys)
    w3 = _wbuild(_topk(scores_p), _topk(scores_s))
    peer = _peer(c, peer_u[0], peer_v[0], w3)

    y_p, y_s = _tail(h1, peer, p_prompt[0].reshape(n_prompt, -1), p_sample[0].reshape(n_sample, -1),
                     row(norm_ple[0]), w_ple_gate[0].astype(BF16), w_ple_proj[0].astype(BF16),
                     row(norm_final))

    st_shape = (1, -1, SSM_GROUPS, SSM_STATE)
    return (
        y_p.reshape(x_prompt.shape), y_s.reshape(x_sample.shape),
        c_p.reshape(1, n_pseq, HEADS, HEAD_DIM, HEAD_DIM), n_p.reshape(1, n_pseq, HEADS, HEAD_DIM),
        m_p[:, 0, 0].reshape(1, n_pseq, HEADS),
        sre_p.reshape(st_shape), sim_p.reshape(st_shape),
        c_s.reshape(1, n_sseq, HEADS, HEAD_DIM, HEAD_DIM), n_s.reshape(1, n_sseq, HEADS, HEAD_DIM),
        m_s[:, 0, 0].reshape(1, n_sseq, HEADS),
        sre_s.reshape(st_shape), sim_s.reshape(st_shape),
    )
```

```python
import functools

import jax
import jax.numpy as jnp
from jax import lax
from jax.experimental import pallas as pl
from jax.experimental.pallas import tpu as pltpu

F32 = jnp.float32
BF16 = jnp.bfloat16
I32 = jnp.int32

EPS = 1e-6
D_MODEL = 2048
HEADS = 4
HEAD_DIM = 256
MLSTM_WIDTH = HEADS * HEAD_DIM
SSM_WIDTH = 1024
SSM_GROUPS = 64
SSM_STATE = 64
SSM_GROUP = 16
GROUPS_PER_BLOCK = 16
SSM_BLOCKS = SSM_GROUPS // GROUPS_PER_BLOCK
BLOCK_CH = GROUPS_PER_BLOCK * SSM_GROUP
BLOCK_ST = GROUPS_PER_BLOCK * SSM_STATE
PEER_HEADS = 8
PEER_KEYS = 128
PEER_TOPK = 16
PEER_HALF = 64
PEER_EXPERTS = PEER_KEYS * PEER_KEYS

Z_Q, Z_K, Z_V, Z_O, Z_U = 0, 1024, 2048, 3072, 4096
ZT_G = 1024
ZT_COLS = ZT_G + 128
IN_PROJ_COLS = 2048

MLSTM_CHUNK = 256
K_SCALE = HEAD_DIM ** -0.5
NEG_INF = float("-inf")

VMEM_LIMIT = 56 * 1024 * 1024


def _cparams(semantics):
    return pltpu.CompilerParams(dimension_semantics=semantics, vmem_limit_bytes=VMEM_LIMIT)


def _resident(shape, index_map):
    return pl.BlockSpec(shape, index_map, pipeline_mode=pl.Buffered(1))


def _sigmoid(x):
    return 1.0 / (1.0 + jnp.exp(-x))


GELU_C1 = 0.7978845608028654
GELU_C2 = GELU_C1 * 0.044715


def _gelu(x):
    return x * (0.5 * (1.0 + jnp.tanh(GELU_C1 * (x + 0.044715 * (x * x * x)))))


def _rmsnorm(x, gain):
    return x * lax.rsqrt(jnp.mean(x * x, axis=-1, keepdims=True) + EPS) * gain


def _dot(a, b):
    return jnp.dot(a, b, preferred_element_type=F32)


def _dot_nt(a, b):
    return lax.dot_general(a, b, (((1,), (1,)), ((), ())), preferred_element_type=F32)


def _dot_tn(a, b):
    return lax.dot_general(a, b, (((0,), (0,)), ((), ())), preferred_element_type=F32)


def _group_specs(tm, width, prompt_tiles):
    return [
        pl.BlockSpec((tm, width), lambda i: (jnp.minimum(i, prompt_tiles - 1), 0)),
        pl.BlockSpec((tm, width), lambda i: (jnp.maximum(i - prompt_tiles, 0), 0)),
    ]


def _group_pick(prompt_ref, sample_ref, prompt_tiles):
    return jnp.where(pl.program_id(0) < prompt_tiles, prompt_ref[...], sample_ref[...])


def _in_proj_kernel(xp_ref, xs_ref, gain_ref, wm_ref, wt_ref, gbias_ref, zm_ref, zt_ref, a_s,
                    *, prompt_tiles, main_steps):
    j = pl.program_id(1)

    @pl.when(j == 0)
    def _():
        a_s[...] = _rmsnorm(_group_pick(xp_ref, xs_ref, prompt_tiles), gain_ref[...]).astype(BF16)

    for step in range(main_steps):
        @pl.when(j == step)
        def _(base=step * IN_PROJ_COLS):
            for c0 in range(0, IN_PROJ_COLS, 512):
                zm_ref[:, c0:c0 + 512] = _dot(a_s[...], wm_ref[:, base + c0:base + c0 + 512])

    @pl.when(j == main_steps)
    def _():
        a = a_s[...]
        for c0 in range(0, ZT_G, 512):
            zt_ref[:, c0:c0 + 512] = _dot(a, wt_ref[:, c0:c0 + 512])
        gz = _dot(a, wt_ref[:, ZT_G:]) + gbias_ref[...]
        lane = lax.broadcasted_iota(I32, gz.shape, 1)
        log_f = jnp.minimum(gz, 0.0) - jnp.log(1.0 + jnp.exp(-jnp.abs(gz)))
        zt_ref[:, ZT_G:] = jnp.where(lane < HEADS, gz, log_f)


def _in_proj(x_p, x_s, gain, w_bf, w_tail, gbias, tm=512):
    t = x_p.shape[0] + x_s.shape[0]
    pt = x_p.shape[0] // tm
    ms = Z_U // IN_PROJ_COLS
    return pl.pallas_call(
        functools.partial(_in_proj_kernel, prompt_tiles=pt, main_steps=ms),
        grid=(t // tm, ms + 1),
        in_specs=[
            pl.BlockSpec((tm, D_MODEL), lambda i, j: (jnp.minimum(i, pt - 1), 0)),
            pl.BlockSpec((tm, D_MODEL), lambda i, j: (jnp.maximum(i - pt, 0), 0),
                         pipeline_mode=pl.Buffered(1)),
            _resident((1, D_MODEL), lambda i, j: (0, 0)),
            _resident((D_MODEL, Z_U), lambda i, j: (0, 0)),
            _resident((D_MODEL, ZT_COLS), lambda i, j: (0, 0)),
            _resident((1, 128), lambda i, j: (0, 0)),
        ],
        out_specs=[
            pl.BlockSpec((tm, IN_PROJ_COLS), lambda i, j: (i, jnp.minimum(j, ms - 1))),
            pl.BlockSpec((tm, ZT_COLS), lambda i, j: (i, 0)),
        ],
        out_shape=[
            jax.ShapeDtypeStruct((t, Z_U), F32),
            jax.ShapeDtypeStruct((t, ZT_COLS), F32),
        ],
        scratch_shapes=[pltpu.VMEM((tm, D_MODEL), BF16)],
        compiler_params=_cparams(("arbitrary", "arbitrary")),
        name="in_proj",
    )(x_p, x_s, gain, w_bf, w_tail, gbias)


def _mlstm_chunks(items):
    L = items[0][0].shape[0]
    row = lax.broadcasted_iota(I32, (L, L), 0)
    col = lax.broadcasted_iota(I32, (L, L), 1)
    diag = row == col
    causal = col <= row

    gates = []
    for (_, _, _, ig_col, lf_col, _, _, m_state) in items:
        lf_row = jnp.sum(jnp.where(diag, lf_col, 0.0), axis=0, keepdims=True)
        ig_row = jnp.sum(jnp.where(diag, ig_col, 0.0), axis=0, keepdims=True)
        b_col = jnp.sum(jnp.where(causal, lf_row, 0.0), axis=1, keepdims=True)
        b_row = jnp.sum(jnp.where(row <= col, lf_col, 0.0), axis=0, keepdims=True)
        r_row = ig_row - b_row
        r_col = ig_col - b_col
        run_max = jnp.max(jnp.where(causal, r_row, NEG_INF), axis=1, keepdims=True)
        m_run = jnp.maximum(m_state, run_max)
        w_intra = jnp.exp(jnp.where(causal, r_row - m_run, NEG_INF))
        w_inter = jnp.exp(m_state - m_run)
        gates.append((b_col, r_col, m_run, w_intra, w_inter))

    products = []
    for (q, k, v, _, _, c_get, _, _) in items:
        qb = q.astype(BF16)
        vb = v.astype(BF16)
        products.append((_dot_nt(qb, k.astype(BF16)), _dot(qb, c_get().astype(BF16)), vb))

    heads = []
    for (q, _, _, _, _, _, n_state, _), (b_col, _, m_run, w_intra, w_inter), (qk, qc, vb) in zip(
            items, gates, products):
        s = qk * w_intra
        num = w_inter * qc + _dot(s.astype(BF16), vb)
        qn = jnp.sum(q * n_state, axis=1, keepdims=True)
        den = w_inter * qn + jnp.sum(s, axis=1, keepdims=True)
        heads.append(num / jnp.maximum(jnp.abs(den), jnp.exp(-(b_col + m_run))))

    out = []
    for (_, k, _, _, _, c_get, n_state, m_state), (b_col, r_col, m_run, _, _), (_, _, vb), hb in zip(
            items, gates, products, heads):
        m_last = m_run[L - 1:L, :]
        m_new = b_col[L - 1:L, :] + m_last
        kw = k * jnp.exp(r_col - m_last)
        sc = jnp.exp(m_state - m_last)
        c_new = sc * c_get() + _dot_tn(kw.astype(BF16), vb)
        n_new = sc * n_state + jnp.sum(kw, axis=0, keepdims=True)
        out.append((hb, c_new, n_new, m_new))
    return out


def _head_out(hb, gain, o_pre):
    return hb * lax.rsqrt(jnp.mean(hb * hb, axis=-1, keepdims=True) + EPS) * gain * _sigmoid(o_pre)


def _mlstm_prompt_kernel(q_ref, k_ref, v_ref, o_ref, g_ref, gain_ref,
                         hm_ref, c_out_ref, n_out_ref, m_out_ref, c_s, n_s, m_s):
    h0 = pl.program_id(1) * PROMPT_HEADS
    c_s[...] = jnp.zeros_like(c_s)
    n_s[...] = jnp.zeros_like(n_s)
    m_s[...] = jnp.zeros_like(m_s)

    def body(ci, carry):
        rows = pl.ds(pl.multiple_of(ci * MLSTM_CHUNK, MLSTM_CHUNK), MLSTM_CHUNK)
        gates = g_ref[rows, :]
        lane = lax.broadcasted_iota(I32, gates.shape, 1)
        items = []
        for hh in range(PROMPT_HEADS):
            cs = slice(hh * HEAD_DIM, (hh + 1) * HEAD_DIM)
            ig_col = jnp.sum(jnp.where(lane == h0 + hh, gates, 0.0), axis=1, keepdims=True)
            lf_col = jnp.sum(jnp.where(lane == h0 + hh + HEADS, gates, 0.0), axis=1, keepdims=True)
            items.append((q_ref[rows, cs], k_ref[rows, cs] * K_SCALE, v_ref[rows, cs], ig_col, lf_col,
                          functools.partial(lambda r, i: r[i], c_s, hh), n_s[hh], m_s[hh][:, 0:1]))
        for hh, (hb, c_new, n_new, m_new) in enumerate(_mlstm_chunks(items)):
            cs = slice(hh * HEAD_DIM, (hh + 1) * HEAD_DIM)
            c_s[hh] = c_new
            n_s[hh] = n_new
            m_s[hh] = jnp.broadcast_to(m_new, (1, 128))
            hm_ref[rows, cs] = _head_out(hb, gain_ref[:, cs], o_ref[rows, cs])
        return carry

    lax.fori_loop(0, q_ref.shape[0] // MLSTM_CHUNK, body, 0)
    c_out_ref[...] = c_s[...]
    n_out_ref[...] = n_s[...]
    m_out_ref[...] = m_s[...]


PROMPT_HEADS = 2


def _mlstm_prompt(z, zt, gain, n_seq, seq_len):
    width = PROMPT_HEADS * HEAD_DIM
    hsteps = HEADS // PROMPT_HEADS

    def sec(off):
        return pl.BlockSpec((seq_len, width), lambda b, h, o=off // width: (b, o + h))

    def state(*tail):
        return pl.BlockSpec((PROMPT_HEADS,) + tail, lambda b, h: (b * hsteps + h,) + (0,) * len(tail))

    nbh = n_seq * HEADS
    return pl.pallas_call(
        _mlstm_prompt_kernel,
        grid=(n_seq, hsteps),
        in_specs=[
            sec(Z_Q), sec(Z_K), sec(Z_V), sec(Z_O),
            pl.BlockSpec((seq_len, 128), lambda b, h: (b, ZT_G // 128)),
            pl.BlockSpec((1, width), lambda b, h: (0, h)),
        ],
        out_specs=[
            pl.BlockSpec((seq_len, width), lambda b, h: (b, h)),
            state(HEAD_DIM, HEAD_DIM), state(1, HEAD_DIM), state(1, 128),
        ],
        out_shape=[
            jax.ShapeDtypeStruct((n_seq * seq_len, MLSTM_WIDTH), F32),
            jax.ShapeDtypeStruct((nbh, HEAD_DIM, HEAD_DIM), F32),
            jax.ShapeDtypeStruct((nbh, 1, HEAD_DIM), F32),
            jax.ShapeDtypeStruct((nbh, 1, 128), F32),
        ],
        scratch_shapes=[
            pltpu.VMEM((PROMPT_HEADS, HEAD_DIM, HEAD_DIM), F32),
            pltpu.VMEM((PROMPT_HEADS, 1, HEAD_DIM), F32),
            pltpu.VMEM((PROMPT_HEADS, 1, 128), F32),
        ],
        compiler_params=_cparams(("parallel", "parallel")),
        name="mlstm_prompt",
    )(z, z, z, z, zt, gain)


SAMPLE_SEQ = 4
SAMPLE_ROWS = 16


def _mlstm_sample_kernel(z_ref, g_ref, gain_ref, c_in_ref, n_in_ref, m_in_ref,
                         hm_ref, c_out_ref, n_out_ref, m_out_ref):
    row = lax.broadcasted_iota(I32, (8, 1), 0)
    for grp in range(SAMPLE_ROWS // 8):
        r0 = grp * 8
        gates = g_ref[r0:r0 + 8, :]
        items, where = [], []
        for half in range(2):
            lo = half * SAMPLE_SEQ
            live = jnp.logical_and(row >= lo, row < lo + SAMPLE_SEQ)
            for h in range(HEADS):
                j = (grp * 2 + half) * HEADS + h
                q = z_ref[r0:r0 + 8, Z_Q + h * HEAD_DIM:Z_Q + (h + 1) * HEAD_DIM]
                k = z_ref[r0:r0 + 8, Z_K + h * HEAD_DIM:Z_K + (h + 1) * HEAD_DIM]
                v = z_ref[r0:r0 + 8, Z_V + h * HEAD_DIM:Z_V + (h + 1) * HEAD_DIM]
                k = jnp.where(live, k * K_SCALE, 0.0)
                v = jnp.where(live, v, 0.0)
                ig_col = jnp.where(live, gates[:, h:h + 1], NEG_INF)
                lf_col = jnp.where(live, gates[:, HEADS + h:HEADS + h + 1], 0.0)
                items.append((q, k, v, ig_col, lf_col, functools.partial(lambda r, i: r[i], c_in_ref, j),
                              n_in_ref[j], m_in_ref[j][:, 0:1]))
                where.append((j, lo, h))
        for (j, lo, h), (hb, c_new, n_new, m_new) in zip(where, _mlstm_chunks(items)):
            cs = slice(h * HEAD_DIM, (h + 1) * HEAD_DIM)
            o = z_ref[r0:r0 + 8, Z_O + h * HEAD_DIM:Z_O + (h + 1) * HEAD_DIM]
            out = _head_out(hb, gain_ref[:, cs], o)
            hm_ref[r0 + lo:r0 + lo + SAMPLE_SEQ, cs] = out[lo:lo + SAMPLE_SEQ, :]
            c_out_ref[j] = c_new
            n_out_ref[j] = n_new
            m_out_ref[j] = jnp.broadcast_to(m_new, (1, 128))


def _mlstm_sample(z, zt, gain, c0, n0, m0, row0, n_rows):
    nbh = c0.shape[0]
    per = SAMPLE_ROWS // SAMPLE_SEQ * HEADS
    rb0 = row0 // SAMPLE_ROWS
    return pl.pallas_call(
        _mlstm_sample_kernel,
        grid=(n_rows // SAMPLE_ROWS,),
        in_specs=[
            pl.BlockSpec((SAMPLE_ROWS, Z_U), lambda i: (rb0 + i, 0)),
            pl.BlockSpec((SAMPLE_ROWS, 128), lambda i: (rb0 + i, ZT_G // 128)),
            pl.BlockSpec((1, MLSTM_WIDTH), lambda i: (0, 0)),
            pl.BlockSpec((per, HEAD_DIM, HEAD_DIM), lambda i: (i, 0, 0)),
            pl.BlockSpec((per, 1, HEAD_DIM), lambda i: (i, 0, 0)),
            pl.BlockSpec((per, 1, 128), lambda i: (i, 0, 0)),
        ],
        out_specs=[
            pl.BlockSpec((SAMPLE_ROWS, MLSTM_WIDTH), lambda i: (i, 0)),
            pl.BlockSpec((per, HEAD_DIM, HEAD_DIM), lambda i: (i, 0, 0)),
            pl.BlockSpec((per, 1, HEAD_DIM), lambda i: (i, 0, 0)),
            pl.BlockSpec((per, 1, 128), lambda i: (i, 0, 0)),
        ],
        out_shape=[
            jax.ShapeDtypeStruct((n_rows, MLSTM_WIDTH), F32),
            jax.ShapeDtypeStruct((nbh, HEAD_DIM, HEAD_DIM), F32),
            jax.ShapeDtypeStruct((nbh, 1, HEAD_DIM), F32),
            jax.ShapeDtypeStruct((nbh, 1, 128), F32),
        ],
        compiler_params=_cparams(("parallel",)),
        name="mlstm_sample",
    )(z, zt, gain, c0, n0, m0)


def _s5_discretise(a_re, a_im, log_dt):
    dt = jnp.exp(log_dt)
    mag = jnp.exp(dt * a_re)
    ab_re = mag * jnp.cos(dt * a_im)
    ab_im = mag * jnp.sin(dt * a_im)
    den = a_re * a_re + a_im * a_im
    xr = ab_re - 1.0
    f_re = (xr * a_re + ab_im * a_im) / den
    f_im = (ab_im * a_re - xr * a_im) / den
    return ab_re, ab_im, f_re, f_im


def _s5_param_kernel(are, aim, ldt, bre, bim, cre, cim, are_r, aim_r, ldt_r,
                     wb_o, wcre_o, wcim_o, ab_re_o, ab_im_o):
    _, _, f_re, f_im = _s5_discretise(are[...], aim[...], ldt[...])
    bb = (f_re * bre[...] - f_im * bim[...], f_re * bim[...] + f_im * bre[...])
    row_g = lax.shift_right_logical(lax.broadcasted_iota(I32, (BLOCK_CH, BLOCK_ST), 0), 4)
    lane_g = lax.shift_right_logical(lax.broadcasted_iota(I32, (BLOCK_CH, BLOCK_ST), 1), 6)
    keep = row_g == lane_g

    def spread(src):
        return jnp.where(keep, jnp.concatenate([src] * (BLOCK_ST // 128), axis=1), 0.0).astype(BF16)

    for j in range(SSM_BLOCKS):
        rows = slice(j * BLOCK_CH, (j + 1) * BLOCK_CH)
        for ri in range(2):
            wb_o[j, :, ri * BLOCK_ST:(ri + 1) * BLOCK_ST] = spread(bb[ri][rows, :])
        wcre_o[j] = spread(cre[rows, :])
        wcim_o[j] = spread(cim[rows, :])
    ab_re, ab_im, _, _ = _s5_discretise(are_r[...], aim_r[...], ldt_r[...])
    ab_re_o[...] = ab_re
    ab_im_o[...] = ab_im


def _s5_params(a_re, a_im, log_dt, b_re, b_im, c_re, c_im):
    g, p = a_re.shape
    c = b_re.shape[-1]
    dup = lambda t: jnp.concatenate([t] * (128 // t.shape[-1]), axis=-1)
    by_channel = lambda t: dup(jnp.repeat(t, c, axis=0))
    b_rows = lambda t: dup(jnp.transpose(t, (0, 2, 1)).reshape(g * c, p))
    c_rows = lambda t: dup(t.reshape(g * c, p))
    ldt =jnp.broadcast_to(jnp.repeat(log_dt, c)[:, None], (g * c, 128))
    ldt_r = jnp.broadcast_to(log_dt[:, None], (g, p)).reshape(1, g * p)
    rowv = jax.ShapeDtypeStruct((1, g * p), F32)
    return pl.pallas_call(
        _s5_param_kernel,
        out_shape=[
            jax.ShapeDtypeStruct((SSM_BLOCKS, BLOCK_CH, 2 * BLOCK_ST), BF16),
            jax.ShapeDtypeStruct((SSM_BLOCKS, BLOCK_CH, BLOCK_ST), BF16),
            jax.ShapeDtypeStruct((SSM_BLOCKS, BLOCK_CH, BLOCK_ST), BF16),
            rowv, rowv,
        ],
        compiler_params=pltpu.CompilerParams(vmem_limit_bytes=VMEM_LIMIT),
        name="s5_params",
    )(by_channel(a_re), by_channel(a_im), ldt, b_rows(b_re), b_rows(b_im), c_rows(c_re), c_rows(c_im),
      a_re.reshape(1, g * p), a_im.reshape(1, g * p), ldt_r)


S5_TILE = 512
S5_SEQS = 4


def _s5_prompt_kernel(u0, u1, u2, u3, wb_ref, wcre_ref, wcim_ref, abre_ref, abim_ref, d_ref,
                      y_ref, sre_ref, sim_ref, u_tm, bu, y_tm, st):
    i = pl.program_id(1)
    half = BLOCK_CH // 2

    @pl.when(i == 0)
    def _():
        st[...] = jnp.zeros_like(st)

    for b, u in enumerate((u0, u1, u2, u3)):
        u_tm[0, pl.ds(b, S5_TILE, stride=S5_SEQS), :] = u[:, :half]
        u_tm[1, pl.ds(b, S5_TILE, stride=S5_SEQS), :] = u[:, half:]
    u_all = jnp.concatenate([u_tm[0], u_tm[1]], axis=1)
    bu[...] = _dot(u_all.astype(BF16), wb_ref[...])
    a_re = abre_ref[...]
    a_im = abim_ref[...]

    first = lax.broadcasted_iota(I32, (2 * S5_SEQS, BLOCK_ST), 0) < S5_SEQS

    def body(t2, carry):
        s_re, s_im = carry
        rows = pl.ds(pl.multiple_of(t2 * 2 * S5_SEQS, 2 * S5_SEQS), 2 * S5_SEQS)
        x_re = bu[rows, :BLOCK_ST]
        x_im = bu[rows, BLOCK_ST:]
        p_re = a_re * s_re - a_im * s_im + x_re
        p_im = a_re * s_im + a_im * s_re + x_im
        r_re = pltpu.roll(p_re, S5_SEQS, 0)
        r_im = pltpu.roll(p_im, S5_SEQS, 0)
        q_re = a_re * r_re - a_im * r_im + x_re
        q_im = a_re * r_im + a_im * r_re + x_im
        bu[rows, :BLOCK_ST] = jnp.where(first, p_re, q_re)
        bu[rows, BLOCK_ST:] = jnp.where(first, p_im, q_im)
        return pltpu.roll(q_re, S5_SEQS, 0), pltpu.roll(q_im, S5_SEQS, 0)

    s_re, s_im = lax.fori_loop(0, S5_TILE // 2, body, (st[:, :BLOCK_ST], st[:, BLOCK_ST:]), unroll=2)
    st[:, :BLOCK_ST] = s_re
    st[:, BLOCK_ST:] = s_im
    y = (_dot_nt(bu[:, :BLOCK_ST].astype(BF16), wcre_ref[...])
         - _dot_nt(bu[:, BLOCK_ST:].astype(BF16), wcim_ref[...]) + d_ref[...] * u_all)
    y = _gelu(y)
    y_tm[0] = y[:, :half]
    y_tm[1] = y[:, half:]
    for b in range(S5_SEQS):
        y_ref[b, :, :half] = y_tm[0, pl.ds(b, S5_TILE, stride=S5_SEQS), :]
        y_ref[b, :, half:] = y_tm[1, pl.ds(b, S5_TILE, stride=S5_SEQS), :]

    @pl.when(i == pl.num_programs(1) - 1)
    def _():
        sre_ref[...] = s_re[:S5_SEQS, :]
        sim_ref[...] = s_im[:S5_SEQS, :]


def _s5_prompt(z, wb, wcre, wcim, ab_re, ab_im, d_row, seq_len):
    nt = seq_len // S5_TILE

    def u_spec(b):
        return pl.BlockSpec((S5_TILE, BLOCK_CH), lambda j, i, b=b: (b * nt + i, j))

    rows = S5_TILE * S5_SEQS
    return pl.pallas_call(
        _s5_prompt_kernel,
        grid=(SSM_BLOCKS, nt),
        in_specs=[u_spec(b) for b in range(S5_SEQS)] + [
            pl.BlockSpec((None, BLOCK_CH, 2 * BLOCK_ST), lambda j, i: (j, 0, 0)),
            pl.BlockSpec((None, BLOCK_CH, BLOCK_ST), lambda j, i: (j, 0, 0)),
            pl.BlockSpec((None, BLOCK_CH, BLOCK_ST), lambda j, i: (j, 0, 0)),
            pl.BlockSpec((1, BLOCK_ST), lambda j, i: (0, j)),
            pl.BlockSpec((1, BLOCK_ST), lambda j, i: (0, j)),
            pl.BlockSpec((1, BLOCK_CH), lambda j, i: (0, j)),
        ],
        out_specs=[
            pl.BlockSpec((S5_SEQS, S5_TILE, BLOCK_CH), lambda j, i: (0, i, j)),
            pl.BlockSpec((S5_SEQS, BLOCK_ST), lambda j, i: (0, j)),
            pl.BlockSpec((S5_SEQS, BLOCK_ST), lambda j, i: (0, j)),
        ],
        out_shape=[
            jax.ShapeDtypeStruct((S5_SEQS, seq_len, SSM_WIDTH), F32),
            jax.ShapeDtypeStruct((S5_SEQS, SSM_GROUPS * SSM_STATE), F32),
            jax.ShapeDtypeStruct((S5_SEQS, SSM_GROUPS * SSM_STATE), F32),
        ],
        scratch_shapes=[
            pltpu.VMEM((2, rows, 128), F32),
            pltpu.VMEM((rows, 2 * BLOCK_ST), F32),
            pltpu.VMEM((2, rows, 128), F32),
            pltpu.VMEM((2 * S5_SEQS, 2 * BLOCK_ST), F32),
        ],
        compiler_params=_cparams(("parallel", "arbitrary")),
        name="s5_prompt",
    )(z, z, z, z, wb, wcre, wcim, ab_re, ab_im, d_row)


def _s5_sample_kernel(u_ref, s0re_ref, s0im_ref, wb_ref, wcre_ref, wcim_ref, abre_ref, abim_ref,
                      d_ref, y_ref, sre_ref, sim_ref, u_sl, y_sl):
    half = BLOCK_CH // 2
    n_seq = s0re_ref.shape[0]
    u_sl[0] = u_ref[:, :half]
    u_sl[1] = u_ref[:, half:]
    a_re = abre_ref[...]
    a_im = abim_ref[...]
    s_re = s0re_ref[...]
    s_im = s0im_ref[...]
    for t in range(SAMPLE_SEQ):
        rows = pl.ds(t, n_seq, stride=SAMPLE_SEQ)
        u_t = jnp.concatenate([u_sl[0, rows, :], u_sl[1, rows, :]], axis=1)
        bu = _dot(u_t.astype(BF16), wb_ref[...])
        n_re = a_re * s_re - a_im * s_im + bu[:, :BLOCK_ST]
        n_im = a_re * s_im + a_im * s_re + bu[:, BLOCK_ST:]
        s_re, s_im = n_re, n_im
        y = (_dot_nt(s_re.astype(BF16), wcre_ref[...]) - _dot_nt(s_im.astype(BF16), wcim_ref[...])
             + d_ref[...] * u_t)
        y = _gelu(y)
        y_sl[0, rows, :] = y[:, :half]
        y_sl[1, rows, :] = y[:, half:]
    y_ref[:, :half] = y_sl[0]
    y_ref[:, half:] = y_sl[1]
    sre_ref[...] = s_re
    sim_ref[...] = s_im


def _s5_sample(z, s0_re, s0_im, wb, wcre, wcim, ab_re, ab_im, d_row, row0, n_rows):
    n_seq = s0_re.shape[0]
    st_spec = pl.BlockSpec((n_seq, BLOCK_ST), lambda j: (0, j))
    return pl.pallas_call(
        _s5_sample_kernel,
        grid=(SSM_BLOCKS,),
        in_specs=[
            pl.BlockSpec((n_rows, BLOCK_CH), lambda j: (row0 // n_rows, j)),
            st_spec, st_spec,
            pl.BlockSpec((None, BLOCK_CH, 2 * BLOCK_ST), lambda j: (j, 0, 0)),
            pl.BlockSpec((None, BLOCK_CH, BLOCK_ST), lambda j: (j, 0, 0)),
            pl.BlockSpec((None, BLOCK_CH, BLOCK_ST), lambda j: (j, 0, 0)),
            pl.BlockSpec((1, BLOCK_ST), lambda j: (0, j)),
            pl.BlockSpec((1, BLOCK_ST), lambda j: (0, j)),
            pl.BlockSpec((1, BLOCK_CH), lambda j: (0, j)),
        ],
        out_specs=[pl.BlockSpec((n_rows, BLOCK_CH), lambda j: (0, j)), st_spec, st_spec],
        out_shape=[
            jax.ShapeDtypeStruct((n_rows, SSM_WIDTH), F32),
            jax.ShapeDtypeStruct(s0_re.shape, F32),
            jax.ShapeDtypeStruct(s0_im.shape, F32),
        ],
        scratch_shapes=[pltpu.VMEM((2, n_rows, 128), F32), pltpu.VMEM((2, n_rows, 128), F32)],
        compiler_params=_cparams(("parallel",)),
        name="s5_sample",
    )(z, s0_re, s0_im, wb, wcre, wcim, ab_re, ab_im, d_row)


def _postmix_kernel(hmp_ref, hms_ref, ysp_ref, yss_ref, xp_ref, xs_ref,
                    wglu_ref, bglu_ref, wout_ref, gain_ref, wq_ref, keys_ref,
                    h1_ref, c_ref, sp_ref, ss_ref, *, prompt_tiles):
    i = pl.program_id(0)
    ys = _group_pick(ysp_ref, yss_ref, prompt_tiles)
    hm = _group_pick(hmp_ref, hms_ref, prompt_tiles)
    glu = ys * _sigmoid(_dot(ys.astype(BF16), wglu_ref[...]) + bglu_ref[...])
    mix = (_dot(hm.astype(BF16), wout_ref[:MLSTM_WIDTH, :])
           + _dot(glu.astype(BF16), wout_ref[MLSTM_WIDTH:, :]))
    h1 = _group_pick(xp_ref, xs_ref, prompt_tiles) + mix
    h1_ref[...] = h1
    c = _rmsnorm(h1, gain_ref[...]).astype(BF16)
    c_ref[...] = c
    qp = _dot(c, wq_ref[...])
    scores = []
    width = KEY_GROUP * PEER_HALF
    for g in range(2 * PEER_HEADS // KEY_GROUP):
        sg = _dot_nt(keys_ref[g], qp[:, g * width:(g + 1) * width].astype(BF16))
        scores += [sg[jj * PEER_KEYS:(jj + 1) * PEER_KEYS, :] for jj in range(KEY_GROUP)]

    @pl.when(i < prompt_tiles)
    def _():
        for j, s in enumerate(scores):
            sp_ref[j] = s

    @pl.when(i >= prompt_tiles)
    def _():
        for j, s in enumerate(scores):
            ss_ref[j] = s


KEY_GROUP = 4


def _blockdiag_keys(keys):
    ng = keys.shape[0] // KEY_GROUP
    k = keys.reshape(ng, KEY_GROUP, PEER_KEYS, PEER_HALF)
    eye = jnp.eye(KEY_GROUP, dtype=bool)
    w = jnp.where(eye[None, :, None, :, None], k[:, :, :, None, :], 0.0)
    return w.reshape(ng, KEY_GROUP * PEER_KEYS, KEY_GROUP * PEER_HALF).astype(BF16)


def _postmix(hm_p, hm_s, ys_p, ys_s, x_p, x_s, wglu, bglu, wout, gain, wq, keys, tm=256):
    n_p, n_s = x_p.shape[0], x_s.shape[0]
    t = n_p + n_s
    pt = n_p // tm
    nk = 2 * PEER_HEADS
    return pl.pallas_call(
        functools.partial(_postmix_kernel, prompt_tiles=pt),
        grid=(t // tm,),
        in_specs=_group_specs(tm, MLSTM_WIDTH, pt) + _group_specs(tm, SSM_WIDTH, pt)
        + _group_specs(tm, D_MODEL, pt) + [
            _resident((SSM_WIDTH, SSM_WIDTH), lambda i: (0, 0)),
            _resident((1, SSM_WIDTH), lambda i: (0, 0)),
            _resident((D_MODEL, D_MODEL), lambda i: (0, 0)),
            _resident((1, D_MODEL), lambda i: (0, 0)),
            _resident((D_MODEL, PEER_HEADS * 2 * PEER_HALF), lambda i: (0, 0)),
            _resident(keys.shape, lambda i: (0, 0, 0)),
        ],
        out_specs=[
            pl.BlockSpec((tm, D_MODEL), lambda i: (i, 0)),
            pl.BlockSpec((tm, D_MODEL), lambda i: (i, 0)),
            pl.BlockSpec((nk, PEER_KEYS, tm), lambda i: (0, 0, jnp.minimum(i, pt - 1))),
            pl.BlockSpec((nk, PEER_KEYS, tm), lambda i: (0, 0, jnp.maximum(i - pt, 0))),
        ],
        out_shape=[
            jax.ShapeDtypeStruct((t, D_MODEL), F32),
            jax.ShapeDtypeStruct((t, D_MODEL), BF16),
            jax.ShapeDtypeStruct((nk, PEER_KEYS, n_p), F32),
            jax.ShapeDtypeStruct((nk, PEER_KEYS, n_s), F32),
        ],
        compiler_params=_cparams(("arbitrary",)),
        name="postmix",
    )(hm_p, hm_s, ys_p, ys_s, x_p, x_s, wglu, bglu, wout, gain, wq, keys)


ID_NONE = 1 << 20
TOPK_SUB = 8


def _sort_network(n):
    pairs = []
    p = 1
    while p < n:
        k = p
        while k >= 1:
            for j in range(k % p, n - k, 2 * k):
                for i in range(min(k, n - j - k)):
                    if (i + j) // (2 * p) == (i + j + k) // (2 * p):
                        pairs.append((i + j, i + j + k))
            k //= 2
        p *= 2
    return pairs


_SORT16 = _sort_network(PEER_TOPK)


def _precedes(b, a):
    (vb, ib), (va, ia) = b, a
    return jnp.logical_or(vb > va, jnp.logical_and(vb == va, ib < ia))


def _first_of(a, b):
    sw = _precedes(b, a)
    return jnp.where(sw, b[0], a[0]), jnp.where(sw, b[1], a[1])


def _exchange(items, i, j):
    a, b = items[i], items[j]
    sw = _precedes(b, a)
    items[i] = (jnp.where(sw, b[0], a[0]), jnp.where(sw, b[1], a[1]))
    items[j] = (jnp.where(sw, a[0], b[0]), jnp.where(sw, a[1], b[1]))


def _sort16(items):
    items = list(items)
    for i, j in _SORT16:
        _exchange(items, i, j)
    return items


def _bitonic_merge16(items):
    items = list(items)
    d = PEER_TOPK // 2
    while d >= 1:
        for i in range(PEER_TOPK):
            if i & d == 0:
                _exchange(items, i, i + d)
        d //= 2
    return items


def _merge_top16(a, b):
    return _bitonic_merge16([_first_of(a[i], b[PEER_TOPK - 1 - i]) for i in range(PEER_TOPK)])


def _top16_of_keys(s_ref, half, shape):
    best = None
    for g in range(PEER_KEYS // PEER_TOPK):
        grp = _sort16([(s_ref[half, g * PEER_TOPK + k], jnp.full(shape, g * PEER_TOPK + k, I32))
                       for k in range(PEER_TOPK)])
        best = grp if best is None else _merge_top16(best, grp)
    return best


def _route_head(s_ref):
    shape = s_ref.shape[2:]
    top1 = _top16_of_keys(s_ref, 0, shape)
    top2 = _top16_of_keys(s_ref, 1, shape)

    def pair(i, j):
        return top1[i][0] + top2[j][0], jnp.full(shape, i * PEER_TOPK + j, I32)

    pad = (jnp.full(shape, NEG_INF, F32), jnp.full(shape, ID_NONE, I32))
    g0 = [pair(0, j) for j in range(16)]
    g1 = _bitonic_merge16([pair(1, j) for j in range(8)] + [pair(i, 0) for i in range(15, 7, -1)])
    g2 = _sort16([pair(i, j) for i in range(2, 7) for j in range(PEER_TOPK // (i + 1))])
    g3 = [pair(7, 0), pair(7, 1)] + [pad] * 14
    best = _merge_top16(_merge_top16(g0, g1), _merge_top16(g2, g3))

    mx = best[0][0]
    e1s, e2s, exps = [], [], []
    for k in range(PEER_TOPK):
        v, pid = best[k]
        a = lax.shift_right_logical(pid, 4)
        b = jnp.bitwise_and(pid, PEER_TOPK - 1)
        e1 = jnp.zeros(shape, I32)
        e2 = jnp.zeros(shape, I32)
        for r in range(PEER_TOPK):
            e1 = jnp.where(a == r, top1[r][1], e1)
            e2 = jnp.where(b == r, top2[r][1], e2)
        e1s.append(e1)
        e2s.append(e2)
        exps.append(jnp.exp(v - mx))
    total = exps[0]
    for k in range(1, PEER_TOPK):
        total = total + exps[k]
    return e1s, e2s, [e / total for e in exps]


def _topk_kernel(s_ref, e1_ref, e2_ref, g_ref):
    sub = s_ref.shape[2]
    for k, vals in enumerate(zip(*_route_head(s_ref))):
        for ref, val in zip((e1_ref, e2_ref, g_ref), vals):
            for s in range(sub):
                ref[k:k + 1, s * 128:(s + 1) * 128] = val[s:s + 1, :]


def _topk(scores):
    t = scores.shape[-1]
    ng = t // 128
    sub = min(ng, TOPK_SUB)
    r = PEER_HEADS * PEER_TOPK
    o_spec = pl.BlockSpec((PEER_TOPK, sub * 128), lambda i, h: (h, i))
    return pl.pallas_call(
        _topk_kernel,
        grid=(ng // sub, PEER_HEADS),
        in_specs=[pl.BlockSpec((2, PEER_KEYS, sub, 128), lambda i, h: (h, 0, i, 0))],
        out_specs=[o_spec, o_spec, o_spec],
        out_shape=[
            jax.ShapeDtypeStruct((r, t), I32),
            jax.ShapeDtypeStruct((r, t), I32),
            jax.ShapeDtypeStruct((r, t), F32),
        ],
        compiler_params=_cparams(("parallel", "parallel")),
        name="topk",
    )(scores.reshape(2 * PEER_HEADS, PEER_KEYS, ng, 128))


WB_GROUP = 16
WB_PITCH = 132


def _token_weights(e1_row, e2_row, g_row):
    sub = lax.broadcasted_iota(I32, (PEER_KEYS, PEER_KEYS), 0)
    onehot1 = jnp.where(sub == e1_row, 1.0, 0.0).astype(BF16)
    gated2 = jnp.where(sub == e2_row, 0.5 * g_row, 0.0).astype(BF16)
    return _dot_nt(onehot1, gated2)


def _wbuild_kernel(e1p_ref, e1s_ref, e2p_ref, e2s_ref, gp_ref, gs_ref, w_ref, stage, *, prompt_tiles):
    e1 = jnp.transpose(_group_pick(e1p_ref, e1s_ref, prompt_tiles))
    e2 = jnp.transpose(_group_pick(e2p_ref, e2s_ref, prompt_tiles))
    g = jnp.transpose(_group_pick(gp_ref, gs_ref, prompt_tiles))
    for grp in range(e1.shape[0] // WB_GROUP):
        base = (grp % 2) * WB_GROUP * WB_PITCH
        for tt in range(WB_GROUP):
            t = grp * WB_GROUP + tt
            stage[base + tt * WB_PITCH:base + tt * WB_PITCH + PEER_KEYS, :] = _token_weights(
                e1[t:t + 1, :], e2[t:t + 1, :], g[t:t + 1, :])
        for e in range(PEER_KEYS):
            blk = stage[pl.ds(base + e, WB_GROUP, stride=WB_PITCH), :]
            w_ref[e, grp * WB_GROUP:(grp + 1) * WB_GROUP, :] = blk.astype(BF16)


def _wbuild(routing_p, routing_s, tw=512):
    n_p, n_s = routing_p[0].shape[1], routing_s[0].shape[1]
    t = n_p + n_s
    pt = n_p // tw
    specs = [
        pl.BlockSpec((PEER_KEYS, tw), lambda i: (0, jnp.minimum(i, pt - 1))),
        pl.BlockSpec((PEER_KEYS, tw), lambda i: (0, jnp.maximum(i - pt, 0))),
    ]
    operands = [x for pair in zip(routing_p, routing_s) for x in pair]
    return pl.pallas_call(
        functools.partial(_wbuild_kernel, prompt_tiles=pt),
        grid=(t // tw,),
        in_specs=specs * 3,
        out_specs=pl.BlockSpec((PEER_KEYS, tw, PEER_KEYS), lambda i: (0, i, 0)),
        out_shape=jax.ShapeDtypeStruct((PEER_KEYS, t, PEER_KEYS), BF16),
        scratch_shapes=[pltpu.VMEM((2 * WB_GROUP * WB_PITCH, PEER_KEYS), F32)],
        compiler_params=_cparams(("arbitrary",)),
        name="wbuild",
    )(*operands)


PEER_EB = 256


def _peer_kernel(c_ref, u_ref, v_ref, w_ref, o_ref, s_s):
    j = pl.program_id(1)
    last = pl.num_programs(1) - 1

    def gated():
        parts = []
        for k in range(PEER_EB // PEER_KEYS):
            x = s_s[:, k * PEER_KEYS:(k + 1) * PEER_KEYS]
            t = jnp.tanh(x * (GELU_C1 + GELU_C2 * (x * x)))
            parts.append((x * w_ref[k].astype(F32)) * (1.0 + t))
        return jnp.concatenate(parts, axis=1).astype(BF16)

    def scores():
        return _dot_nt(c_ref[...], u_ref[...].astype(BF16))

    @pl.when(j == 0)
    def _():
        o_ref[...] = jnp.zeros_like(o_ref)
        s_s[...] = scores()

    @pl.when(jnp.logical_and(j > 0, j < last))
    def _():
        wact = gated()
        s_s[...] = scores()
        o_ref[...] += _dot(wact, v_ref[...].astype(BF16))

    @pl.when(j == last)
    def _():
        o_ref[...] += _dot(gated(), v_ref[...].astype(BF16))


def _peer(c, u, v, w3, n_tiles=4):
    t = c.shape[0]
    tm = t // n_tiles
    nb = PEER_EXPERTS // PEER_EB
    return pl.pallas_call(
        _peer_kernel,
        grid=(n_tiles, nb + 1),
        in_specs=[
            pl.BlockSpec((tm, D_MODEL), lambda i, j: (i, 0), pipeline_mode=pl.Buffered(1)),
            pl.BlockSpec((PEER_EB, D_MODEL), lambda i, j: (jnp.minimum(j, nb - 1), 0)),
            pl.BlockSpec((PEER_EB, D_MODEL), lambda i, j: (jnp.maximum(j - 1, 0), 0)),
            pl.BlockSpec((PEER_EB // PEER_KEYS, tm, PEER_KEYS),
                         lambda i, j: (jnp.maximum(j - 1, 0), i, 0)),
        ],
        out_specs=pl.BlockSpec((tm, D_MODEL), lambda i, j: (i, 0), pipeline_mode=pl.Buffered(1)),
        out_shape=jax.ShapeDtypeStruct((t, D_MODEL), F32),
        scratch_shapes=[pltpu.VMEM((tm, PEER_EB), F32)],
        compiler_params=_cparams(("parallel", "arbitrary")),
        name="peer",
    )(c, u, v, w3)


def _tail_kernel(h1_ref, peer_ref, pp_ref, ps_ref, gple_ref, wgate_ref, wproj_ref, gfin_ref,
                 yp_ref, ys_ref, *, prompt_tiles):
    i = pl.program_id(0)
    h2 = h1_ref[...] + peer_ref[...]
    gate = _sigmoid(_dot(_rmsnorm(h2, gple_ref[...]).astype(BF16), wgate_ref[...]))
    e = _dot(_group_pick(pp_ref, ps_ref, prompt_tiles).astype(BF16), wproj_ref[...])
    y = _rmsnorm(h2 + e * gate, gfin_ref[...])

    @pl.when(i < prompt_tiles)
    def _():
        yp_ref[...] = y

    @pl.when(i >= prompt_tiles)
    def _():
        ys_ref[...] = y


def _tail(h1, peer, p_p, p_s, gple, wgate, wproj, gfin, tm=512):
    t = h1.shape[0]
    n_prompt = p_p.shape[0]
    pt = n_prompt // tm
    ple = p_p.shape[1]
    return pl.pallas_call(
        functools.partial(_tail_kernel, prompt_tiles=pt),
        grid=(t // tm,),
        in_specs=[
            pl.BlockSpec((tm, D_MODEL), lambda i: (i, 0)),
            pl.BlockSpec((tm, D_MODEL), lambda i: (i, 0)),
        ] + _group_specs(tm, ple, pt) + [
            _resident((1, D_MODEL), lambda i: (0, 0)),
            _resident((D_MODEL, D_MODEL), lambda i: (0, 0)),
            _resident((ple, D_MODEL), lambda i: (0, 0)),
            _resident((1, D_MODEL), lambda i: (0, 0)),
        ],
        out_specs=[
            pl.BlockSpec((tm, D_MODEL), lambda i: (jnp.minimum(i, pt - 1), 0)),
            pl.BlockSpec((tm, D_MODEL), lambda i: (jnp.maximum(i - pt, 0), 0)),
        ],
        out_shape=[
            jax.ShapeDtypeStruct((n_prompt, D_MODEL), F32),
            jax.ShapeDtypeStruct((t - n_prompt, D_MODEL), F32),
        ],
        compiler_params=_cparams(("arbitrary",)),
        name="tail",
    )(h1, peer, p_p, p_s, gple, wgate, wproj, gfin)


def kernel(x_prompt, x_sample, state_mlstm_C, state_mlstm_n, state_mlstm_m, state_ssm_re, state_ssm_im, p_prompt, p_sample, norm_mix, w_in, b_igate, b_fgate, mlstm_norm, ssm_A_re, ssm_A_im, ssm_B_re, ssm_B_im, ssm_C_re, ssm_C_im, ssm_D, ssm_log_dt, w_glu, b_glu, w_out, norm_ffn, peer_w_q, peer_keys, peer_u, peer_v, norm_ple, w_ple_gate, w_ple_proj, norm_final):
    n_pseq, p_len, _ = x_prompt.shape
    n_sseq, s_len, _ = x_sample.shape
    assert s_len == SAMPLE_SEQ and n_pseq == S5_SEQS and w_in.shape[0] == 1
    n_prompt = n_pseq * p_len
    n_sample = n_sseq * s_len
    row = lambda t: t.reshape(1, -1)

    x_p = x_prompt.reshape(n_prompt, D_MODEL)
    x_s = x_sample.reshape(n_sample, D_MODEL)

    w_bf = w_in[0].astype(BF16)
    n_gate = 2 * HEADS
    w_tail = jnp.concatenate(
        [w_bf[:, Z_U + n_gate:], w_bf[:, Z_U:Z_U + n_gate],
         jnp.zeros((D_MODEL, 128 - n_gate), BF16)], axis=1)
    gbias = jnp.concatenate([b_igate[0], b_fgate[0], jnp.zeros((128 - n_gate,), F32)]).reshape(1, 128)
    z, zt = _in_proj(x_p, x_s, row(norm_mix[0]), w_bf, w_tail, gbias)

    gain_m = row(mlstm_norm[0])
    hm_p, c_p, n_p, m_p = _mlstm_prompt(z, zt, gain_m, n_pseq, p_len)
    nbh = n_sseq * HEADS
    hm_s, c_s, n_s, m_s = _mlstm_sample(
        z, zt, gain_m,
        state_mlstm_C[0].reshape(nbh, HEAD_DIM, HEAD_DIM),
        state_mlstm_n[0].reshape(nbh, 1, HEAD_DIM),
        jnp.broadcast_to(state_mlstm_m[0].reshape(nbh, 1, 1), (nbh, 1, 128)),
        n_prompt, n_sample)

    wb, wcre, wcim, ab_re, ab_im = _s5_params(ssm_A_re[0], ssm_A_im[0], ssm_log_dt[0], ssm_B_re[0],
                                              ssm_B_im[0], ssm_C_re[0], ssm_C_im[0])
    d_row = row(ssm_D[0])
    ys_p, sre_p, sim_p = _s5_prompt(zt, wb, wcre, wcim, ab_re, ab_im, d_row, p_len)
    n_st = SSM_GROUPS * SSM_STATE
    ys_s, sre_s, sim_s = _s5_sample(
        zt, state_ssm_re[0].reshape(n_sseq, n_st), state_ssm_im[0].reshape(n_sseq, n_st),
        wb, wcre, wcim, ab_re, ab_im, d_row, n_prompt, n_sample)

    keys = _blockdiag_keys(peer_keys[0].reshape(2 * PEER_HEADS, PEER_KEYS, PEER_HALF))
    h1, c, scores_p, scores_s = _postmix(
        hm_p, hm_s, ys_p.reshape(n_prompt, SSM_WIDTH), ys_s, x_p, x_s,
        w_glu[0].astype(BF16), row(b_glu[0]), w_out[0].astype(BF16),
        row(norm_ffn[0]), peer_w_q[0].astype(BF16), keys)
    w3 = _wbuild(_topk(scores_p), _topk(scores_s))
    peer = _peer(c, peer_u[0], peer_v[0], w3)

    y_p, y_s = _tail(h1, peer, p_prompt[0].reshape(n_prompt, -1), p_sample[0].reshape(n_sample, -1),
                     row(norm_ple[0]), w_ple_gate[0].astype(BF16), w_ple_proj[0].astype(BF16),
                     row(norm_final))

    st_shape = (1, -1, SSM_GROUPS, SSM_STATE)
    return (
        y_p.reshape(x_prompt.shape), y_s.reshape(x_sample.shape),
        c_p.reshape(1, n_pseq, HEADS, HEAD_DIM, HEAD_DIM), n_p.reshape(1, n_pseq, HEADS, HEAD_DIM),
        m_p[:, 0, 0].reshape(1, n_pseq, HEADS),
        sre_p.reshape(st_shape), sim_p.reshape(st_shape),
        c_s.reshape(1, n_sseq, HEADS, HEAD_DIM, HEAD_DIM), n_s.reshape(1, n_sseq, HEADS, HEAD_DIM),
        m_s[:, 0, 0].reshape(1, n_sseq, HEADS),
        sre_s.reshape(st_shape), sim_s.reshape(st_shape),
    )
```

```python
import functools

import jax
import jax.numpy as jnp
from jax import lax
from jax.experimental import pallas as pl
from jax.experimental.pallas import tpu as pltpu

F32 = jnp.float32
BF16 = jnp.bfloat16
I32 = jnp.int32

EPS = 1e-6
D_MODEL = 2048
HEADS = 4
HEAD_DIM = 256
MLSTM_WIDTH = HEADS * HEAD_DIM
SSM_WIDTH = 1024
SSM_GROUPS = 64
SSM_STATE = 64
SSM_GROUP = 16
GROUPS_PER_BLOCK = 16
SSM_BLOCKS = SSM_GROUPS // GROUPS_PER_BLOCK
BLOCK_CH = GROUPS_PER_BLOCK * SSM_GROUP
BLOCK_ST = GROUPS_PER_BLOCK * SSM_STATE
PEER_HEADS = 8
PEER_KEYS = 128
PEER_TOPK = 16
PEER_HALF = 64
PEER_EXPERTS = PEER_KEYS * PEER_KEYS

Z_Q, Z_K, Z_V, Z_O, Z_U = 0, 1024, 2048, 3072, 4096
ZT_G = 1024
ZT_COLS = ZT_G + 128
IN_PROJ_COLS = 2048

MLSTM_CHUNK = 256
K_SCALE = HEAD_DIM ** -0.5
NEG_INF = float("-inf")

VMEM_LIMIT = 56 * 1024 * 1024


def _cparams(semantics):
    return pltpu.CompilerParams(dimension_semantics=semantics, vmem_limit_bytes=VMEM_LIMIT)


def _resident(shape, index_map):
    return pl.BlockSpec(shape, index_map, pipeline_mode=pl.Buffered(1))


def _sigmoid(x):
    return 1.0 / (1.0 + jnp.exp(-x))


GELU_C1 = 0.7978845608028654
GELU_C2 = GELU_C1 * 0.044715


def _gelu(x):
    return x * (0.5 * (1.0 + jnp.tanh(GELU_C1 * (x + 0.044715 * (x * x * x)))))


def _rmsnorm(x, gain):
    return x * lax.rsqrt(jnp.mean(x * x, axis=-1, keepdims=True) + EPS) * gain


def _dot(a, b):
    return jnp.dot(a, b, preferred_element_type=F32)


def _dot_nt(a, b):
    return lax.dot_general(a, b, (((1,), (1,)), ((), ())), preferred_element_type=F32)


def _dot_tn(a, b):
    return lax.dot_general(a, b, (((0,), (0,)), ((), ())), preferred_element_type=F32)


def _group_specs(tm, width, prompt_tiles):
    return [
        pl.BlockSpec((tm, width), lambda i: (jnp.minimum(i, prompt_tiles - 1), 0)),
        pl.BlockSpec((tm, width), lambda i: (jnp.maximum(i - prompt_tiles, 0), 0)),
    ]


def _group_pick(prompt_ref, sample_ref, prompt_tiles):
    return jnp.where(pl.program_id(0) < prompt_tiles, prompt_ref[...], sample_ref[...])


def _in_proj_kernel(xp_ref, xs_ref, gain_ref, wm_ref, wt_ref, gbias_ref, zm_ref, zt_ref, a_s,
                    *, prompt_tiles, main_steps):
    j = pl.program_id(1)

    @pl.when(j == 0)
    def _():
        a_s[...] = _rmsnorm(_group_pick(xp_ref, xs_ref, prompt_tiles), gain_ref[...]).astype(BF16)

    for step in range(main_steps):
        @pl.when(j == step)
        def _(base=step * IN_PROJ_COLS):
            for c0 in range(0, IN_PROJ_COLS, 512):
                zm_ref[:, c0:c0 + 512] = _dot(a_s[...], wm_ref[:, base + c0:base + c0 + 512])

    @pl.when(j == main_steps)
    def _():
        a = a_s[...]
        for c0 in range(0, ZT_G, 512):
            zt_ref[:, c0:c0 + 512] = _dot(a, wt_ref[:, c0:c0 + 512])
        gz = _dot(a, wt_ref[:, ZT_G:]) + gbias_ref[...]
        lane = lax.broadcasted_iota(I32, gz.shape, 1)
        log_f = jnp.minimum(gz, 0.0) - jnp.log(1.0 + jnp.exp(-jnp.abs(gz)))
        zt_ref[:, ZT_G:] = jnp.where(lane < HEADS, gz, log_f)


def _in_proj(x_p, x_s, gain, w_bf, w_tail, gbias, tm=512):
    t = x_p.shape[0] + x_s.shape[0]
    pt = x_p.shape[0] // tm
    ms = Z_U // IN_PROJ_COLS
    return pl.pallas_call(
        functools.partial(_in_proj_kernel, prompt_tiles=pt, main_steps=ms),
        grid=(t // tm, ms + 1),
        in_specs=[
            pl.BlockSpec((tm, D_MODEL), lambda i, j: (jnp.minimum(i, pt - 1), 0)),
            pl.BlockSpec((tm, D_MODEL), lambda i, j: (jnp.maximum(i - pt, 0), 0),
                         pipeline_mode=pl.Buffered(1)),
            _resident((1, D_MODEL), lambda i, j: (0, 0)),
            _resident((D_MODEL, Z_U), lambda i, j: (0, 0)),
            _resident((D_MODEL, ZT_COLS), lambda i, j: (0, 0)),
            _resident((1, 128), lambda i, j: (0, 0)),
        ],
        out_specs=[
            pl.BlockSpec((tm, IN_PROJ_COLS), lambda i, j: (i, jnp.minimum(j, ms - 1))),
            pl.BlockSpec((tm, ZT_COLS), lambda i, j: (i, 0)),
        ],
        out_shape=[
            jax.ShapeDtypeStruct((t, Z_U), F32),
            jax.ShapeDtypeStruct((t, ZT_COLS), F32),
        ],
        scratch_shapes=[pltpu.VMEM((tm, D_MODEL), BF16)],
        compiler_params=_cparams(("arbitrary", "arbitrary")),
        name="in_proj",
    )(x_p, x_s, gain, w_bf, w_tail, gbias)


def _mlstm_chunks(items):
    L = items[0][0].shape[0]
    row = lax.broadcasted_iota(I32, (L, L), 0)
    col = lax.broadcasted_iota(I32, (L, L), 1)
    diag = row == col
    causal = col <= row

    gates = []
    for (_, _, _, ig_col, lf_col, _, _, m_state) in items:
        lf_row = jnp.sum(jnp.where(diag, lf_col, 0.0), axis=0, keepdims=True)
        ig_row = jnp.sum(jnp.where(diag, ig_col, 0.0), axis=0, keepdims=True)
        b_col = jnp.sum(jnp.where(causal, lf_row, 0.0), axis=1, keepdims=True)
        b_row = jnp.sum(jnp.where(row <= col, lf_col, 0.0), axis=0, keepdims=True)
        r_row = ig_row - b_row
        r_col = ig_col - b_col
        run_max = jnp.max(jnp.where(causal, r_row, NEG_INF), axis=1, keepdims=True)
        m_run = jnp.maximum(m_state, run_max)
        w_intra = jnp.exp(jnp.where(causal, r_row - m_run, NEG_INF))
        w_inter = jnp.exp(m_state - m_run)
        gates.append((b_col, r_col, m_run, w_intra, w_inter))

    products = []
    for (q, k, v, _, _, c_get, _, _) in items:
        qb = q.astype(BF16)
        vb = v.astype(BF16)
        products.append((_dot_nt(qb, k.astype(BF16)), _dot(qb, c_get().astype(BF16)), vb))

    heads = []
    for (q, _, _, _, _, _, n_state, _), (b_col, _, m_run, w_intra, w_inter), (qk, qc, vb) in zip(
            items, gates, products):
        s = qk * w_intra
        num = w_inter * qc + _dot(s.astype(BF16), vb)
        qn = jnp.sum(q * n_state, axis=1, keepdims=True)
        den = w_inter * qn + jnp.sum(s, axis=1, keepdims=True)
        heads.append(num / jnp.maximum(jnp.abs(den), jnp.exp(-(b_col + m_run))))

    out = []
    for (_, k, _, _, _, c_get, n_state, m_state), (b_col, r_col, m_run, _, _), (_, _, vb), hb in zip(
            items, gates, products, heads):
        m_last = m_run[L - 1:L, :]
        m_new = b_col[L - 1:L, :] + m_last
        kw = k * jnp.exp(r_col - m_last)
        sc = jnp.exp(m_state - m_last)
        c_new = sc * c_get() + _dot_tn(kw.astype(BF16), vb)
        n_new = sc * n_state + jnp.sum(kw, axis=0, keepdims=True)
        out.append((hb, c_new, n_new, m_new))
    return out


def _head_out(hb, gain, o_pre):
    return hb * lax.rsqrt(jnp.mean(hb * hb, axis=-1, keepdims=True) + EPS) * gain * _sigmoid(o_pre)


def _mlstm_prompt_kernel(q_ref, k_ref, v_ref, o_ref, g_ref, gain_ref,
                         hm_ref, c_out_ref, n_out_ref, m_out_ref, c_s, n_s, m_s):
    h0 = pl.program_id(1) * PROMPT_HEADS
    c_s[...] = jnp.zeros_like(c_s)
    n_s[...] = jnp.zeros_like(n_s)
    m_s[...] = jnp.zeros_like(m_s)

    def body(ci, carry):
        rows = pl.ds(pl.multiple_of(ci * MLSTM_CHUNK, MLSTM_CHUNK), MLSTM_CHUNK)
        gates = g_ref[rows, :]
        lane = lax.broadcasted_iota(I32, gates.shape, 1)
        items = []
        for hh in range(PROMPT_HEADS):
            cs = slice(hh * HEAD_DIM, (hh + 1) * HEAD_DIM)
            ig_col = jnp.sum(jnp.where(lane == h0 + hh, gates, 0.0), axis=1, keepdims=True)
            lf_col = jnp.sum(jnp.where(lane == h0 + hh + HEADS, gates, 0.0), axis=1, keepdims=True)
            items.append((q_ref[rows, cs], k_ref[rows, cs] * K_SCALE, v_ref[rows, cs], ig_col, lf_col,
                          functools.partial(lambda r, i: r[i], c_s, hh), n_s[hh], m_s[hh][:, 0:1]))
        for hh, (hb, c_new, n_new, m_new) in enumerate(_mlstm_chunks(items)):
            cs = slice(hh * HEAD_DIM, (hh + 1) * HEAD_DIM)
            c_s[hh] = c_new
            n_s[hh] = n_new
            m_s[hh] = jnp.broadcast_to(m_new, (1, 128))
            hm_ref[rows, cs] = _head_out(hb, gain_ref[:, cs], o_ref[rows, cs])
        return carry

    lax.fori_loop(0, q_ref.shape[0] // MLSTM_CHUNK, body, 0)
    c_out_ref[...] = c_s[...]
    n_out_ref[...] = n_s[...]
    m_out_ref[...] = m_s[...]


PROMPT_HEADS = 2


def _mlstm_prompt(z, zt, gain, n_seq, seq_len):
    width = PROMPT_HEADS * HEAD_DIM
    hsteps = HEADS // PROMPT_HEADS

    def sec(off):
        return pl.BlockSpec((seq_len, width), lambda b, h, o=off // width: (b, o + h))

    def state(*tail):
        return pl.BlockSpec((PROMPT_HEADS,) + tail, lambda b, h: (b * hsteps + h,) + (0,) * len(tail))

    nbh = n_seq * HEADS
    return pl.pallas_call(
        _mlstm_prompt_kernel,
        grid=(n_seq, hsteps),
        in_specs=[
            sec(Z_Q), sec(Z_K), sec(Z_V), sec(Z_O),
            pl.BlockSpec((seq_len, 128), lambda b, h: (b, ZT_G // 128)),
            pl.BlockSpec((1, width), lambda b, h: (0, h)),
        ],
        out_specs=[
            pl.BlockSpec((seq_len, width), lambda b, h: (b, h)),
            state(HEAD_DIM, HEAD_DIM), state(1, HEAD_DIM), state(1, 128),
        ],
        out_shape=[
            jax.ShapeDtypeStruct((n_seq * seq_len, MLSTM_WIDTH), F32),
            jax.ShapeDtypeStruct((nbh, HEAD_DIM, HEAD_DIM), F32),
            jax.ShapeDtypeStruct((nbh, 1, HEAD_DIM), F32),
            jax.ShapeDtypeStruct((nbh, 1, 128), F32),
        ],
        scratch_shapes=[
            pltpu.VMEM((PROMPT_HEADS, HEAD_DIM, HEAD_DIM), F32),
            pltpu.VMEM((PROMPT_HEADS, 1, HEAD_DIM), F32),
            pltpu.VMEM((PROMPT_HEADS, 1, 128), F32),
        ],
        compiler_params=_cparams(("parallel", "parallel")),
        name="mlstm_prompt",
    )(z, z, z, z, zt, gain)


SAMPLE_SEQ = 4
SAMPLE_ROWS = 16


def _mlstm_sample_kernel(z_ref, g_ref, gain_ref, c_in_ref, n_in_ref, m_in_ref,
                         hm_ref, c_out_ref, n_out_ref, m_out_ref):
    row = lax.broadcasted_iota(I32, (8, 1), 0)
    for grp in range(SAMPLE_ROWS // 8):
        r0 = grp * 8
        gates = g_ref[r0:r0 + 8, :]
        items, where = [], []
        for half in range(2):
            lo = half * SAMPLE_SEQ
            live = jnp.logical_and(row >= lo, row < lo + SAMPLE_SEQ)
            for h in range(HEADS):
                j = (grp * 2 + half) * HEADS + h
                q = z_ref[r0:r0 + 8, Z_Q + h * HEAD_DIM:Z_Q + (h + 1) * HEAD_DIM]
                k = z_ref[r0:r0 + 8, Z_K + h * HEAD_DIM:Z_K + (h + 1) * HEAD_DIM]
                v = z_ref[r0:r0 + 8, Z_V + h * HEAD_DIM:Z_V + (h + 1) * HEAD_DIM]
                k = jnp.where(live, k * K_SCALE, 0.0)
                v = jnp.where(live, v, 0.0)
                ig_col = jnp.where(live, gates[:, h:h + 1], NEG_INF)
                lf_col = jnp.where(live, gates[:, HEADS + h:HEADS + h + 1], 0.0)
                items.append((q, k, v, ig_col, lf_col, functools.partial(lambda r, i: r[i], c_in_ref, j),
                              n_in_ref[j], m_in_ref[j][:, 0:1]))
                where.append((j, lo, h))
        for (j, lo, h), (hb, c_new, n_new, m_new) in zip(where, _mlstm_chunks(items)):
            cs = slice(h * HEAD_DIM, (h + 1) * HEAD_DIM)
            o = z_ref[r0:r0 + 8, Z_O + h * HEAD_DIM:Z_O + (h + 1) * HEAD_DIM]
            out = _head_out(hb, gain_ref[:, cs], o)
            hm_ref[r0 + lo:r0 + lo + SAMPLE_SEQ, cs] = out[lo:lo + SAMPLE_SEQ, :]
            c_out_ref[j] = c_new
            n_out_ref[j] = n_new
            m_out_ref[j] = jnp.broadcast_to(m_new, (1, 128))


def _mlstm_sample(z, zt, gain, c0, n0, m0, row0, n_rows):
    nbh = c0.shape[0]
    per = SAMPLE_ROWS // SAMPLE_SEQ * HEADS
    rb0 = row0 // SAMPLE_ROWS
    return pl.pallas_call(
        _mlstm_sample_kernel,
        grid=(n_rows // SAMPLE_ROWS,),
        in_specs=[
            pl.BlockSpec((SAMPLE_ROWS, Z_U), lambda i: (rb0 + i, 0)),
            pl.BlockSpec((SAMPLE_ROWS, 128), lambda i: (rb0 + i, ZT_G // 128)),
            pl.BlockSpec((1, MLSTM_WIDTH), lambda i: (0, 0)),
            pl.BlockSpec((per, HEAD_DIM, HEAD_DIM), lambda i: (i, 0, 0)),
            pl.BlockSpec((per, 1, HEAD_DIM), lambda i: (i, 0, 0)),
            pl.BlockSpec((per, 1, 128), lambda i: (i, 0, 0)),
        ],
        out_specs=[
            pl.BlockSpec((SAMPLE_ROWS, MLSTM_WIDTH), lambda i: (i, 0)),
            pl.BlockSpec((per, HEAD_DIM, HEAD_DIM), lambda i: (i, 0, 0)),
            pl.BlockSpec((per, 1, HEAD_DIM), lambda i: (i, 0, 0)),
            pl.BlockSpec((per, 1, 128), lambda i: (i, 0, 0)),
        ],
        out_shape=[
            jax.ShapeDtypeStruct((n_rows, MLSTM_WIDTH), F32),
            jax.ShapeDtypeStruct((nbh, HEAD_DIM, HEAD_DIM), F32),
            jax.ShapeDtypeStruct((nbh, 1, HEAD_DIM), F32),
            jax.ShapeDtypeStruct((nbh, 1, 128), F32),
        ],
        compiler_params=_cparams(("parallel",)),
        name="mlstm_sample",
    )(z, zt, gain, c0, n0, m0)


def _s5_discretise(a_re, a_im, log_dt):
    dt = jnp.exp(log_dt)
    mag = jnp.exp(dt * a_re)
    ab_re = mag * jnp.cos(dt * a_im)
    ab_im = mag * jnp.sin(dt * a_im)
    den = a_re * a_re + a_im * a_im
    xr = ab_re - 1.0
    f_re = (xr * a_re + ab_im * a_im) / den
    f_im = (ab_im * a_re - xr * a_im) / den
    return ab_re, ab_im, f_re, f_im


def _s5_param_kernel(are, aim, ldt, bre, bim, cre, cim, are_r, aim_r, ldt_r,
                     wb_o, wcre_o, wcim_o, ab_re_o, ab_im_o):
    _, _, f_re, f_im = _s5_discretise(are[...], aim[...], ldt[...])
    bb = (f_re * bre[...] - f_im * bim[...], f_re * bim[...] + f_im * bre[...])
    row_g = lax.shift_right_logical(lax.broadcasted_iota(I32, (BLOCK_CH, BLOCK_ST), 0), 4)
    lane_g = lax.shift_right_logical(lax.broadcasted_iota(I32, (BLOCK_CH, BLOCK_ST), 1), 6)
    keep = row_g == lane_g

    def spread(src):
        return jnp.where(keep, jnp.concatenate([src] * (BLOCK_ST // 128), axis=1), 0.0).astype(BF16)

    for j in range(SSM_BLOCKS):
        rows = slice(j * BLOCK_CH, (j + 1) * BLOCK_CH)
        for ri in range(2):
            wb_o[j, :, ri * BLOCK_ST:(ri + 1) * BLOCK_ST] = spread(bb[ri][rows, :])
        wcre_o[j] = spread(cre[rows, :])
        wcim_o[j] = spread(cim[rows, :])
    ab_re, ab_im, _, _ = _s5_discretise(are_r[...], aim_r[...], ldt_r[...])
    ab_re_o[...] = ab_re
    ab_im_o[...] = ab_im


def _s5_params(a_re, a_im, log_dt, b_re, b_im, c_re, c_im):
    g, p = a_re.shape
    c = b_re.shape[-1]
    dup = lambda t: jnp.concatenate([t] * (128 // t.shape[-1]), axis=-1)
    by_channel = lambda t: dup(jnp.repeat(t, c, axis=0))
    b_rows = lambda t: dup(jnp.transpose(t, (0, 2, 1)).reshape(g * c, p))
    c_rows = lambda t: dup(t.reshape(g * c, p))
    ldt =jnp.broadcast_to(jnp.repeat(log_dt, c)[:, None], (g * c, 128))
    ldt_r = jnp.broadcast_to(log_dt[:, None], (g, p)).reshape(1, g * p)
    rowv = jax.ShapeDtypeStruct((1, g * p), F32)
    return pl.pallas_call(
        _s5_param_kernel,
        out_shape=[
            jax.ShapeDtypeStruct((SSM_BLOCKS, BLOCK_CH, 2 * BLOCK_ST), BF16),
            jax.ShapeDtypeStruct((SSM_BLOCKS, BLOCK_CH, BLOCK_ST), BF16),
            jax.ShapeDtypeStruct((SSM_BLOCKS, BLOCK_CH, BLOCK_ST), BF16),
            rowv, rowv,
        ],
        compiler_params=pltpu.CompilerParams(vmem_limit_bytes=VMEM_LIMIT),
        name="s5_params",
    )(by_channel(a_re), by_channel(a_im), ldt, b_rows(b_re), b_rows(b_im), c_rows(c_re), c_rows(c_im),
      a_re.reshape(1, g * p), a_im.reshape(1, g * p), ldt_r)


S5_TILE = 512
S5_SEQS = 4


def _s5_prompt_kernel(u0, u1, u2, u3, wb_ref, wcre_ref, wcim_ref, abre_ref, abim_ref, d_ref,
                      y_ref, sre_ref, sim_ref, u_tm, bu, y_tm, st):
    i = pl.program_id(1)
    half = BLOCK_CH // 2

    @pl.when(i == 0)
    def _():
        st[...] = jnp.zeros_like(st)

    for b, u in enumerate((u0, u1, u2, u3)):
        u_tm[0, pl.ds(b, S5_TILE, stride=S5_SEQS), :] = u[:, :half]
        u_tm[1, pl.ds(b, S5_TILE, stride=S5_SEQS), :] = u[:, half:]
    u_all = jnp.concatenate([u_tm[0], u_tm[1]], axis=1)
    bu[...] = _dot(u_all.astype(BF16), wb_ref[...])
    a_re = abre_ref[...]
    a_im = abim_ref[...]

    first = lax.broadcasted_iota(I32, (2 * S5_SEQS, BLOCK_ST), 0) < S5_SEQS

    def body(t2, carry):
        s_re, s_im = carry
        rows = pl.ds(pl.multiple_of(t2 * 2 * S5_SEQS, 2 * S5_SEQS), 2 * S5_SEQS)
        x_re = bu[rows, :BLOCK_ST]
        x_im = bu[rows, BLOCK_ST:]
        p_re = a_re * s_re - a_im * s_im + x_re
        p_im = a_re * s_im + a_im * s_re + x_im
        r_re = pltpu.roll(p_re, S5_SEQS, 0)
        r_im = pltpu.roll(p_im, S5_SEQS, 0)
        q_re = a_re * r_re - a_im * r_im + x_re
        q_im = a_re * r_im + a_im * r_re + x_im
        bu[rows, :BLOCK_ST] = jnp.where(first, p_re, q_re)
        bu[rows, BLOCK_ST:] = jnp.where(first, p_im, q_im)
        return pltpu.roll(q_re, S5_SEQS, 0), pltpu.roll(q_im, S5_SEQS, 0)

    s_re, s_im = lax.fori_loop(0, S5_TILE // 2, body, (st[:, :BLOCK_ST], st[:, BLOCK_ST:]), unroll=2)
    st[:, :BLOCK_ST] = s_re
    st[:, BLOCK_ST:] = s_im
    y = (_dot_nt(bu[:, :BLOCK_ST].astype(BF16), wcre_ref[...])
         - _dot_nt(bu[:, BLOCK_ST:].astype(BF16), wcim_ref[...]) + d_ref[...] * u_all)
    y = _gelu(y)
    y_tm[0] = y[:, :half]
    y_tm[1] = y[:, half:]
    for b in range(S5_SEQS):
        y_ref[b, :, :half] = y_tm[0, pl.ds(b, S5_TILE, stride=S5_SEQS), :]
        y_ref[b, :, half:] = y_tm[1, pl.ds(b, S5_TILE, stride=S5_SEQS), :]

    @pl.when(i == pl.num_programs(1) - 1)
    def _():
        sre_ref[...] = s_re[:S5_SEQS, :]
        sim_ref[...] = s_im[:S5_SEQS, :]


def _s5_prompt(z, wb, wcre, wcim, ab_re, ab_im, d_row, seq_len):
    nt = seq_len // S5_TILE

    def u_spec(b):
        return pl.BlockSpec((S5_TILE, BLOCK_CH), lambda j, i, b=b: (b * nt + i, j))

    rows = S5_TILE * S5_SEQS
    return pl.pallas_call(
        _s5_prompt_kernel,
        grid=(SSM_BLOCKS, nt),
        in_specs=[u_spec(b) for b in range(S5_SEQS)] + [
            pl.BlockSpec((None, BLOCK_CH, 2 * BLOCK_ST), lambda j, i: (j, 0, 0)),
            pl.BlockSpec((None, BLOCK_CH, BLOCK_ST), lambda j, i: (j, 0, 0)),
            pl.BlockSpec((None, BLOCK_CH, BLOCK_ST), lambda j, i: (j, 0, 0)),
            pl.BlockSpec((1, BLOCK_ST), lambda j, i: (0, j)),
            pl.BlockSpec((1, BLOCK_ST), lambda j, i: (0, j)),
            pl.BlockSpec((1, BLOCK_CH), lambda j, i: (0, j)),
        ],
        out_specs=[
            pl.BlockSpec((S5_SEQS, S5_TILE, BLOCK_CH), lambda j, i: (0, i, j)),
            pl.BlockSpec((S5_SEQS, BLOCK_ST), lambda j, i: (0, j)),
            pl.BlockSpec((S5_SEQS, BLOCK_ST), lambda j, i: (0, j)),
        ],
        out_shape=[
            jax.ShapeDtypeStruct((S5_SEQS, seq_len, SSM_WIDTH), F32),
            jax.ShapeDtypeStruct((S5_SEQS, SSM_GROUPS * SSM_STATE), F32),
            jax.ShapeDtypeStruct((S5_SEQS, SSM_GROUPS * SSM_STATE), F32),
        ],
        scratch_shapes=[
            pltpu.VMEM((2, rows, 128), F32),
            pltpu.VMEM((rows, 2 * BLOCK_ST), F32),
            pltpu.VMEM((2, rows, 128), F32),
            pltpu.VMEM((2 * S5_SEQS, 2 * BLOCK_ST), F32),
        ],
        compiler_params=_cparams(("parallel", "arbitrary")),
        name="s5_prompt",
    )(z, z, z, z, wb, wcre, wcim, ab_re, ab_im, d_row)


def _s5_sample_kernel(u_ref, s0re_ref, s0im_ref, wb_ref, wcre_ref, wcim_ref, abre_ref, abim_ref,
                      d_ref, y_ref, sre_ref, sim_ref, u_sl, y_sl):
    half = BLOCK_CH // 2
    n_seq = s0re_ref.shape[0]
    u_sl[0] = u_ref[:, :half]
    u_sl[1] = u_ref[:, half:]
    a_re = abre_ref[...]
    a_im = abim_ref[...]
    s_re = s0re_ref[...]
    s_im = s0im_ref[...]
    for t in range(SAMPLE_SEQ):
        rows = pl.ds(t, n_seq, stride=SAMPLE_SEQ)
        u_t = jnp.concatenate([u_sl[0, rows, :], u_sl[1, rows, :]], axis=1)
        bu = _dot(u_t.astype(BF16), wb_ref[...])
        n_re = a_re * s_re - a_im * s_im + bu[:, :BLOCK_ST]
        n_im = a_re * s_im + a_im * s_re + bu[:, BLOCK_ST:]
        s_re, s_im = n_re, n_im
        y = (_dot_nt(s_re.astype(BF16), wcre_ref[...]) - _dot_nt(s_im.astype(BF16), wcim_ref[...])
             + d_ref[...] * u_t)
        y = _gelu(y)
        y_sl[0, rows, :] = y[:, :half]
        y_sl[1, rows, :] = y[:, half:]
    y_ref[:, :half] = y_sl[0]
    y_ref[:, half:] = y_sl[1]
    sre_ref[...] = s_re
    sim_ref[...] = s_im


def _s5_sample(z, s0_re, s0_im, wb, wcre, wcim, ab_re, ab_im, d_row, row0, n_rows):
    n_seq = s0_re.shape[0]
    st_spec = pl.BlockSpec((n_seq, BLOCK_ST), lambda j: (0, j))
    return pl.pallas_call(
        _s5_sample_kernel,
        grid=(SSM_BLOCKS,),
        in_specs=[
            pl.BlockSpec((n_rows, BLOCK_CH), lambda j: (row0 // n_rows, j)),
            st_spec, st_spec,
            pl.BlockSpec((None, BLOCK_CH, 2 * BLOCK_ST), lambda j: (j, 0, 0)),
            pl.BlockSpec((None, BLOCK_CH, BLOCK_ST), lambda j: (j, 0, 0)),
            pl.BlockSpec((None, BLOCK_CH, BLOCK_ST), lambda j: (j, 0, 0)),
            pl.BlockSpec((1, BLOCK_ST), lambda j: (0, j)),
            pl.BlockSpec((1, BLOCK_ST), lambda j: (0, j)),
            pl.BlockSpec((1, BLOCK_CH), lambda j: (0, j)),
        ],
        out_specs=[pl.BlockSpec((n_rows, BLOCK_CH), lambda j: (0, j)), st_spec, st_spec],
        out_shape=[
            jax.ShapeDtypeStruct((n_rows, SSM_WIDTH), F32),
            jax.ShapeDtypeStruct(s0_re.shape, F32),
            jax.ShapeDtypeStruct(s0_im.shape, F32),
        ],
        scratch_shapes=[pltpu.VMEM((2, n_rows, 128), F32), pltpu.VMEM((2, n_rows, 128), F32)],
        compiler_params=_cparams(("parallel",)),
        name="s5_sample",
    )(z, s0_re, s0_im, wb, wcre, wcim, ab_re, ab_im, d_row)


def _postmix_kernel(hmp_ref, hms_ref, ysp_ref, yss_ref, xp_ref, xs_ref,
                    wglu_ref, bglu_ref, wout_ref, gain_ref, wq_ref, keys_ref,
                    h1_ref, c_ref, sp_ref, ss_ref, *, prompt_tiles):
    i = pl.program_id(0)
    ys = _group_pick(ysp_ref, yss_ref, prompt_tiles)
    hm = _group_pick(hmp_ref, hms_ref, prompt_tiles)
    glu = ys * _sigmoid(_dot(ys.astype(BF16), wglu_ref[...]) + bglu_ref[...])
    mix = (_dot(hm.astype(BF16), wout_ref[:MLSTM_WIDTH, :])
           + _dot(glu.astype(BF16), wout_ref[MLSTM_WIDTH:, :]))
    h1 = _group_pick(xp_ref, xs_ref, prompt_tiles) + mix
    h1_ref[...] = h1
    c = _rmsnorm(h1, gain_ref[...]).astype(BF16)
    c_ref[...] = c
    qp = _dot(c, wq_ref[...])
    scores = []
    width = KEY_GROUP * PEER_HALF
    for g in range(2 * PEER_HEADS // KEY_GROUP):
        sg = _dot_nt(keys_ref[g], qp[:, g * width:(g + 1) * width].astype(BF16))
        scores += [sg[jj * PEER_KEYS:(jj + 1) * PEER_KEYS, :] for jj in range(KEY_GROUP)]

    @pl.when(i < prompt_tiles)
    def _():
        for j, s in enumerate(scores):
            sp_ref[j] = s

    @pl.when(i >= prompt_tiles)
    def _():
        for j, s in enumerate(scores):
            ss_ref[j] = s


KEY_GROUP = 4


def _blockdiag_keys(keys):
    ng = keys.shape[0] // KEY_GROUP
    k = keys.reshape(ng, KEY_GROUP, PEER_KEYS, PEER_HALF)
    eye = jnp.eye(KEY_GROUP, dtype=bool)
    w = jnp.where(eye[None, :, None, :, None], k[:, :, :, None, :], 0.0)
    return w.reshape(ng, KEY_GROUP * PEER_KEYS, KEY_GROUP * PEER_HALF).astype(BF16)


def _postmix(hm_p, hm_s, ys_p, ys_s, x_p, x_s, wglu, bglu, wout, gain, wq, keys, tm=256):
    n_p, n_s = x_p.shape[0], x_s.shape[0]
    t = n_p + n_s
    pt = n_p // tm
    nk = 2 * PEER_HEADS
    return pl.pallas_call(
        functools.partial(_postmix_kernel, prompt_tiles=pt),
        grid=(t // tm,),
        in_specs=_group_specs(tm, MLSTM_WIDTH, pt) + _group_specs(tm, SSM_WIDTH, pt)
        + _group_specs(tm, D_MODEL, pt) + [
            _resident((SSM_WIDTH, SSM_WIDTH), lambda i: (0, 0)),
            _resident((1, SSM_WIDTH), lambda i: (0, 0)),
            _resident((D_MODEL, D_MODEL), lambda i: (0, 0)),
            _resident((1, D_MODEL), lambda i: (0, 0)),
            _resident((D_MODEL, PEER_HEADS * 2 * PEER_HALF), lambda i: (0, 0)),
            _resident(keys.shape, lambda i: (0, 0, 0)),
        ],
        out_specs=[
            pl.BlockSpec((tm, D_MODEL), lambda i: (i, 0)),
            pl.BlockSpec((tm, D_MODEL), lambda i: (i, 0)),
            pl.BlockSpec((nk, PEER_KEYS, tm), lambda i: (0, 0, jnp.minimum(i, pt - 1))),
            pl.BlockSpec((nk, PEER_KEYS, tm), lambda i: (0, 0, jnp.maximum(i - pt, 0))),
        ],
        out_shape=[
            jax.ShapeDtypeStruct((t, D_MODEL), F32),
            jax.ShapeDtypeStruct((t, D_MODEL), BF16),
            jax.ShapeDtypeStruct((nk, PEER_KEYS, n_p), F32),
            jax.ShapeDtypeStruct((nk, PEER_KEYS, n_s), F32),
        ],
        compiler_params=_cparams(("arbitrary",)),
        name="postmix",
    )(hm_p, hm_s, ys_p, ys_s, x_p, x_s, wglu, bglu, wout, gain, wq, keys)


ID_NONE = 1 << 20
TOPK_SUB = 8


def _sort_network(n):
    pairs = []
    p = 1
    while p < n:
        k = p
        while k >= 1:
            for j in range(k % p, n - k, 2 * k):
                for i in range(min(k, n - j - k)):
                    if (i + j) // (2 * p) == (i + j + k) // (2 * p):
                        pairs.append((i + j, i + j + k))
            k //= 2
        p *= 2
    return pairs


_SORT16 = _sort_network(PEER_TOPK)


def _precedes(b, a):
    (vb, ib), (va, ia) = b, a
    return jnp.logical_or(vb > va, jnp.logical_and(vb == va, ib < ia))


def _first_of(a, b):
    sw = _precedes(b, a)
    return jnp.where(sw, b[0], a[0]), jnp.where(sw, b[1], a[1])


def _exchange(items, i, j):
    a, b = items[i], items[j]
    sw = _precedes(b, a)
    items[i] = (jnp.where(sw, b[0], a[0]), jnp.where(sw, b[1], a[1]))
    items[j] = (jnp.where(sw, a[0], b[0]), jnp.where(sw, a[1], b[1]))


def _sort16(items):
    items = list(items)
    for i, j in _SORT16:
        _exchange(items, i, j)
    return items


def _bitonic_merge16(items):
    items = list(items)
    d = PEER_TOPK // 2
    while d >= 1:
        for i in range(PEER_TOPK):
            if i & d == 0:
                _exchange(items, i, i + d)
        d //= 2
    return items


def _merge_top16(a, b):
    return _bitonic_merge16([_first_of(a[i], b[PEER_TOPK - 1 - i]) for i in range(PEER_TOPK)])


def _top16_of_keys(s_ref, half, shape):
    best = None
    for g in range(PEER_KEYS // PEER_TOPK):
        grp = _sort16([(s_ref[half, g * PEER_TOPK + k], jnp.full(shape, g * PEER_TOPK + k, I32))
                       for k in range(PEER_TOPK)])
        best = grp if best is None else _merge_top16(best, grp)
    return best


def _route_head(s_ref):
    shape = s_ref.shape[2:]
    top1 = _top16_of_keys(s_ref, 0, shape)
    top2 = _top16_of_keys(s_ref, 1, shape)

    def pair(i, j):
        return top1[i][0] + top2[j][0], jnp.full(shape, i * PEER_TOPK + j, I32)

    pad = (jnp.full(shape, NEG_INF, F32), jnp.full(shape, ID_NONE, I32))
    g0 = [pair(0, j) for j in range(16)]
    g1 = _bitonic_merge16([pair(1, j) for j in range(8)] + [pair(i, 0) for i in range(15, 7, -1)])
    g2 = _sort16([pair(i, j) for i in range(2, 7) for j in range(PEER_TOPK // (i + 1))])
    g3 = [pair(7, 0), pair(7, 1)] + [pad] * 14
    best = _merge_top16(_merge_top16(g0, g1), _merge_top16(g2, g3))

    mx = best[0][0]
    e1s, e2s, exps = [], [], []
    for k in range(PEER_TOPK):
        v, pid = best[k]
        a = lax.shift_right_logical(pid, 4)
        b = jnp.bitwise_and(pid, PEER_TOPK - 1)
        e1 = jnp.zeros(shape, I32)
        e2 = jnp.zeros(shape, I32)
        for r in range(PEER_TOPK):
            e1 = jnp.where(a == r, top1[r][1], e1)
            e2 = jnp.where(b == r, top2[r][1], e2)
        e1s.append(e1)
        e2s.append(e2)
        exps.append(jnp.exp(v - mx))
    total = exps[0]
    for k in range(1, PEER_TOPK):
        total = total + exps[k]
    return e1s, e2s, [e / total for e in exps]


def _topk_kernel(s_ref, e1_ref, e2_ref, g_ref):
    sub = s_ref.shape[2]
    for k, vals in enumerate(zip(*_route_head(s_ref))):
        for ref, val in zip((e1_ref, e2_ref, g_ref), vals):
            for s in range(sub):
                ref[k:k + 1, s * 128:(s + 1) * 128] = val[s:s + 1, :]


def _topk(scores):
    t = scores.shape[-1]
    ng = t // 128
    sub = min(ng, TOPK_SUB)
    r = PEER_HEADS * PEER_TOPK
    o_spec = pl.BlockSpec((PEER_TOPK, sub * 128), lambda i, h: (h, i))
    return pl.pallas_call(
        _topk_kernel,
        grid=(ng // sub, PEER_HEADS),
        in_specs=[pl.BlockSpec((2, PEER_KEYS, sub, 128), lambda i, h: (h, 0, i, 0))],
        out_specs=[o_spec, o_spec, o_spec],
        out_shape=[
            jax.ShapeDtypeStruct((r, t), I32),
            jax.ShapeDtypeStruct((r, t), I32),
            jax.ShapeDtypeStruct((r, t), F32),
        ],
        compiler_params=_cparams(("parallel", "parallel")),
        name="topk",
    )(scores.reshape(2 * PEER_HEADS, PEER_KEYS, ng, 128))


WB_GROUP = 16
WB_PITCH = 132


def _token_weights(e1_row, e2_row, g_row):
    sub = lax.broadcasted_iota(I32, (PEER_KEYS, PEER_KEYS), 0)
    onehot1 = jnp.where(sub == e1_row, 1.0, 0.0).astype(BF16)
    gated2 = jnp.where(sub == e2_row, 0.5 * g_row, 0.0).astype(BF16)
    return _dot_nt(onehot1, gated2)


def _wbuild_kernel(e1p_ref, e1s_ref, e2p_ref, e2s_ref, gp_ref, gs_ref, w_ref, stage, *, prompt_tiles):
    e1 = jnp.transpose(_group_pick(e1p_ref, e1s_ref, prompt_tiles))
    e2 = jnp.transpose(_group_pick(e2p_ref, e2s_ref, prompt_tiles))
    g = jnp.transpose(_group_pick(gp_ref, gs_ref, prompt_tiles))
    for grp in range(e1.shape[0] // WB_GROUP):
        base = (grp % 2) * WB_GROUP * WB_PITCH
        for tt in range(WB_GROUP):
            t = grp * WB_GROUP + tt
            stage[base + tt * WB_PITCH:base + tt * WB_PITCH + PEER_KEYS, :] = _token_weights(
                e1[t:t + 1, :], e2[t:t + 1, :], g[t:t + 1, :])
        for e in range(PEER_KEYS):
            blk = stage[pl.ds(base + e, WB_GROUP, stride=WB_PITCH), :]
            w_ref[e, grp * WB_GROUP:(grp + 1) * WB_GROUP, :] = blk.astype(BF16)


W_TILE = 128


def _wbuild(routing_p, routing_s, tw=W_TILE):
    n_p, n_s = routing_p[0].shape[1], routing_s[0].shape[1]
    t = n_p + n_s
    pt = n_p // tw
    specs = [
        pl.BlockSpec((PEER_KEYS, tw), lambda i: (0, jnp.minimum(i, pt - 1))),
        pl.BlockSpec((PEER_KEYS, tw), lambda i: (0, jnp.maximum(i - pt, 0))),
    ]
    operands = [x for pair in zip(routing_p, routing_s) for x in pair]
    return pl.pallas_call(
        functools.partial(_wbuild_kernel, prompt_tiles=pt),
        grid=(t // tw,),
        in_specs=specs * 3,
        out_specs=pl.BlockSpec((None, PEER_KEYS, tw, PEER_KEYS), lambda i: (i, 0, 0, 0)),
        out_shape=jax.ShapeDtypeStruct((t // tw, PEER_KEYS, tw, PEER_KEYS), BF16),
        scratch_shapes=[pltpu.VMEM((2 * WB_GROUP * WB_PITCH, PEER_KEYS), F32)],
        compiler_params=_cparams(("arbitrary",)),
        name="wbuild",
    )(*operands)


PEER_EB = 256


def _peer_kernel(c_ref, u_ref, v_ref, w_ref, o_ref, s_s):
    j = pl.program_id(1)
    last = pl.num_programs(1) - 1

    def gated():
        parts = []
        for k in range(PEER_EB // PEER_KEYS):
            rows = []
            for tt in range(w_ref.shape[0]):
                x = s_s[tt * W_TILE:(tt + 1) * W_TILE, k * PEER_KEYS:(k + 1) * PEER_KEYS]
                t = jnp.tanh(x * (GELU_C1 + GELU_C2 * (x * x)))
                rows.append((x * w_ref[tt, k].astype(F32)) * (1.0 + t))
            parts.append(jnp.concatenate(rows, axis=0))
        return jnp.concatenate(parts, axis=1).astype(BF16)

    def scores():
        return _dot_nt(c_ref[...], u_ref[...].astype(BF16))

    @pl.when(j == 0)
    def _():
        o_ref[...] = jnp.zeros_like(o_ref)
        s_s[...] = scores()

    @pl.when(jnp.logical_and(j > 0, j < last))
    def _():
        wact = gated()
        s_s[...] = scores()
        o_ref[...] += _dot(wact, v_ref[...].astype(BF16))

    @pl.when(j == last)
    def _():
        o_ref[...] += _dot(gated(), v_ref[...].astype(BF16))


def _peer(c, u, v, w3, n_tiles=4):
    t = c.shape[0]
    tm = t // n_tiles
    nb = PEER_EXPERTS // PEER_EB
    return pl.pallas_call(
        _peer_kernel,
        grid=(n_tiles, nb + 1),
        in_specs=[
            pl.BlockSpec((tm, D_MODEL), lambda i, j: (i, 0), pipeline_mode=pl.Buffered(1)),
            pl.BlockSpec((PEER_EB, D_MODEL), lambda i, j: (jnp.minimum(j, nb - 1), 0)),
            pl.BlockSpec((PEER_EB, D_MODEL), lambda i, j: (jnp.maximum(j - 1, 0), 0)),
            pl.BlockSpec((tm // W_TILE, PEER_EB // PEER_KEYS, W_TILE, PEER_KEYS),
                         lambda i, j: (i, jnp.maximum(j - 1, 0), 0, 0)),
        ],
        out_specs=pl.BlockSpec((tm, D_MODEL), lambda i, j: (i, 0), pipeline_mode=pl.Buffered(1)),
        out_shape=jax.ShapeDtypeStruct((t, D_MODEL), F32),
        scratch_shapes=[pltpu.VMEM((tm, PEER_EB), F32)],
        compiler_params=_cparams(("parallel", "arbitrary")),
        name="peer",
    )(c, u, v, w3)


def _tail_kernel(h1_ref, peer_ref, pp_ref, ps_ref, gple_ref, wgate_ref, wproj_ref, gfin_ref,
                 yp_ref, ys_ref, *, prompt_tiles):
    i = pl.program_id(0)
    h2 = h1_ref[...] + peer_ref[...]
    gate = _sigmoid(_dot(_rmsnorm(h2, gple_ref[...]).astype(BF16), wgate_ref[...]))
    e = _dot(_group_pick(pp_ref, ps_ref, prompt_tiles).astype(BF16), wproj_ref[...])
    y = _rmsnorm(h2 + e * gate, gfin_ref[...])

    @pl.when(i < prompt_tiles)
    def _():
        yp_ref[...] = y

    @pl.when(i >= prompt_tiles)
    def _():
        ys_ref[...] = y


def _tail(h1, peer, p_p, p_s, gple, wgate, wproj, gfin, tm=512):
    t = h1.shape[0]
    n_prompt = p_p.shape[0]
    pt = n_prompt // tm
    ple = p_p.shape[1]
    return pl.pallas_call(
        functools.partial(_tail_kernel, prompt_tiles=pt),
        grid=(t // tm,),
        in_specs=[
            pl.BlockSpec((tm, D_MODEL), lambda i: (i, 0)),
            pl.BlockSpec((tm, D_MODEL), lambda i: (i, 0)),
        ] + _group_specs(tm, ple, pt) + [
            _resident((1, D_MODEL), lambda i: (0, 0)),
            _resident((D_MODEL, D_MODEL), lambda i: (0, 0)),
            _resident((ple, D_MODEL), lambda i: (0, 0)),
            _resident((1, D_MODEL), lambda i: (0, 0)),
        ],
        out_specs=[
            pl.BlockSpec((tm, D_MODEL), lambda i: (jnp.minimum(i, pt - 1), 0)),
            pl.BlockSpec((tm, D_MODEL), lambda i: (jnp.maximum(i - pt, 0), 0)),
        ],
        out_shape=[
            jax.ShapeDtypeStruct((n_prompt, D_MODEL), F32),
            jax.ShapeDtypeStruct((t - n_prompt, D_MODEL), F32),
        ],
        compiler_params=_cparams(("arbitrary",)),
        name="tail",
    )(h1, peer, p_p, p_s, gple, wgate, wproj, gfin)


def kernel(x_prompt, x_sample, state_mlstm_C, state_mlstm_n, state_mlstm_m, state_ssm_re, state_ssm_im, p_prompt, p_sample, norm_mix, w_in, b_igate, b_fgate, mlstm_norm, ssm_A_re, ssm_A_im, ssm_B_re, ssm_B_im, ssm_C_re, ssm_C_im, ssm_D, ssm_log_dt, w_glu, b_glu, w_out, norm_ffn, peer_w_q, peer_keys, peer_u, peer_v, norm_ple, w_ple_gate, w_ple_proj, norm_final):
    n_pseq, p_len, _ = x_prompt.shape
    n_sseq, s_len, _ = x_sample.shape
    assert s_len == SAMPLE_SEQ and n_pseq == S5_SEQS and w_in.shape[0] == 1
    n_prompt = n_pseq * p_len
    n_sample = n_sseq * s_len
    row = lambda t: t.reshape(1, -1)

    x_p = x_prompt.reshape(n_prompt, D_MODEL)
    x_s = x_sample.reshape(n_sample, D_MODEL)

    w_bf = w_in[0].astype(BF16)
    n_gate = 2 * HEADS
    w_tail = jnp.concatenate(
        [w_bf[:, Z_U + n_gate:], w_bf[:, Z_U:Z_U + n_gate],
         jnp.zeros((D_MODEL, 128 - n_gate), BF16)], axis=1)
    gbias = jnp.concatenate([b_igate[0], b_fgate[0], jnp.zeros((128 - n_gate,), F32)]).reshape(1, 128)
    z, zt = _in_proj(x_p, x_s, row(norm_mix[0]), w_bf, w_tail, gbias)

    gain_m = row(mlstm_norm[0])
    hm_p, c_p, n_p, m_p = _mlstm_prompt(z, zt, gain_m, n_pseq, p_len)
    nbh = n_sseq * HEADS
    hm_s, c_s, n_s, m_s = _mlstm_sample(
        z, zt, gain_m,
        state_mlstm_C[0].reshape(nbh, HEAD_DIM, HEAD_DIM),
        state_mlstm_n[0].reshape(nbh, 1, HEAD_DIM),
        jnp.broadcast_to(state_mlstm_m[0].reshape(nbh, 1, 1), (nbh, 1, 128)),
        n_prompt, n_sample)

    wb, wcre, wcim, ab_re, ab_im = _s5_params(ssm_A_re[0], ssm_A_im[0], ssm_log_dt[0], ssm_B_re[0],
                                              ssm_B_im[0], ssm_C_re[0], ssm_C_im[0])
    d_row = row(ssm_D[0])
    ys_p, sre_p, sim_p = _s5_prompt(zt, wb, wcre, wcim, ab_re, ab_im, d_row, p_len)
    n_st = SSM_GROUPS * SSM_STATE
    ys_s, sre_s, sim_s = _s5_sample(
        zt, state_ssm_re[0].reshape(n_sseq, n_st), state_ssm_im[0].reshape(n_sseq, n_st),
        wb, wcre, wcim, ab_re, ab_im, d_row, n_prompt, n_sample)

    keys = _blockdiag_keys(peer_keys[0].reshape(2 * PEER_HEADS, PEER_KEYS, PEER_HALF))
    h1, c, scores_p, scores_s = _postmix(
        hm_p, hm_s, ys_p.reshape(n_prompt, SSM_WIDTH), ys_s, x_p, x_s,
        w_glu[0].astype(BF16), row(b_glu[0]), w_out[0].astype(BF16),
        row(norm_ffn[0]), peer_w_q[0].astype(BF16), keys)
    w3 = _wbuild(_topk(scores_p), _topk(scores_s))
    peer = _peer(c, peer_u[0], peer_v[0], w3)

    y_p, y_s = _tail(h1, peer, p_prompt[0].reshape(n_prompt, -1), p_sample[0].reshape(n_sample, -1),
                     row(norm_ple[0]), w_ple_gate[0].astype(BF16), w_ple_proj[0].astype(BF16),
                     row(norm_final))

    st_shape = (1, -1, SSM_GROUPS, SSM_STATE)
    return (
        y_p.reshape(x_prompt.shape), y_s.reshape(x_sample.shape),
        c_p.reshape(1, n_pseq, HEADS, HEAD_DIM, HEAD_DIM), n_p.reshape(1, n_pseq, HEADS, HEAD_DIM),
        m_p[:, 0, 0].reshape(1, n_pseq, HEADS),
        sre_p.reshape(st_shape), sim_p.reshape(st_shape),
        c_s.reshape(1, n_sseq, HEADS, HEAD_DIM, HEAD_DIM), n_s.reshape(1, n_sseq, HEADS, HEAD_DIM),
        m_s[:, 0, 0].reshape(1, n_sseq, HEADS),
        sre_s.reshape(st_shape), sim_s.reshape(st_shape),
    )
```
